```python
import math, functools
import jax, jax.numpy as jnp
from jax import lax
import numpy as np

D_MODEL = 1024
BATCH = 32
SEQ = 2048
DEPTH = 1
DEC_BATCH = 128
DEC_SEQ = 1
PAST_LEN = 8192
PAGE_SIZE = 128

N_HEADS = 16
HEAD_DIM = 64
N_KV_HEADS = 4
N_IDX_HEADS = 8
IDX_DIM = 64
TOPK_MAX = 256
Q_BLOCK = 128
N_BUCKETS = 32
MAX_DISTANCE = 128
SSD_EXPAND = 2
D_INNER = SSD_EXPAND * D_MODEL
SSD_HEAD_DIM = 64
SSD_HEADS = D_INNER // SSD_HEAD_DIM
SSD_GROUPS = 4
D_STATE = 128
CONV_WIDTH = 4
SSD_CHUNK = 128
CONV_DIM = D_INNER + 2 * SSD_GROUPS * D_STATE
D_FF = 2816
N_SUBLAYERS = 3
EPS = 1e-6
ATT_Q = N_HEADS * HEAD_DIM
ATT_KV = N_KV_HEADS * HEAD_DIM
IDX_Q = N_IDX_HEADS * IDX_DIM
IN_WIDTHS = (ATT_Q, ATT_KV, ATT_KV, IDX_Q, IDX_DIM, N_IDX_HEADS, D_INNER, CONV_DIM, SSD_HEADS, D_MODEL, D_MODEL)
IN_DIM = ATT_Q + 2 * ATT_KV + IDX_Q + IDX_DIM + N_IDX_HEADS + D_INNER + CONV_DIM + SSD_HEADS + 2 * D_MODEL

kernel_name = 'hybrid_dsa_ssd_macaron_decoder_step'


def split_points():
    return np.cumsum(np.array(IN_WIDTHS))[:-1].tolist()


def rmsnorm(x, g):
    xf = x.astype(jnp.float32)
    y = xf * lax.rsqrt(jnp.mean(xf * xf, axis=-1, keepdims=True) + EPS)
    return (y * g.astype(jnp.float32)).astype(x.dtype)


def modulate(h, shift, scale):
    return h * (1.0 + scale) + shift


def swiglu(h, w_in, w_out):
    gate, up = jnp.split(h @ w_in, 2, axis=-1)
    return (jax.nn.silu(gate) * up) @ w_out


def t5_bucket(dist):
    n = jnp.maximum(dist, 0)
    max_exact = N_BUCKETS // 2
    nf = jnp.maximum(n, 1).astype(jnp.float32)
    large = max_exact + (jnp.log(nf / max_exact) / math.log(MAX_DISTANCE / max_exact)
                         * (N_BUCKETS - max_exact)).astype(jnp.int32)
    large = jnp.minimum(large, N_BUCKETS - 1)
    return jnp.where(n < max_exact, n, large)


def dsa_queries(q, qi, wi, qpos, ki_all, fetch_kv, n_top, rel_bias):
    L = ki_all.shape[0]
    T = q.shape[0]
    kpos = jnp.arange(L, dtype=jnp.int32)
    dots = jnp.einsum('thd,sd->ths', qi, ki_all).astype(jnp.float32)
    score = jnp.einsum('ths,th->ts', jax.nn.relu(dots), wi.astype(jnp.float32)) * ((N_IDX_HEADS * IDX_DIM) ** -0.5)
    score = jnp.where(kpos[None, :] <= qpos[:, None], score, -jnp.inf)
    _, idx = lax.top_k(score, n_top)
    valid = idx <= qpos[:, None]
    k_sel, v_sel = fetch_kv(idx)
    qg = q.reshape(T, N_KV_HEADS, N_HEADS // N_KV_HEADS, HEAD_DIM)
    logits = jnp.einsum('tngd,tjnd->tngj', qg, k_sel).astype(jnp.float32) * (HEAD_DIM ** -0.5)
    bias = rel_bias[t5_bucket(qpos[:, None] - idx)]
    bias = jnp.transpose(bias, (0, 2, 1)).reshape(T, N_KV_HEADS, N_HEADS // N_KV_HEADS, n_top)
    logits = jnp.where(valid[:, None, None, :], logits + bias.astype(jnp.float32), -jnp.inf)
    p = jax.nn.softmax(logits, axis=-1).astype(v_sel.dtype)
    o = jnp.einsum('tngj,tjnd->tngd', p, v_sel)
    return o.reshape(T, N_HEADS * HEAD_DIM)


def dsa_prompt(q, k, v, qi, ki, wi, rel_bias):
    S = q.shape[1]
    n_top = min(TOPK_MAX, S // 4)
    nblk = S // Q_BLOCK
    qpos = jnp.arange(S, dtype=jnp.int32).reshape(nblk, Q_BLOCK)

    def per_seq(args):
        q_b, k_b, v_b, qi_b, ki_b, wi_b = args
        fetch = lambda idx: (k_b[idx], v_b[idx])

        def per_block(bargs):
            qq, qqi, ww, pp = bargs
            return dsa_queries(qq, qqi, ww, pp, ki_b, fetch, n_top, rel_bias)

        out = lax.map(per_block, (q_b.reshape(nblk, Q_BLOCK, N_HEADS, HEAD_DIM),
                                  qi_b.reshape(nblk, Q_BLOCK, N_IDX_HEADS, IDX_DIM),
                                  wi_b.reshape(nblk, Q_BLOCK, N_IDX_HEADS), qpos))
        return out.reshape(S, N_HEADS * HEAD_DIM)

    return lax.map(per_seq, (q, k, v, qi, ki, wi))


def dsa_sample(q, k_new, v_new, qi, ki_new, wi, cache_k, cache_v, cache_kidx, page_table, rel_bias):
    T = q.shape[1]
    n_pages = page_table.shape[1]
    past = n_pages * PAGE_SIZE
    n_top = min(TOPK_MAX, (past + T) // 4)
    qpos = past + jnp.arange(T, dtype=jnp.int32)

    def per_seq(args):
        q_b, kn, vn, qi_b, kin, wi_b, pt = args
        ki_all = jnp.concatenate([cache_kidx[pt].reshape(past, IDX_DIM), kin], axis=0)

        def fetch(idx):
            in_past = (idx < past)[..., None, None]
            page = pt[jnp.minimum(idx // PAGE_SIZE, n_pages - 1)]
            off = idx % PAGE_SIZE
            j = jnp.clip(idx - past, 0, T - 1)
            return (jnp.where(in_past, cache_k[page, off], kn[j]),
                    jnp.where(in_past, cache_v[page, off], vn[j]))

        return dsa_queries(q_b, qi_b, wi_b, qpos, ki_all, fetch, n_top, rel_bias)

    return lax.map(per_seq, (q, k_new, v_new, qi, ki_new, wi, page_table))


def causal_conv(xbc, conv_state, w, bias):
    full = jnp.concatenate([conv_state.astype(xbc.dtype), xbc], axis=1)
    out = lax.conv_general_dilated(full, w.astype(full.dtype)[:, None, :], window_strides=(1,), padding='VALID',
                                   dimension_numbers=('NWC', 'WIO', 'NWC'), feature_group_count=CONV_DIM)
    return jax.nn.silu(out + bias), full[:, -(CONV_WIDTH - 1):]


def ssd_scan(x, dt, A, Bm, Cm, h0):
    b, T = x.shape[:2]
    R = SSD_HEADS // SSD_GROUPS
    cl = min(SSD_CHUNK, T)
    nc = -(-T // cl)
    pad = nc * cl - T
    if pad:
        padT = lambda a: jnp.pad(a, [(0, 0), (0, pad)] + [(0, 0)] * (a.ndim - 2))
        x, dt, Bm, Cm = padT(x), padT(dt), padT(Bm), padT(Cm)
    Tp = nc * cl
    to_chunks = lambda a: jnp.moveaxis(a.reshape(b, nc, cl, *a.shape[2:]), 1, 0)
    xs = to_chunks(x.reshape(b, Tp, SSD_GROUPS, R, SSD_HEAD_DIM))
    dts = to_chunks(dt.reshape(b, Tp, SSD_GROUPS, R))
    Bs, Cs = to_chunks(Bm), to_chunks(Cm)
    Ar = A.reshape(SSD_GROUPS, R)
    tril = jnp.tril(jnp.ones((cl, cl), dtype=bool))[None, :, :, None, None]

    def step(h, inp):
        xc, dtc, Bc, Cc = inp
        acs = jnp.cumsum(dtc * Ar, axis=1)
        seg = acs[:, :, None] - acs[:, None, :]
        decay = jnp.exp(jnp.where(tril, seg, -jnp.inf))
        xdt = xc * dtc[..., None]
        cb = jnp.einsum('blgn,bsgn->blsg', Cc, Bc)
        y_diag = jnp.einsum('blsg,blsgr,bsgrp->blgrp', cb, decay, xdt)
        y_off = jnp.einsum('blgn,bgrpn,blgr->blgrp', Cc, h, jnp.exp(acs))
        to_end = jnp.exp(acs[:, -1:] - acs)
        h_new = jnp.exp(acs[:, -1])[..., None, None] * h + jnp.einsum('bsgn,bsgr,bsgrp->bgrpn', Bc, to_end, xdt)
        return h_new, y_diag + y_off

    h_fin, ys = lax.scan(step, h0.reshape(b, SSD_GROUPS, R, SSD_HEAD_DIM, D_STATE), (xs, dts, Bs, Cs))
    y = jnp.moveaxis(ys, 0, 1).reshape(b, Tp, SSD_HEADS, SSD_HEAD_DIM)[:, :T]
    return y, h_fin.reshape(b, SSD_HEADS, SSD_HEAD_DIM, D_STATE)


def ssd_branch(z, xbc, dt_raw, conv_state, ssm_state, p):
    b, T = z.shape[:2]
    xbc_c, conv_new = causal_conv(xbc, conv_state, p['conv_w'], p['conv_b'])
    xs, Bm, Cm = jnp.split(xbc_c.astype(jnp.float32), [D_INNER, D_INNER + SSD_GROUPS * D_STATE], axis=-1)
    xs = xs.reshape(b, T, SSD_HEADS, SSD_HEAD_DIM)
    Bm = Bm.reshape(b, T, SSD_GROUPS, D_STATE)
    Cm = Cm.reshape(b, T, SSD_GROUPS, D_STATE)
    dt = jax.nn.softplus(dt_raw.astype(jnp.float32) + p['dt_bias'].astype(jnp.float32))
    A = -jnp.exp(p['a_log'].astype(jnp.float32))
    y, h_new = ssd_scan(xs, dt, A, Bm, Cm, ssm_state.astype(jnp.float32))
    y = y + p['d_skip'].astype(jnp.float32)[:, None] * xs
    y = y.reshape(b, T, D_INNER) * jax.nn.silu(z.astype(jnp.float32))
    yg = y.reshape(b, T, SSD_GROUPS, D_INNER // SSD_GROUPS)
    yg = yg * lax.rsqrt(jnp.mean(yg * yg, axis=-1, keepdims=True) + EPS)
    y = yg.reshape(b, T, D_INNER) * p['norm_ssd'].astype(jnp.float32)
    return y.astype(z.dtype), h_new, conv_new


def trunk_layer(x, c, p, attend, conv_state, ssm_state):
    ada = jax.nn.silu(c) @ p['w_ada'] + p['b_ada']
    sh1, sc1, g1, sh2, sc2, g2, sh3, sc3, g3 = [a[:, None, :] for a in jnp.split(ada, 3 * N_SUBLAYERS, axis=-1)]
    h = modulate(rmsnorm(x, p['norm_ffn1']), sh1, sc1)
    x = x + 0.5 * g1 * swiglu(h, p['w_ffn1_in'], p['w_ffn1_out'])
    h = modulate(rmsnorm(x, p['norm_mix']), sh2, sc2)
    b, T, _ = h.shape
    q, k, v, qi, ki, wi, z, xbc, dt_raw, gate_a, gate_s = jnp.split(h @ p['w_in'], split_points(), axis=-1)
    q = q.reshape(b, T, N_HEADS, HEAD_DIM)
    k = k.reshape(b, T, N_KV_HEADS, HEAD_DIM)
    v = v.reshape(b, T, N_KV_HEADS, HEAD_DIM)
    qi = qi.reshape(b, T, N_IDX_HEADS, IDX_DIM)
    att = attend(q, k, v, qi, ki, wi)
    ssd_y, ssm_new, conv_new = ssd_branch(z, xbc, dt_raw, conv_state, ssm_state, p)
    merged = jax.nn.sigmoid(gate_a) * (att @ p['w_attn_out']) + jax.nn.sigmoid(gate_s) * (ssd_y @ p['w_ssd_out'])
    x = x + g2 * (merged @ p['w_out'])
    h = modulate(rmsnorm(x, p['norm_ffn2']), sh3, sc3)
    x = x + 0.5 * g3 * swiglu(h, p['w_ffn2_in'], p['w_ffn2_out'])
    return x, (k, v, ki, ssm_new, conv_new)


def setup_inputs(seed: int = 0) -> dict:
    key = jax.random.key(seed)
    ks = jax.random.split(key, 32)
    f32 = jnp.float32
    nrm = lambda kk, shape, scale: jax.random.normal(kk, shape, f32) * scale
    n_pages = PAST_LEN // PAGE_SIZE
    n_pool = (DEC_BATCH * n_pages * 5) // 4
    page_table = jax.random.permutation(ks[0], n_pool)[:DEC_BATCH * n_pages].reshape(DEC_BATCH, n_pages).astype(jnp.int32)
    dt0 = jnp.exp(jax.random.uniform(ks[1], (DEPTH, SSD_HEADS), f32) * (math.log(0.1) - math.log(0.001)) + math.log(0.001))
    dt_bias = dt0 + jnp.log(-jnp.expm1(-dt0))
    a_log = jnp.log(jax.random.uniform(ks[2], (DEPTH, SSD_HEADS), f32, 1.0, 16.0))
    gain = lambda kk, n: 1.0 + nrm(kk, (DEPTH, n), 0.02)
    return {
        'x_prompt': nrm(ks[3], (BATCH, SEQ, D_MODEL), 1.0),
        'x_sample': nrm(ks[4], (DEC_BATCH, DEC_SEQ, D_MODEL), 1.0),
        'c_prompt': nrm(ks[5], (BATCH, D_MODEL), 1.0),
        'c_sample': nrm(ks[6], (DEC_BATCH, D_MODEL), 1.0),
        'cache_k': nrm(ks[7], (DEPTH, n_pool, PAGE_SIZE, N_KV_HEADS, HEAD_DIM), 1.0),
        'cache_v': nrm(ks[8], (DEPTH, n_pool, PAGE_SIZE, N_KV_HEADS, HEAD_DIM), 1.0),
        'cache_kidx': nrm(ks[9], (DEPTH, n_pool, PAGE_SIZE, IDX_DIM), 1.0),
        'state_ssm': nrm(ks[10], (DEPTH, DEC_BATCH, SSD_HEADS, SSD_HEAD_DIM, D_STATE), 0.1),
        'state_conv': nrm(ks[11], (DEPTH, DEC_BATCH, CONV_WIDTH - 1, CONV_DIM), 1.0),
        'page_table': page_table,
        'w_ada': nrm(ks[12], (DEPTH, D_MODEL, 3 * N_SUBLAYERS * D_MODEL), D_MODEL ** -0.5),
        'b_ada': nrm(ks[13], (DEPTH, 3 * N_SUBLAYERS * D_MODEL), 0.02),
        'norm_ffn1': gain(ks[14], D_MODEL),
        'w_ffn1_in': nrm(ks[15], (DEPTH, D_MODEL, 2 * D_FF), D_MODEL ** -0.5),
        'w_ffn1_out': nrm(ks[16], (DEPTH, D_FF, D_MODEL), D_FF ** -0.5),
        'norm_mix': gain(ks[17], D_MODEL),
        'w_in': nrm(ks[18], (DEPTH, D_MODEL, IN_DIM), D_MODEL ** -0.5),
        'rel_bias': nrm(ks[19], (N_BUCKETS, N_HEADS), 0.5),
        'conv_w': nrm(ks[20], (DEPTH, CONV_WIDTH, CONV_DIM), CONV_WIDTH ** -0.5),
        'conv_b': nrm(ks[21], (DEPTH, CONV_DIM), 0.02),
        'a_log': a_log,
        'dt_bias': dt_bias,
        'd_skip': 1.0 + nrm(ks[22], (DEPTH, SSD_HEADS), 0.1),
        'norm_ssd': gain(ks[23], D_INNER),
        'w_attn_out': nrm(ks[24], (DEPTH, ATT_Q, D_MODEL), ATT_Q ** -0.5),
        'w_ssd_out': nrm(ks[25], (DEPTH, D_INNER, D_MODEL), D_INNER ** -0.5),
        'w_out': nrm(ks[26], (DEPTH, D_MODEL, D_MODEL), D_MODEL ** -0.5),
        'norm_ffn2': gain(ks[27], D_MODEL),
        'w_ffn2_in': nrm(ks[28], (DEPTH, D_MODEL, 2 * D_FF), D_MODEL ** -0.5),
        'w_ffn2_out': nrm(ks[29], (DEPTH, D_FF, D_MODEL), D_FF ** -0.5),
        'norm_final': 1.0 + nrm(ks[30], (D_MODEL,), 0.02),
    }


def reference(x_prompt, x_sample, c_prompt, c_sample, cache_k, cache_v, cache_kidx, state_ssm, state_conv, page_table,
              w_ada, b_ada, norm_ffn1, w_ffn1_in, w_ffn1_out, norm_mix, w_in, rel_bias, conv_w, conv_b, a_log, dt_bias,
              d_skip, norm_ssd, w_attn_out, w_ssd_out, w_out, norm_ffn2, w_ffn2_in, w_ffn2_out, norm_final):
    yp, ys = x_prompt, x_sample
    bp = x_prompt.shape[0]
    zero_conv = jnp.zeros((bp, CONV_WIDTH - 1, CONV_DIM), x_prompt.dtype)
    zero_ssm = jnp.zeros((bp, SSD_HEADS, SSD_HEAD_DIM, D_STATE), jnp.float32)
    outs_p, outs_s = [], []
    for l in range(DEPTH):
        p = {'w_ada': w_ada[l], 'b_ada': b_ada[l], 'norm_ffn1': norm_ffn1[l], 'w_ffn1_in': w_ffn1_in[l],
             'w_ffn1_out': w_ffn1_out[l], 'norm_mix': norm_mix[l], 'w_in': w_in[l], 'conv_w': conv_w[l],
             'conv_b': conv_b[l], 'a_log': a_log[l], 'dt_bias': dt_bias[l], 'd_skip': d_skip[l],
             'norm_ssd': norm_ssd[l], 'w_attn_out': w_attn_out[l], 'w_ssd_out': w_ssd_out[l], 'w_out': w_out[l],
             'norm_ffn2': norm_ffn2[l], 'w_ffn2_in': w_ffn2_in[l], 'w_ffn2_out': w_ffn2_out[l]}
        attend_p = functools.partial(dsa_prompt, rel_bias=rel_bias)
        attend_s = functools.partial(dsa_sample, cache_k=cache_k[l], cache_v=cache_v[l], cache_kidx=cache_kidx[l],
                                     page_table=page_table, rel_bias=rel_bias)
        yp, st_p = trunk_layer(yp, c_prompt, p, attend_p, zero_conv, zero_ssm)
        ys, st_s = trunk_layer(ys, c_sample, p, attend_s, state_conv[l], state_ssm[l])
        outs_p.append(st_p)
        outs_s.append(st_s)
    y_prompt = rmsnorm(yp, norm_final)
    y_sample = rmsnorm(ys, norm_final)
    stack = lambda outs, i: jnp.stack([o[i] for o in outs], axis=0)
    k_prompt, v_prompt, kidx_prompt = stack(outs_p, 0), stack(outs_p, 1), stack(outs_p, 2)
    ssm_prompt, conv_prompt = stack(outs_p, 3), stack(outs_p, 4)
    k_sample, v_sample, kidx_sample = stack(outs_s, 0), stack(outs_s, 1), stack(outs_s, 2)
    ssm_sample, conv_sample = stack(outs_s, 3), stack(outs_s, 4)
    return (y_prompt, y_sample, k_prompt, v_prompt, kidx_prompt, ssm_prompt, conv_prompt,
            k_sample, v_sample, kidx_sample, ssm_sample, conv_sample)
```

```python
import functools
import math

import numpy as np
import jax
import jax.numpy as jnp
from jax import lax
from jax.experimental import pallas as pl
from jax.experimental.pallas import tpu as pltpu

N_HEADS = 16
HEAD_DIM = 64
N_KV_HEADS = 4
N_IDX_HEADS = 8
IDX_DIM = 64
TOPK_MAX = 256
N_BUCKETS = 32
MAX_DISTANCE = 128
SSD_HEAD_DIM = 64
SSD_GROUPS = 4
D_STATE = 128
SSD_CHUNK = 128
EPS = 1e-6
PAGE_SIZE = 128

LANES = 128
SUBLANES = 8
VMEM_LIMIT = 56 * 1024 * 1024

F32 = jnp.float32
BF16 = jnp.bfloat16
I32 = jnp.int32
NEG_INF = float("-inf")
INT_MIN = -(2 ** 31)

_NT = (((1,), (1,)), ((), ()))


def _dot(a, b):
    return jnp.dot(a, b, preferred_element_type=F32)


def _dot_nt(a, b):
    return lax.dot_general(a, b, _NT, preferred_element_type=F32)


def _split3(a):
    hi = a.astype(BF16)
    r = a - hi.astype(F32)
    mid = r.astype(BF16)
    lo = (r - mid.astype(F32)).astype(BF16)
    return hi, mid, lo


def _dot_exact_rhs(a, b_bf16):
    hi, mid, lo = _split3(a)
    return _dot(hi, b_bf16) + _dot(mid, b_bf16) + _dot(lo, b_bf16)


def _dot_exact_lhs(a_bf16, b):
    hi, mid, lo = _split3(b)
    return _dot(a_bf16, hi) + _dot(a_bf16, mid) + _dot(a_bf16, lo)


def _rmsnorm(x, g):
    return (x * lax.rsqrt(jnp.mean(x * x, axis=-1, keepdims=True) + EPS)) * g


def _silu(x):
    return x * jax.nn.sigmoid(x)


def _params(*sem):
    return pltpu.CompilerParams(dimension_semantics=sem, vmem_limit_bytes=VMEM_LIMIT)


def _t5_bucket_np(dist):
    n = np.maximum(dist, 0)
    max_exact = N_BUCKETS // 2
    nf = np.maximum(n, 1).astype(np.float32)
    val = (np.log(nf / np.float32(max_exact)) / np.float32(math.log(MAX_DISTANCE / max_exact))
           * np.float32(N_BUCKETS - max_exact)).astype(np.float32)
    frac = np.abs(val - np.round(val))
    knife = (frac < 1e-3) & (n > max_exact) & (val < N_BUCKETS - max_exact - 0.5)
    assert not knife.any()
    large = np.minimum(max_exact + val.astype(np.int32), N_BUCKETS - 1)
    return np.where(n < max_exact, n, large).astype(np.int32)


def _ada_kernel(c_ref, w_ref, b_ref, o_ref):
    h = _silu(c_ref[...]).astype(BF16)
    o_ref[...] = _dot(h, w_ref[...].astype(BF16)) + b_ref[...]


def _ada(c, w, b):
    rows, d = c.shape
    n = w.shape[1]
    tn = 1024
    return pl.pallas_call(
        _ada_kernel,
        grid=(n // tn,),
        in_specs=[pl.BlockSpec((rows, d), lambda j: (0, 0)),
                  pl.BlockSpec((d, tn), lambda j: (0, j)),
                  pl.BlockSpec((1, tn), lambda j: (0, j))],
        out_specs=pl.BlockSpec((rows, tn), lambda j: (0, j)),
        out_shape=jax.ShapeDtypeStruct((rows, n), F32),
        compiler_params=_params("arbitrary"),
    )(c, w, b.reshape(1, n))


class _Mod:
    def __init__(self, arr, rows_per_seq):
        self.arr = arr
        self.rows_per_seq = rows_per_seq

    def spec(self, tm):
        r = self.arr.shape[1]
        d = self.arr.shape[2]
        if r == 1:
            per = self.rows_per_seq // tm
            return pl.BlockSpec((1, 1, d), lambda i, *_: (i // per, 0, 0))
        assert r == tm
        return pl.BlockSpec((1, r, d), lambda i, *_: (i, 0, 0))


def _ffn_kernel(x_ref, sh_ref, sc_ref, gt_ref, g_ref, wg_ref, wu_ref, wo_ref, fg_ref, o_ref, h_scr, acc_scr,
                *, final_norm):
    j = pl.program_id(1)

    @pl.when(j == 0)
    def _():
        h = _rmsnorm(x_ref[...], g_ref[...]) * (1.0 + sc_ref[0]) + sh_ref[0]
        h_scr[...] = h.astype(BF16)
        acc_scr[...] = jnp.zeros_like(acc_scr)

    h = h_scr[...]
    gate = _dot(h, wg_ref[...])
    up = _dot(h, wu_ref[...])
    act = (_silu(gate) * up).astype(BF16)
    acc_scr[...] += _dot(act, wo_ref[...])

    @pl.when(j == pl.num_programs(1) - 1)
    def _():
        out = x_ref[...] + 0.5 * gt_ref[0] * acc_scr[...]
        if final_norm:
            out = _rmsnorm(out, fg_ref[...])
        o_ref[...] = out


def _ffn(x, shift, scale, gate, norm_g, w_in, w_out, final_g, *, tm, final_norm):
    m, d = x.shape
    f = w_out.shape[0]
    nj = 2
    tf = f // nj
    assert tf % LANES == 0 and m % tm == 0
    row = lambda i, j: (i, 0)
    const = lambda i, j: (0, 0)
    return pl.pallas_call(
        functools.partial(_ffn_kernel, final_norm=final_norm),
        grid=(m // tm, nj),
        in_specs=[pl.BlockSpec((tm, d), row),
                  shift.spec(tm), scale.spec(tm), gate.spec(tm),
                  pl.BlockSpec((1, d), const),
                  pl.BlockSpec((d, tf), lambda i, j: (0, j)),
                  pl.BlockSpec((d, tf), lambda i, j: (0, j + nj)),
                  pl.BlockSpec((tf, d), lambda i, j: (j, 0)),
                  pl.BlockSpec((1, d), const)],
        out_specs=pl.BlockSpec((tm, d), row),
        out_shape=jax.ShapeDtypeStruct((m, d), F32),
        scratch_shapes=[pltpu.VMEM((tm, d), BF16), pltpu.VMEM((tm, d), F32)],
        compiler_params=_params("parallel", "arbitrary"),
    )(x, shift.arr, scale.arr, gate.arr, norm_g, w_in, w_in, w_out, final_g)


def _proj_kernel(x_ref, sh_ref, sc_ref, g_ref, w_ref, *o_refs, offsets):
    h = (_rmsnorm(x_ref[...], g_ref[...]) * (1.0 + sc_ref[0]) + sh_ref[0]).astype(BF16)
    for o_ref, off in zip(o_refs, offsets):
        width = o_ref.shape[1]
        o_ref[...] = _dot(h, w_ref[:, off:off + width]).astype(o_ref.dtype)


def _proj(x, shift, scale, norm_g, w, segments, *, tm):
    m, d = x.shape
    offsets, off = [], 0
    for width, _ in segments:
        offsets.append(off)
        off += -(-width // LANES) * LANES
    assert off == w.shape[1]
    row = lambda i: (i, 0)
    const = lambda i: (0, 0)
    return pl.pallas_call(
        functools.partial(_proj_kernel, offsets=tuple(offsets)),
        grid=(m // tm,),
        in_specs=[pl.BlockSpec((tm, d), row), shift.spec(tm), scale.spec(tm),
                  pl.BlockSpec((1, d), const), pl.BlockSpec(w.shape, const)],
        out_specs=[pl.BlockSpec((tm, width), row) for width, _ in segments],
        out_shape=[jax.ShapeDtypeStruct((m, width), dt) for width, dt in segments],
        compiler_params=_params("parallel"),
    )(x, shift.arr, scale.arr, norm_g, w)


def _sort_key(score):
    score = jnp.where(score == 0.0, 0.0, score)
    bits = pltpu.bitcast(score, I32)
    return jnp.where(bits >= 0, bits, bits ^ jnp.int32(0x7FFFFFFF))


def _radix_threshold(count_ge, n_top, shape):
    def bit_body(t, thr_u):
        cand_u = thr_u | jnp.left_shift(jnp.int32(1), 31 - t)
        cnt = count_ge(cand_u ^ jnp.int32(INT_MIN))
        return jnp.where(cnt >= n_top, cand_u, thr_u)

    thr_u = lax.fori_loop(0, 32, bit_body, jnp.zeros(shape, I32))
    return thr_u ^ jnp.int32(INT_MIN)


def _bias_table_kernel(rb_ref, bkt_ref, o_ref):
    far = N_BUCKETS - 1
    for slot in range(3):
        bkt = bkt_ref[slot]
        for h in range(N_HEADS):
            acc = jnp.zeros(bkt.shape, F32)
            for b in range(N_BUCKETS):
                acc = jnp.where(bkt == b, rb_ref[b, h] - rb_ref[far, h], acc)
            o_ref[slot, h] = acc


def _bias_table(rel_bias, tq):
    t = np.arange(tq)[:, None]
    c = np.arange(tq)[None, :]
    bkt = np.stack([np.full((tq, tq), N_BUCKETS - 1, np.int32),
                    _t5_bucket_np(t + tq - c), _t5_bucket_np(t - c)])
    assert _t5_bucket_np(np.array([tq + 1]))[0] == N_BUCKETS - 1
    return pl.pallas_call(
        _bias_table_kernel,
        in_specs=[pl.BlockSpec(memory_space=pltpu.SMEM), pl.BlockSpec(memory_space=pltpu.VMEM)],
        out_specs=pl.BlockSpec(memory_space=pltpu.VMEM),
        out_shape=jax.ShapeDtypeStruct((3, N_HEADS, tq, tq), F32),
    )(rel_bias, jnp.asarray(bkt))


def _dsa_prompt_kernel(q_ref, qi_ref, wi_ref, k_ref, v_ref, ki_ref, bias_ref, o_ref,
                       kb_scr, vb_scr, kib_scr, key_scr, mb_scr, lg_scr, m_scr, l_scr, acc_scr,
                       *, n_top, tq):
    i = pl.program_id(1)
    nq = N_HEADS // N_KV_HEADS
    row = lax.broadcasted_iota(I32, (tq, tq), 0)
    col = lax.broadcasted_iota(I32, (tq, tq), 1)
    causal = col <= row

    @pl.when(i == 0)
    def _():
        kb_scr[...] = k_ref[...].astype(BF16)
        vb_scr[...] = v_ref[...].astype(BF16)
        kib_scr[...] = ki_ref[...].astype(BF16)

    def chunk(j):
        return pl.ds(pl.multiple_of(j * tq, tq), tq)

    qi = qi_ref[...]
    qi_st = jnp.concatenate([qi[:, h * IDX_DIM:(h + 1) * IDX_DIM] for h in range(N_IDX_HEADS)], axis=0)
    wi = wi_ref[...]
    wi_st = jnp.concatenate([jnp.broadcast_to(wi[:, h:h + 1], (tq, tq)) for h in range(N_IDX_HEADS)], axis=0)
    idx_scale = (N_IDX_HEADS * IDX_DIM) ** -0.5

    def score_body(j, carry):
        d = _dot_nt(qi_st, kib_scr[chunk(j), :])
        d = jnp.maximum(d, 0.0) * wi_st
        s = jnp.sum(d.reshape(N_IDX_HEADS, tq, tq), axis=0) * idx_scale
        s = jnp.where((j < i) | causal, s, NEG_INF)
        key_scr[j] = _sort_key(s)
        return carry

    lax.fori_loop(0, i + 1, score_body, 0)

    def count_ge(cand):
        def body(j, c):
            return c + (key_scr[j] >= cand).astype(I32)
        c = lax.fori_loop(0, i + 1, body, jnp.zeros((tq, tq), I32))
        return jnp.sum(c, axis=1, keepdims=True)

    thr = _radix_threshold(count_ge, n_top, (tq, 1))

    def gt_body(j, c):
        return c + (key_scr[j] > thr).astype(I32)
    n_gt = jnp.sum(lax.fori_loop(0, i + 1, gt_body, jnp.zeros((tq, tq), I32)), axis=1, keepdims=True)
    need = (n_top - n_gt).astype(F32)

    triu = (row <= col).astype(BF16)

    def sel_body(j, run_eq):
        key = key_scr[j]
        eq = key == thr
        pre = _dot(eq.astype(BF16), triu)
        sel = (key > thr) | (eq & (run_eq + pre <= need))
        sel = sel & ((j < i) | causal)
        mb_scr[j] = jnp.where(sel, 0.0, NEG_INF)
        return run_eq + pre[:, tq - 1:tq]

    lax.fori_loop(0, i + 1, sel_body, jnp.zeros((tq, 1), F32))

    q = q_ref[...]
    for n in range(N_KV_HEADS):
        q_st = jnp.concatenate([q[:, (n * nq + g) * HEAD_DIM:(n * nq + g + 1) * HEAD_DIM] for g in range(nq)],
                               axis=0) * (HEAD_DIM ** -0.5)
        m_scr[...] = jnp.full(m_scr.shape, NEG_INF, F32)
        l_scr[...] = jnp.zeros_like(l_scr)
        acc_scr[...] = jnp.zeros_like(acc_scr)

        def logit_body(j, carry):
            lg = _dot_nt(q_st, kb_scr[chunk(j), n * HEAD_DIM:(n + 1) * HEAD_DIM])
            slot = jnp.clip(j - i + 2, 0, 2)
            lg = lg.reshape(nq, tq, tq) + bias_ref[slot, n * nq:(n + 1) * nq] + mb_scr[j][None]
            lg = lg.reshape(nq * tq, tq)
            lg_scr[j] = lg
            m_scr[...] = jnp.maximum(m_scr[...], lg)
            return carry

        lax.fori_loop(0, i + 1, logit_body, 0)
        m_row = jnp.max(m_scr[...], axis=1, keepdims=True)

        def pv_body(j, carry):
            p = jnp.exp(lg_scr[j] - m_row)
            l_scr[...] += p
            acc_scr[...] += _dot(p.astype(BF16), vb_scr[chunk(j), n * HEAD_DIM:(n + 1) * HEAD_DIM])
            return carry

        lax.fori_loop(0, i + 1, pv_body, 0)
        out = acc_scr[...] / jnp.sum(l_scr[...], axis=1, keepdims=True)
        for g in range(nq):
            h = n * nq + g
            o_ref[:, h * HEAD_DIM:(h + 1) * HEAD_DIM] = out[g * tq:(g + 1) * tq].astype(o_ref.dtype)


def _dsa_prompt(q, qi, wi, k, v, ki, bias_tab, *, batch, seq, tq):
    nblk = seq // tq
    n_top = min(TOPK_MAX, seq // 4)
    nq = N_HEADS // N_KV_HEADS
    blk = lambda b, i: (b * nblk + i, 0)
    whole = lambda b, i: (b, 0)
    kvw = N_KV_HEADS * HEAD_DIM
    return pl.pallas_call(
        functools.partial(_dsa_prompt_kernel, n_top=n_top, tq=tq),
        grid=(batch, nblk),
        in_specs=[pl.BlockSpec((tq, N_HEADS * HEAD_DIM), blk),
                  pl.BlockSpec((tq, N_IDX_HEADS * IDX_DIM), blk),
                  pl.BlockSpec((tq, LANES), blk),
                  pl.BlockSpec((seq, kvw), whole),
                  pl.BlockSpec((seq, kvw), whole),
                  pl.BlockSpec((seq, IDX_DIM), whole),
                  pl.BlockSpec(bias_tab.shape, lambda b, i: (0, 0, 0, 0))],
        out_specs=pl.BlockSpec((tq, N_HEADS * HEAD_DIM), blk),
        out_shape=jax.ShapeDtypeStruct((batch * seq, N_HEADS * HEAD_DIM), BF16),
        scratch_shapes=[pltpu.VMEM((seq, kvw), BF16), pltpu.VMEM((seq, kvw), BF16),
                        pltpu.VMEM((seq, IDX_DIM), BF16),
                        pltpu.VMEM((nblk, tq, tq), I32), pltpu.VMEM((nblk, tq, tq), F32),
                        pltpu.VMEM((nblk, nq * tq, tq), F32),
                        pltpu.VMEM((nq * tq, tq), F32), pltpu.VMEM((nq * tq, tq), F32),
                        pltpu.VMEM((nq * tq, HEAD_DIM), F32)],
        compiler_params=_params("arbitrary", "arbitrary"),
    )(q, qi, wi, k, v, ki, bias_tab)


def _dsa_s_score_kernel(pt_ref, qi_ref, wi_ref, *refs, pg):
    page_refs, o_ref = refs[:pg], refs[pg]
    qi = qi_ref[0].astype(BF16)
    wi = wi_ref[0]
    idx_scale = (N_IDX_HEADS * IDX_DIM) ** -0.5
    for r in range(pg):
        d = _dot_nt(qi, page_refs[r][0].astype(BF16))
        s = jnp.sum(jnp.maximum(d, 0.0) * wi, axis=0, keepdims=True) * idx_scale
        o_ref[0, r:r + 1, :] = s


def _dsa_s_scores(page_table, qi, wi_bc, cache_kidx, *, pg):
    db, n_pages = page_table.shape
    g = n_pages // pg
    page_spec = lambda r: pl.BlockSpec(
        (1, PAGE_SIZE, IDX_DIM), lambda b, s, pt: (pt[b * n_pages + s * pg + r], 0, 0))
    grid_spec = pltpu.PrefetchScalarGridSpec(
        num_scalar_prefetch=1,
        grid=(db, g),
        in_specs=[pl.BlockSpec((1, N_IDX_HEADS, IDX_DIM), lambda b, s, pt: (b, 0, 0)),
                  pl.BlockSpec((1, N_IDX_HEADS, LANES), lambda b, s, pt: (b, 0, 0))]
                 + [page_spec(r) for r in range(pg)],
        out_specs=pl.BlockSpec((1, pg, PAGE_SIZE), lambda b, s, pt: (b, s, 0)),
    )
    return pl.pallas_call(
        functools.partial(_dsa_s_score_kernel, pg=pg),
        grid_spec=grid_spec,
        out_shape=jax.ShapeDtypeStruct((db, n_pages, PAGE_SIZE), F32),
        compiler_params=_params("arbitrary", "arbitrary"),
    )(page_table.reshape(-1), qi, wi_bc, *([cache_kidx] * pg))


def _dsa_s_attend_kernel(pt_ref, sc_ref, q_ref, qi_ref, kis_ref, wi_ref, ks_ref, vs_ref, bkt_ref, rbt_ref, *refs,
                         pg, n_pages, n_top, rp):
    k_refs, v_refs = refs[:pg], refs[pg:2 * pg]
    o_ref, mb_scr, m_scr, l_scr, acc_scr = refs[2 * pg:]
    s = pl.program_id(1)
    nq = N_HEADS // N_KV_HEADS
    kvw = N_KV_HEADS * HEAD_DIM
    idx_scale = (N_IDX_HEADS * IDX_DIM) ** -0.5

    @pl.when(s == 0)
    def _():
        qi = qi_ref[0].astype(BF16).astype(F32)
        kis = kis_ref[0].astype(BF16).astype(F32)
        d_self = jnp.sum(qi * kis, axis=1, keepdims=True)
        s_self = jnp.sum(jnp.maximum(d_self, 0.0) * wi_ref[0][:, 0:1], axis=0, keepdims=True) * idx_scale
        r_io = lax.broadcasted_iota(I32, (rp, LANES), 0)
        c_io = lax.broadcasted_iota(I32, (rp, LANES), 1)
        tail = jnp.where((r_io == n_pages) & (c_io == 0), s_self, NEG_INF)
        if rp > n_pages:
            score = jnp.concatenate([sc_ref[0], jnp.zeros((rp - n_pages, LANES), F32)], axis=0)
        else:
            score = sc_ref[0]
        score = jnp.where(r_io < n_pages, score, tail)
        key = _sort_key(score)

        def count_ge(cand):
            c = jnp.sum((key >= cand).astype(I32), axis=1, keepdims=True)
            return jnp.sum(c, axis=0, keepdims=True)

        thr = _radix_threshold(count_ge, n_top, (1, 1))
        gt = key > thr
        eq = key == thr
        n_gt = jnp.sum(jnp.sum(gt.astype(I32), axis=1, keepdims=True), axis=0, keepdims=True)
        need = (n_top - n_gt).astype(F32)
        row = lax.broadcasted_iota(I32, (LANES, LANES), 0)
        col = lax.broadcasted_iota(I32, (LANES, LANES), 1)
        pre = _dot(eq.astype(BF16), (row <= col).astype(BF16))
        rr = lax.broadcasted_iota(I32, (rp, rp), 0)
        cc = lax.broadcasted_iota(I32, (rp, rp), 1)
        row_tot = jnp.broadcast_to(pre[:, LANES - 1:LANES], (rp, LANES)).astype(BF16)
        row_off = _dot((cc < rr).astype(BF16), row_tot)
        valid = (r_io < n_pages) | ((r_io == n_pages) & (c_io == 0))
        sel = (gt | (eq & (row_off + pre <= need))) & valid
        mb_scr[...] = jnp.where(sel, 0.0, NEG_INF)
        m_scr[...] = jnp.full(m_scr.shape, -1e30, F32)
        l_scr[...] = jnp.zeros_like(l_scr)
        acc_scr[...] = jnp.zeros_like(acc_scr)

    q = q_ref[0] * (HEAD_DIM ** -0.5)
    h_io = lax.broadcasted_iota(I32, (N_HEADS, kvw), 0)
    c_io2 = lax.broadcasted_iota(I32, (N_HEADS, kvw), 1)
    band = (c_io2 // HEAD_DIM) == (h_io // nq)
    q_bd = jnp.where(band, jnp.concatenate([q] * N_KV_HEADS, axis=1), 0.0)
    q_bd16 = q_bd.astype(BF16)
    rbt = rbt_ref[...]

    def bias_of(bkt_row):
        out = jnp.zeros((N_HEADS, bkt_row.shape[1]), F32)
        for b in range(N_BUCKETS):
            out = jnp.where(bkt_row == b, rbt[:, b:b + 1], out)
        return out

    logits = []
    for r in range(pg):
        page = s * pg + r
        lg = _dot_nt(q_bd16, k_refs[r][0].astype(BF16))
        lg = lg + bias_of(bkt_ref[pl.ds(page, 1), :]) + mb_scr[pl.ds(page, 1), :]
        logits.append(lg)
    m_old = m_scr[...]
    m_new = m_old
    for lg in logits:
        m_new = jnp.maximum(m_new, jnp.max(lg, axis=1, keepdims=True))
    alpha = jnp.exp(m_old - m_new)
    l_new = alpha * l_scr[...]
    acc = alpha * acc_scr[...]
    for r, lg in enumerate(logits):
        p = jnp.exp(lg - m_new)
        l_new = l_new + jnp.sum(p, axis=1, keepdims=True)
        acc = acc + _dot(p.astype(BF16), v_refs[r][0].astype(BF16))
    m_scr[...] = m_new
    l_scr[...] = l_new
    acc_scr[...] = acc

    @pl.when(s == pl.num_programs(1) - 1)
    def _():
        ks = ks_ref[0].astype(BF16).astype(F32)
        lg = jnp.sum(q_bd16.astype(F32) * ks, axis=1, keepdims=True)
        lg = lg + bias_of(bkt_ref[n_pages:n_pages + 1, :])[:, 0:1] + mb_scr[n_pages:n_pages + 1, 0:1]
        m_fin = jnp.maximum(m_new, lg)
        a = jnp.exp(m_new - m_fin)
        p = jnp.exp(lg - m_fin)
        l_fin = a * l_new + p
        out = (a * acc + p.astype(BF16).astype(F32) * vs_ref[0].astype(BF16).astype(F32)) / l_fin
        out = jnp.where(band, out, 0.0)
        res = out[:, 0:HEAD_DIM]
        for n in range(1, N_KV_HEADS):
            res = res + out[:, n * HEAD_DIM:(n + 1) * HEAD_DIM]
        o_ref[0] = res.astype(o_ref.dtype)


def _dsa_s_attend(page_table, scores, q, qi, ki_s, wi_bc, k_s, v_s, cache_k, cache_v, rel_bias, *, pg):
    db, n_pages = page_table.shape
    past = n_pages * PAGE_SIZE
    n_top = min(TOPK_MAX, (past + 1) // 4)
    g = n_pages // pg
    rp = -(-(n_pages + 1) // LANES) * LANES
    kvw = N_KV_HEADS * HEAD_DIM
    pos = np.arange(rp * PAGE_SIZE).reshape(rp, PAGE_SIZE)
    bkt = jnp.asarray(_t5_bucket_np(past - pos))
    per_seq = lambda *shape: pl.BlockSpec((1,) + shape, lambda b, s, pt: (b,) + (0,) * len(shape))
    const2 = lambda shape: pl.BlockSpec(shape, lambda b, s, pt: (0, 0))
    page_spec = lambda r: pl.BlockSpec(
        (1, PAGE_SIZE, kvw), lambda b, s, pt: (pt[b * n_pages + s * pg + r], 0, 0))
    grid_spec = pltpu.PrefetchScalarGridSpec(
        num_scalar_prefetch=1,
        grid=(db, g),
        in_specs=[per_seq(n_pages, PAGE_SIZE), per_seq(N_HEADS, HEAD_DIM), per_seq(N_IDX_HEADS, IDX_DIM),
                  per_seq(1, IDX_DIM), per_seq(N_IDX_HEADS, LANES), per_seq(1, kvw), per_seq(1, kvw),
                  const2((rp, PAGE_SIZE)), const2((N_HEADS, N_BUCKETS))]
                 + [page_spec(r) for r in range(pg)] * 2,
        out_specs=per_seq(N_HEADS, HEAD_DIM),
        scratch_shapes=[pltpu.VMEM((rp, LANES), F32), pltpu.VMEM((N_HEADS, 1), F32),
                        pltpu.VMEM((N_HEADS, 1), F32), pltpu.VMEM((N_HEADS, kvw), F32)],
    )
    return pl.pallas_call(
        functools.partial(_dsa_s_attend_kernel, pg=pg, n_pages=n_pages, n_top=n_top, rp=rp),
        grid_spec=grid_spec,
        out_shape=jax.ShapeDtypeStruct((db, N_HEADS, HEAD_DIM), BF16),
        compiler_params=_params("arbitrary", "arbitrary"),
    )(page_table.reshape(-1), scores, q, qi, ki_s, wi_bc, k_s, v_s, bkt, rel_bias.T,
      *([cache_k] * pg), *([cache_v] * pg))


def _ssd_kernel(xbc_ref, z_ref, dt_ref, cw_ref, cb_ref, alog_ref, dtb_ref, dsk_ref, ng_ref, e_ref, *refs,
                rows, has_init, d_inner):
    if has_init:
        conv0_ref, ssm0_ref = refs[:2]
        refs = refs[2:]
    y_ref, ssm_ref, conv_ref, ext_scr, ht_scr, y_scr = refs
    c = pl.program_id(1)
    cl = SSD_CHUNK
    gw = d_inner // SSD_GROUPS
    hpg = gw // SSD_HEAD_DIM
    n_heads = d_inner // SSD_HEAD_DIM
    conv_dim = d_inner + 2 * SSD_GROUPS * D_STATE
    conv_w = cw_ref.shape[0]
    top = SUBLANES
    row_io = lax.broadcasted_iota(I32, (cl, 1), 0)

    def padded(ref):
        a = ref[0]
        if rows == cl:
            return a
        return jnp.where(row_io < rows, jnp.broadcast_to(a, (cl, a.shape[1])), 0.0)

    @pl.when(c == 0)
    def _():
        ext_scr[0:top, :] = jnp.zeros((top, conv_dim), F32)
        if has_init:
            ext_scr[top - conv_w + 1:top, :] = conv0_ref[0]
            for g in range(SSD_GROUPS):
                ht_scr[g] = ssm0_ref[0, g * gw:(g + 1) * gw, :].T
        else:
            ht_scr[...] = jnp.zeros_like(ht_scr)

    ext_scr[top:top + cl, :] = padded(xbc_ref)
    conv_ref[0] = ext_scr[top + rows - conv_w + 1:top + rows, :]

    cblk = 512
    for cb in range(conv_dim // cblk):
        sl = slice(cb * cblk, (cb + 1) * cblk)
        acc = jnp.broadcast_to(cb_ref[:, sl], (cl, cblk))
        for w in range(conv_w):
            acc = acc + ext_scr[top - conv_w + 1 + w:top - conv_w + 1 + w + cl, sl] * cw_ref[w:w + 1, sl]
        y_scr[:, sl] = _silu(acc)
    ext_scr[top - conv_w + 1:top, :] = ext_scr[top + cl - conv_w + 1:top + cl, :]

    dt = jax.nn.softplus(padded(dt_ref) + dtb_ref[...])
    if rows < cl:
        dt = jnp.where(row_io < rows, dt, 0.0)
    a_neg = -jnp.exp(alog_ref[...])
    r_io = lax.broadcasted_iota(I32, (cl, cl), 0)
    c_io = lax.broadcasted_iota(I32, (cl, cl), 1)
    tril = r_io >= c_io
    acs = _dot_exact_lhs(tril.astype(BF16), dt * a_neg)
    acs_t = acs.T
    acs_last = acs[cl - 1:cl, :]
    stacked = jnp.concatenate([dt, jnp.exp(acs), jnp.exp(acs_last - acs)], axis=0)
    expanded = _dot_exact_rhs(stacked, e_ref[...])
    dt_x, ea_x, te_x = expanded[0:cl], expanded[cl:2 * cl], expanded[2 * cl:3 * cl]

    z = padded(z_ref)
    for g in range(SSD_GROUPS):
        gs = slice(g * gw, (g + 1) * gw)
        x_g = y_scr[:, gs]
        b_g = y_scr[:, d_inner + g * D_STATE:d_inner + (g + 1) * D_STATE]
        c_g = y_scr[:, d_inner + (SSD_GROUPS + g) * D_STATE:d_inner + (SSD_GROUPS + g + 1) * D_STATE]
        c16 = c_g.astype(BF16)
        cbm = _dot_nt(c16, b_g.astype(BF16))
        xdt = x_g * dt_x[:, gs]
        xdt16 = xdt.astype(BF16)
        ht = ht_scr[g]
        y_g = _dot(c16, ht.astype(BF16)) * ea_x[:, gs] + dsk_ref[:, gs] * x_g
        ht_scr[g] = ht * ea_x[cl - 1:cl, gs] + _dot(b_g.T.astype(BF16), (xdt * te_x[:, gs]).astype(BF16))
        diag = []
        for r in range(hpg):
            h = g * hpg + r
            seg = acs[:, h:h + 1] - acs_t[h:h + 1, :]
            m = (cbm * jnp.exp(jnp.where(tril, seg, NEG_INF))).astype(BF16)
            diag.append(_dot(m, xdt16[:, r * SSD_HEAD_DIM:(r + 1) * SSD_HEAD_DIM]))
        y_g = (y_g + jnp.concatenate(diag, axis=1)) * _silu(z[:, gs])
        y_g = y_g * lax.rsqrt(jnp.mean(y_g * y_g, axis=1, keepdims=True) + EPS) * ng_ref[:, gs]
        y_ref[0, :, gs] = y_g[0:rows].astype(y_ref.dtype)

    @pl.when(c == pl.num_programs(1) - 1)
    def _():
        for g in range(SSD_GROUPS):
            ssm_ref[0, g * gw:(g + 1) * gw, :] = ht_scr[g].T


def _ssd(xbc, z, dt, conv_w, conv_b, a_log, dt_bias, d_skip, norm_g, *, n_seq, n_chunks, rows,
         conv0=None, ssm0=None):
    conv_dim = xbc.shape[-1]
    d_inner = z.shape[-1]
    n_heads = d_inner // SSD_HEAD_DIM
    cw = conv_w.shape[0]
    has_init = conv0 is not None
    pad = lambda a: jnp.pad(a, (0, LANES - a.shape[0])).reshape(1, LANES)
    expand = np.zeros((LANES, d_inner), np.float32)
    expand[np.arange(d_inner) // SSD_HEAD_DIM, np.arange(d_inner)] = 1.0
    step = lambda b, c: (b * n_chunks + c, 0, 0)
    seq = lambda b, c: (b, 0, 0)
    const = lambda b, c: (0, 0)
    in_specs = [pl.BlockSpec((1, rows, conv_dim), step), pl.BlockSpec((1, rows, d_inner), step),
                pl.BlockSpec((1, rows, LANES), step),
                pl.BlockSpec((cw, conv_dim), const), pl.BlockSpec((1, conv_dim), const),
                pl.BlockSpec((1, LANES), const), pl.BlockSpec((1, LANES), const),
                pl.BlockSpec((1, d_inner), const), pl.BlockSpec((1, d_inner), const),
                pl.BlockSpec((LANES, d_inner), const)]
    args = [xbc, z, dt, conv_w, conv_b.reshape(1, conv_dim), pad(a_log), pad(dt_bias),
            jnp.repeat(d_skip, SSD_HEAD_DIM).reshape(1, d_inner), norm_g.reshape(1, d_inner),
            jnp.asarray(expand, BF16)]
    if has_init:
        in_specs += [pl.BlockSpec((1, cw - 1, conv_dim), seq), pl.BlockSpec((1, d_inner, D_STATE), seq)]
        args += [conv0, ssm0]
    return pl.pallas_call(
        functools.partial(_ssd_kernel, rows=rows, has_init=has_init, d_inner=d_inner),
        grid=(n_seq, n_chunks),
        in_specs=in_specs,
        out_specs=[pl.BlockSpec((1, rows, d_inner), step), pl.BlockSpec((1, d_inner, D_STATE), seq),
                   pl.BlockSpec((1, cw - 1, conv_dim), seq)],
        out_shape=[jax.ShapeDtypeStruct((n_seq * n_chunks, rows, d_inner), BF16),
                   jax.ShapeDtypeStruct((n_seq, d_inner, D_STATE), F32),
                   jax.ShapeDtypeStruct((n_seq, cw - 1, conv_dim), F32)],
        scratch_shapes=[pltpu.VMEM((SUBLANES + SSD_CHUNK, conv_dim), F32),
                        pltpu.VMEM((SSD_GROUPS, D_STATE, d_inner // SSD_GROUPS), F32),
                        pltpu.VMEM((SSD_CHUNK, conv_dim), F32)],
        compiler_params=_params("arbitrary", "arbitrary"),
    )(*args)


def _merge_kernel(x_ref, gt_ref, att_ref, ssd_ref, ga_ref, gs_ref, wa_ref, ws_ref, wo_ref, o_ref):
    merged = (jax.nn.sigmoid(ga_ref[...]) * _dot(att_ref[...], wa_ref[...])
              + jax.nn.sigmoid(gs_ref[...]) * _dot(ssd_ref[...], ws_ref[...]))
    o_ref[...] = x_ref[...] + gt_ref[0] * _dot(merged.astype(BF16), wo_ref[...])


def _merge(x, gate, att, ssd_y, gate_a, gate_s, w_a, w_s, w_o, *, tm):
    m, d = x.shape
    row = lambda i: (i, 0)
    const = lambda i: (0, 0)
    return pl.pallas_call(
        _merge_kernel,
        grid=(m // tm,),
        in_specs=[pl.BlockSpec((tm, d), row), gate.spec(tm),
                  pl.BlockSpec((tm, att.shape[1]), row), pl.BlockSpec((tm, ssd_y.shape[1]), row),
                  pl.BlockSpec((tm, d), row), pl.BlockSpec((tm, d), row),
                  pl.BlockSpec(w_a.shape, const), pl.BlockSpec(w_s.shape, const), pl.BlockSpec(w_o.shape, const)],
        out_specs=pl.BlockSpec((tm, d), row),
        out_shape=jax.ShapeDtypeStruct((m, d), F32),
        compiler_params=_params("parallel"),
    )(x, gate.arr, att, ssd_y, gate_a, gate_s, w_a, w_s, w_o)


def _pad_cols(w, width):
    return jnp.pad(w, ((0, 0), (0, width - w.shape[1])))


def _trunk(x, mods, rows_per_seq, p, tm, attend, ssd_fn, final_g):
    sh1, sc1, g1, sh2, sc2, g2, sh3, sc3, g3 = [_Mod(a, rows_per_seq) for a in mods]
    x = _ffn(x, sh1, sc1, g1, p["norm_ffn1"], p["w_ffn1_in"], p["w_ffn1_out"], final_g, tm=tm, final_norm=False)
    q, k, v, qi, ki, wi = _proj(x, sh2, sc2, p["norm_mix"], p["w_att"], p["seg_att"], tm=tm)
    z, xbc, dt = _proj(x, sh2, sc2, p["norm_mix"], p["w_ssd"], p["seg_ssd"], tm=tm)
    gate_a, gate_s = _proj(x, sh2, sc2, p["norm_mix"], p["w_gate"], p["seg_gate"], tm=tm)
    att = attend(q, k, v, qi, ki, wi)
    ssd_y, ssm_new, conv_new = ssd_fn(z, xbc, dt)
    x = _merge(x, g2, att, ssd_y, gate_a, gate_s, p["w_attn_out"], p["w_ssd_out"], p["w_out"], tm=tm)
    y = _ffn(x, sh3, sc3, g3, p["norm_ffn2"], p["w_ffn2_in"], p["w_ffn2_out"], final_g, tm=tm, final_norm=True)
    return y, (k, v, ki, ssm_new, conv_new)


def kernel(x_prompt, x_sample, c_prompt, c_sample, cache_k, cache_v, cache_kidx, state_ssm, state_conv, page_table,
           w_ada, b_ada, norm_ffn1, w_ffn1_in, w_ffn1_out, norm_mix, w_in, rel_bias, conv_w, conv_b, a_log, dt_bias,
           d_skip, norm_ssd, w_attn_out, w_ssd_out, w_out, norm_ffn2, w_ffn2_in, w_ffn2_out, norm_final):
    depth = w_ada.shape[0]
    assert depth == 1
    batch, seq, d = x_prompt.shape
    db, dec_seq, _ = x_sample.shape
    assert dec_seq == 1
    n_pool = cache_k.shape[1]
    n_pages = page_table.shape[1]
    d_inner = norm_ssd.shape[1]
    conv_dim = conv_w.shape[2]
    n_ssd_heads = d_inner // SSD_HEAD_DIM
    att_q = N_HEADS * HEAD_DIM
    att_kv = N_KV_HEADS * HEAD_DIM
    idx_q = N_IDX_HEADS * IDX_DIM
    l = 0

    widths = (att_q, att_kv, att_kv, idx_q, IDX_DIM, N_IDX_HEADS, d_inner, conv_dim, n_ssd_heads, d, d)
    bounds = np.concatenate([[0], np.cumsum(widths)])
    assert bounds[-1] == w_in.shape[2]
    cols = [w_in[l][:, bounds[i]:bounds[i + 1]].astype(BF16) for i in range(len(widths))]
    w_q, w_k, w_v, w_qi, w_ki, w_wi, w_z, w_xbc, w_dt, w_ga, w_gs = cols
    row1 = lambda a: a.reshape(1, -1)
    p = {
        "norm_ffn1": row1(norm_ffn1[l]), "w_ffn1_in": w_ffn1_in[l].astype(BF16), "w_ffn1_out": w_ffn1_out[l].astype(BF16),
        "norm_mix": row1(norm_mix[l]),
        "w_att": jnp.concatenate([w_q, w_k, w_v, w_qi, _pad_cols(w_ki, LANES), _pad_cols(w_wi, LANES)], axis=1),
        "seg_att": [(att_q, BF16), (att_kv, F32), (att_kv, F32), (idx_q, BF16), (IDX_DIM, F32), (LANES, F32)],
        "w_ssd": jnp.concatenate([w_z, w_xbc, _pad_cols(w_dt, LANES)], axis=1),
        "seg_ssd": [(d_inner, F32), (conv_dim, F32), (LANES, F32)],
        "w_gate": jnp.concatenate([w_ga, w_gs], axis=1),
        "seg_gate": [(d, F32), (d, F32)],
        "w_attn_out": w_attn_out[l].astype(BF16), "w_ssd_out": w_ssd_out[l].astype(BF16), "w_out": w_out[l].astype(BF16),
        "norm_ffn2": row1(norm_ffn2[l]), "w_ffn2_in": w_ffn2_in[l].astype(BF16), "w_ffn2_out": w_ffn2_out[l].astype(BF16),
    }
    final_g = row1(norm_final)
    ssd_args = (conv_w[l], conv_b[l], a_log[l], dt_bias[l], d_skip[l], norm_ssd[l])

    ada = _ada(jnp.concatenate([c_prompt, c_sample], axis=0), w_ada[l], b_ada[l])
    ada_p = [a.reshape(batch, 1, d) for a in jnp.split(ada[:batch], 9, axis=1)]
    ada_s = [a.reshape(1, db, d) for a in jnp.split(ada[batch:], 9, axis=1)]

    tq = 128
    n_chunks = seq // SSD_CHUNK
    bias_tab = _bias_table(rel_bias, tq)

    def attend_p(q, k, v, qi, ki, wi):
        return _dsa_prompt(q, qi, wi, k, v, ki, bias_tab, batch=batch, seq=seq, tq=tq)

    def ssd_p(z, xbc, dt):
        r3 = lambda a: a.reshape(batch * n_chunks, SSD_CHUNK, a.shape[-1])
        y, ssm, conv = _ssd(r3(xbc), r3(z), r3(dt), *ssd_args, n_seq=batch, n_chunks=n_chunks, rows=SSD_CHUNK)
        return y.reshape(batch * seq, d_inner), ssm, conv

    yp, (k_p, v_p, ki_p, ssm_p, conv_p) = _trunk(x_prompt.reshape(batch * seq, d), ada_p, seq, p, 512,
                                                 attend_p, ssd_p, final_g)

    pg = min(16, n_pages)
    ck = cache_k[l].reshape(n_pool, PAGE_SIZE, att_kv)
    cv = cache_v[l].reshape(n_pool, PAGE_SIZE, att_kv)

    def attend_s(q, k, v, qi, ki, wi):
        qi3 = qi.reshape(db, N_IDX_HEADS, IDX_DIM)
        wi_bc = jnp.broadcast_to(wi[:, :N_IDX_HEADS, None], (db, N_IDX_HEADS, LANES))
        scores = _dsa_s_scores(page_table, qi3, wi_bc, cache_kidx[l], pg=pg)
        att = _dsa_s_attend(page_table, scores, q.reshape(db, N_HEADS, HEAD_DIM), qi3, ki.reshape(db, 1, IDX_DIM),
                            wi_bc, k.reshape(db, 1, att_kv), v.reshape(db, 1, att_kv), ck, cv, rel_bias, pg=pg)
        return att.reshape(db, att_q)

    def ssd_s(z, xbc, dt):
        r3 = lambda a: a.reshape(db, 1, a.shape[-1])
        y, ssm, conv = _ssd(r3(xbc), r3(z), r3(dt), *ssd_args, n_seq=db, n_chunks=1, rows=1,
                            conv0=state_conv[l], ssm0=state_ssm[l].reshape(db, d_inner, D_STATE))
        return y.reshape(db, d_inner), ssm, conv

    ys, (k_s, v_s, ki_s, ssm_s, conv_s) = _trunk(x_sample.reshape(db, d), ada_s, db, p, db,
                                                 attend_s, ssd_s, final_g)

    st = lambda a, *shape: a.reshape((1,) + shape)
    return (yp.reshape(batch, seq, d), ys.reshape(db, 1, d),
            st(k_p, batch, seq, N_KV_HEADS, HEAD_DIM), st(v_p, batch, seq, N_KV_HEADS, HEAD_DIM),
            st(ki_p, batch, seq, IDX_DIM),
            st(ssm_p, batch, n_ssd_heads, SSD_HEAD_DIM, D_STATE), st(conv_p, batch, conv_w.shape[1] - 1, conv_dim),
            st(k_s, db, 1, N_KV_HEADS, HEAD_DIM), st(v_s, db, 1, N_KV_HEADS, HEAD_DIM), st(ki_s, db, 1, IDX_DIM),
            st(ssm_s, db, n_ssd_heads, SSD_HEAD_DIM, D_STATE), st(conv_s, db, conv_w.shape[1] - 1, conv_dim))
```

```python
import functools
import math

import numpy as np
import jax
import jax.numpy as jnp
from jax import lax
from jax.experimental import pallas as pl
from jax.experimental.pallas import tpu as pltpu

N_HEADS = 16
HEAD_DIM = 64
N_KV_HEADS = 4
N_IDX_HEADS = 8
IDX_DIM = 64
TOPK_MAX = 256
N_BUCKETS = 32
MAX_DISTANCE = 128
SSD_HEAD_DIM = 64
SSD_GROUPS = 4
D_STATE = 128
SSD_CHUNK = 128
EPS = 1e-6
PAGE_SIZE = 128

LANES = 128
SUBLANES = 8
VMEM_LIMIT = 56 * 1024 * 1024

F32 = jnp.float32
BF16 = jnp.bfloat16
I32 = jnp.int32
NEG_INF = float("-inf")
INT_MIN = -(2 ** 31)
LOG2E = math.log2(math.e)

_NT = (((1,), (1,)), ((), ()))


def _dot(a, b):
    return jnp.dot(a, b, preferred_element_type=F32)


def _dot_nt(a, b):
    return lax.dot_general(a, b, _NT, preferred_element_type=F32)


def _split3(a):
    hi = a.astype(BF16)
    r = a - hi.astype(F32)
    mid = r.astype(BF16)
    lo = (r - mid.astype(F32)).astype(BF16)
    return hi, mid, lo


def _dot_exact_rhs(a, b_bf16):
    hi, mid, lo = _split3(a)
    return _dot(hi, b_bf16) + _dot(mid, b_bf16) + _dot(lo, b_bf16)


def _dot_exact_lhs(a_bf16, b):
    hi, mid, lo = _split3(b)
    return _dot(a_bf16, hi) + _dot(a_bf16, mid) + _dot(a_bf16, lo)


def _rmsnorm(x, g):
    return (x * lax.rsqrt(jnp.mean(x * x, axis=-1, keepdims=True) + EPS)) * g


def _silu(x):
    return x * jax.nn.sigmoid(x)


def _params(*sem):
    return pltpu.CompilerParams(dimension_semantics=sem, vmem_limit_bytes=VMEM_LIMIT)


def _t5_bucket_np(dist):
    n = np.maximum(dist, 0)
    max_exact = N_BUCKETS // 2
    nf = np.maximum(n, 1).astype(np.float32)
    val = (np.log(nf / np.float32(max_exact)) / np.float32(math.log(MAX_DISTANCE / max_exact))
           * np.float32(N_BUCKETS - max_exact)).astype(np.float32)
    frac = np.abs(val - np.round(val))
    knife = (frac < 1e-3) & (n > max_exact) & (val < N_BUCKETS - max_exact - 0.5)
    assert not knife.any()
    large = np.minimum(max_exact + val.astype(np.int32), N_BUCKETS - 1)
    return np.where(n < max_exact, n, large).astype(np.int32)


def _ada_kernel(c_ref, w_ref, b_ref, o_ref):
    h = _silu(c_ref[...]).astype(BF16)
    o_ref[...] = _dot(h, w_ref[...].astype(BF16)) + b_ref[...]


def _ada(c, w, b):
    rows, d = c.shape
    n = w.shape[1]
    tn = 1024
    return pl.pallas_call(
        _ada_kernel,
        grid=(n // tn,),
        in_specs=[pl.BlockSpec((rows, d), lambda j: (0, 0)),
                  pl.BlockSpec((d, tn), lambda j: (0, j)),
                  pl.BlockSpec((1, tn), lambda j: (0, j))],
        out_specs=pl.BlockSpec((rows, tn), lambda j: (0, j)),
        out_shape=jax.ShapeDtypeStruct((rows, n), F32),
        compiler_params=_params("arbitrary"),
    )(c, w, b.reshape(1, n))


class _Mod:
    def __init__(self, arr, rows_per_seq):
        self.arr = arr
        self.rows_per_seq = rows_per_seq

    def spec(self, tm):
        r = self.arr.shape[1]
        d = self.arr.shape[2]
        if r == 1:
            per = self.rows_per_seq // tm
            return pl.BlockSpec((1, 1, d), lambda i, *_: (i // per, 0, 0))
        assert r == tm
        return pl.BlockSpec((1, r, d), lambda i, *_: (i, 0, 0))


def _ffn_kernel(x_ref, sh_ref, sc_ref, gt_ref, g_ref, wg_ref, wu_ref, wo_ref, fg_ref, o_ref, h_scr, acc_scr,
                *, final_norm):
    j = pl.program_id(1)

    @pl.when(j == 0)
    def _():
        h = _rmsnorm(x_ref[...], g_ref[...]) * (1.0 + sc_ref[0]) + sh_ref[0]
        h_scr[...] = h.astype(BF16)
        acc_scr[...] = jnp.zeros_like(acc_scr)

    h = h_scr[...]
    gate = _dot(h, wg_ref[...])
    up = _dot(h, wu_ref[...])
    act = (_silu(gate) * up).astype(BF16)
    acc_scr[...] += _dot(act, wo_ref[...])

    @pl.when(j == pl.num_programs(1) - 1)
    def _():
        out = x_ref[...] + 0.5 * gt_ref[0] * acc_scr[...]
        if final_norm:
            out = _rmsnorm(out, fg_ref[...])
        o_ref[...] = out


def _ffn(x, shift, scale, gate, norm_g, w_in, w_out, final_g, *, tm, final_norm):
    m, d = x.shape
    f = w_out.shape[0]
    nj = 2
    tf = f // nj
    assert tf % LANES == 0 and m % tm == 0
    row = lambda i, j: (i, 0)
    const = lambda i, j: (0, 0)
    return pl.pallas_call(
        functools.partial(_ffn_kernel, final_norm=final_norm),
        grid=(m // tm, nj),
        in_specs=[pl.BlockSpec((tm, d), row),
                  shift.spec(tm), scale.spec(tm), gate.spec(tm),
                  pl.BlockSpec((1, d), const),
                  pl.BlockSpec((d, tf), lambda i, j: (0, j)),
                  pl.BlockSpec((d, tf), lambda i, j: (0, j + nj)),
                  pl.BlockSpec((tf, d), lambda i, j: (j, 0)),
                  pl.BlockSpec((1, d), const)],
        out_specs=pl.BlockSpec((tm, d), row),
        out_shape=jax.ShapeDtypeStruct((m, d), F32),
        scratch_shapes=[pltpu.VMEM((tm, d), BF16), pltpu.VMEM((tm, d), F32)],
        compiler_params=_params("parallel", "arbitrary"),
    )(x, shift.arr, scale.arr, gate.arr, norm_g, w_in, w_in, w_out, final_g)


def _proj_kernel(x_ref, sh_ref, sc_ref, g_ref, w_ref, *o_refs, offsets, scales):
    h = (_rmsnorm(x_ref[...], g_ref[...]) * (1.0 + sc_ref[0]) + sh_ref[0]).astype(BF16)
    for o_ref, off, scale in zip(o_refs, offsets, scales):
        width = o_ref.shape[1]
        out = _dot(h, w_ref[:, off:off + width])
        if scale != 1.0:
            out = out * scale
        o_ref[...] = out.astype(o_ref.dtype)


def _proj(x, shift, scale, norm_g, w, segments, *, tm):
    m, d = x.shape
    offsets, off = [], 0
    for width, _, _ in segments:
        offsets.append(off)
        off += -(-width // LANES) * LANES
    assert off == w.shape[1]
    row = lambda i: (i, 0)
    const = lambda i: (0, 0)
    return pl.pallas_call(
        functools.partial(_proj_kernel, offsets=tuple(offsets), scales=tuple(s for _, _, s in segments)),
        grid=(m // tm,),
        in_specs=[pl.BlockSpec((tm, d), row), shift.spec(tm), scale.spec(tm),
                  pl.BlockSpec((1, d), const), pl.BlockSpec(w.shape, const)],
        out_specs=[pl.BlockSpec((tm, width), row) for width, _, _ in segments],
        out_shape=[jax.ShapeDtypeStruct((m, width), dt) for width, dt, _ in segments],
        compiler_params=_params("parallel"),
    )(x, shift.arr, scale.arr, norm_g, w)


def _sort_key(score):
    score = jnp.where(score == 0.0, 0.0, score)
    bits = pltpu.bitcast(score, I32)
    return jnp.where(bits >= 0, bits, bits ^ jnp.int32(0x7FFFFFFF))


def _radix_threshold(count_ge, n_top, shape):
    def bit_body(t, carry):
        thr_u, cnt_thr = carry
        cand_u = thr_u | jnp.left_shift(jnp.int32(1), 31 - t)
        cnt = count_ge(cand_u ^ jnp.int32(INT_MIN))
        take = cnt >= n_top
        return jnp.where(take, cand_u, thr_u), jnp.where(take, cnt, cnt_thr)

    thr_u, cnt_thr = lax.fori_loop(0, 32, bit_body, (jnp.zeros(shape, I32), jnp.zeros(shape, I32)))
    return thr_u ^ jnp.int32(INT_MIN), cnt_thr


def _bias_table_kernel(rb_ref, bkt_ref, o_ref):
    far = N_BUCKETS - 1
    for slot in range(bkt_ref.shape[0]):
        bkt = bkt_ref[slot]
        for h in range(N_HEADS):
            acc = jnp.zeros(bkt.shape, F32)
            for b in range(N_BUCKETS - 1):
                acc = jnp.where(bkt == b, (rb_ref[b, h] - rb_ref[far, h]) * LOG2E, acc)
            o_ref[slot, h] = acc


def _bias_table(rel_bias, tq, seq):
    s = np.arange(tq)[:, None]
    t = np.arange(tq)[None, :]
    bkt = np.stack([_t5_bucket_np(t + tq - s), _t5_bucket_np(t - s)])
    assert (_t5_bucket_np(np.arange(tq + 1, seq + 1)) == N_BUCKETS - 1).all()
    return pl.pallas_call(
        _bias_table_kernel,
        in_specs=[pl.BlockSpec(memory_space=pltpu.SMEM), pl.BlockSpec(memory_space=pltpu.VMEM)],
        out_specs=pl.BlockSpec(memory_space=pltpu.VMEM),
        out_shape=jax.ShapeDtypeStruct((2, N_HEADS, tq, tq), F32),
    )(rel_bias, jnp.asarray(bkt))


_V_ROWS = HEAD_DIM + 16


def _dsa_prompt_kernel(q_ref, qi_ref, wi_ref, k_ref, v_ref, ki_ref, bias_ref, o_ref,
                       kb_scr, vt_scr, kib_scr, key_scr, mb_scr, m_scr, acc_scr, *, n_top, tq):
    i = pl.program_id(1)
    nq = N_HEADS // N_KV_HEADS
    nblk = key_scr.shape[0]
    s_io = lax.broadcasted_iota(I32, (tq, tq), 0)
    t_io = lax.broadcasted_iota(I32, (tq, tq), 1)
    causal = s_io <= t_io

    @pl.when(i == 0)
    def _():
        kb_scr[...] = k_ref[...].astype(BF16)
        kib_scr[...] = ki_ref[...].astype(BF16)
        for c in range(nblk):
            for pair in range(N_KV_HEADS // 2):
                vt = v_ref[c * tq:(c + 1) * tq, pair * 2 * HEAD_DIM:(pair + 1) * 2 * HEAD_DIM].T.astype(BF16)
                vt_scr[c, 2 * pair, 0:HEAD_DIM, :] = vt[0:HEAD_DIM]
                vt_scr[c, 2 * pair + 1, 0:HEAD_DIM, :] = vt[HEAD_DIM:2 * HEAD_DIM]
            for n in range(N_KV_HEADS):
                vt_scr[c, n, HEAD_DIM:_V_ROWS, :] = jnp.ones((_V_ROWS - HEAD_DIM, tq), BF16)

    def chunk(j):
        return pl.ds(pl.multiple_of(j * tq, tq), tq)

    def fold(a):
        return jnp.sum(a.reshape(tq // SUBLANES, SUBLANES, tq), axis=0)

    qi = qi_ref[...]
    qi_st = jnp.concatenate([qi[:, h * IDX_DIM:(h + 1) * IDX_DIM] for h in range(N_IDX_HEADS)], axis=0)
    wi_t = wi_ref[...].T
    wi_row = jnp.concatenate([wi_t[h:h + 1, :] for h in range(N_IDX_HEADS)], axis=1)
    idx_scale = (N_IDX_HEADS * IDX_DIM) ** -0.5

    def visible(j):
        return (j < i) | ((j == i) & causal)

    n_pairs = (i + 2) // 2

    def score_body(jp, carry):
        for j in (2 * jp, 2 * jp + 1):
            d = jnp.maximum(_dot_nt(kib_scr[chunk(j), :], qi_st), 0.0) * wi_row
            s = d[:, 0:tq]
            for h in range(1, N_IDX_HEADS):
                s = s + d[:, h * tq:(h + 1) * tq]
            key_scr[j] = _sort_key(jnp.where(visible(j), s * idx_scale, NEG_INF))
        return carry

    lax.fori_loop(0, n_pairs, score_body, 0)

    def count(pred):
        def body(jp, c):
            return c + fold(pred(key_scr[2 * jp]).astype(I32)) + fold(pred(key_scr[2 * jp + 1]).astype(I32))
        c = lax.fori_loop(0, n_pairs, body, jnp.zeros((SUBLANES, tq), I32))
        return jnp.sum(c, axis=0, keepdims=True)

    thr, cnt_thr = _radix_threshold(lambda cand: count(lambda key: key >= cand), n_top, (1, tq))

    def sel_plain():
        def body(j, carry):
            mb_scr[j] = jnp.where((key_scr[j] >= thr) & visible(j), 0.0, NEG_INF)
            return carry
        lax.fori_loop(0, i + 1, body, 0)

    def sel_ties():
        need = (n_top - count(lambda key: key > thr)).astype(F32)
        tril = (s_io >= t_io).astype(BF16)

        def body(j, run_eq):
            key = key_scr[j]
            eq = key == thr
            pre = _dot(tril, eq.astype(BF16))
            sel = ((key > thr) | (eq & (run_eq + pre <= need))) & visible(j)
            mb_scr[j] = jnp.where(sel, 0.0, NEG_INF)
            return run_eq + pre[tq - 1:tq, :]
        lax.fori_loop(0, i + 1, body, jnp.zeros((1, tq), F32))

    lax.cond(jnp.max(cnt_thr) > n_top, sel_ties, sel_plain)

    q = q_ref[...]
    q_st = [jnp.concatenate([q[:, (n * nq + g) * HEAD_DIM:(n * nq + g + 1) * HEAD_DIM] for g in range(nq)], axis=0)
            for n in range(N_KV_HEADS)]
    m_scr[...] = jnp.full(m_scr.shape, -1e30, F32)
    acc_scr[...] = jnp.zeros_like(acc_scr)

    def att_chunk(j, slot):
        mbt = mb_scr[j]
        mb4 = jnp.concatenate([mbt] * nq, axis=1)
        heads = range(N_KV_HEADS)
        lgs = [_dot_nt(kb_scr[chunk(j), n * HEAD_DIM:(n + 1) * HEAD_DIM], q_st[n]) + mb4 for n in heads]
        if slot is not None:
            lgs = [lgs[n] + jnp.concatenate([bias_ref[slot, n * nq + g] for g in range(nq)], axis=1) for n in heads]
        m_old = [m_scr[n] for n in heads]
        m_new = [jnp.maximum(m_old[n], jnp.max(lgs[n], axis=0, keepdims=True)) for n in heads]
        ps = [jnp.exp2(lgs[n] - m_new[n]).astype(BF16) for n in heads]
        pvs = [_dot(vt_scr[j, n], ps[n]) for n in heads]
        for n in heads:
            acc_scr[n] = jnp.exp2(m_old[n] - m_new[n]) * acc_scr[n] + pvs[n]
            m_scr[n] = m_new[n]

    def far_body(j, carry):
        att_chunk(j, None)
        return carry

    lax.fori_loop(0, jnp.maximum(i - 1, 0), far_body, 0)

    @pl.when(i >= 1)
    def _():
        att_chunk(i - 1, 0)

    att_chunk(i, 1)

    for n in range(N_KV_HEADS):
        acc = acc_scr[n]
        out_t = acc[0:HEAD_DIM] / acc[HEAD_DIM:HEAD_DIM + 1]
        for pair in range(nq // 2):
            g0 = 2 * pair
            two = jnp.concatenate([out_t[:, g0 * tq:(g0 + 1) * tq], out_t[:, (g0 + 1) * tq:(g0 + 2) * tq]], axis=0)
            h0 = n * nq + g0
            o_ref[:, h0 * HEAD_DIM:(h0 + 2) * HEAD_DIM] = two.T.astype(o_ref.dtype)


def _dsa_prompt(q, qi, wi, k, v, ki, bias_tab, *, batch, seq, tq):
    nblk = seq // tq
    n_top = min(TOPK_MAX, seq // 4)
    nq = N_HEADS // N_KV_HEADS
    assert tq == LANES and nq % 2 == 0 and N_KV_HEADS % 2 == 0
    blk = lambda b, i: (b * nblk + i, 0)
    whole = lambda b, i: (b, 0)
    kvw = N_KV_HEADS * HEAD_DIM
    return pl.pallas_call(
        functools.partial(_dsa_prompt_kernel, n_top=n_top, tq=tq),
        grid=(batch, nblk),
        in_specs=[pl.BlockSpec((tq, N_HEADS * HEAD_DIM), blk),
                  pl.BlockSpec((tq, N_IDX_HEADS * IDX_DIM), blk),
                  pl.BlockSpec((tq, LANES), blk),
                  pl.BlockSpec((seq, kvw), whole),
                  pl.BlockSpec((seq, kvw), whole),
                  pl.BlockSpec((seq, IDX_DIM), whole),
                  pl.BlockSpec(bias_tab.shape, lambda b, i: (0, 0, 0, 0))],
        out_specs=pl.BlockSpec((tq, N_HEADS * HEAD_DIM), blk),
        out_shape=jax.ShapeDtypeStruct((batch * seq, N_HEADS * HEAD_DIM), BF16),
        scratch_shapes=[pltpu.VMEM((seq, kvw), BF16),
                        pltpu.VMEM((nblk, N_KV_HEADS, _V_ROWS, tq), BF16),
                        pltpu.VMEM((seq, IDX_DIM), BF16),
                        pltpu.VMEM((nblk, tq, tq), I32), pltpu.VMEM((nblk, tq, tq), F32),
                        pltpu.VMEM((N_KV_HEADS, 1, nq * tq), F32),
                        pltpu.VMEM((N_KV_HEADS, _V_ROWS, nq * tq), F32)],
        compiler_params=_params("arbitrary", "arbitrary"),
    )(q, qi, wi, k, v, ki, bias_tab)


def _dsa_s_score_kernel(pt_ref, qi_ref, wi_ref, *refs, pg):
    page_refs, o_ref = refs[:pg], refs[pg]
    qi = qi_ref[0].astype(BF16)
    wi = wi_ref[0]
    idx_scale = (N_IDX_HEADS * IDX_DIM) ** -0.5
    for r in range(pg):
        d = _dot_nt(qi, page_refs[r][0].astype(BF16))
        o_ref[r, 0] = jnp.sum(jnp.maximum(d, 0.0) * wi, axis=0, keepdims=True) * idx_scale


def _dsa_s_scores(page_table, qi, wi_bc, cache_kidx, *, pg):
    db, n_pages = page_table.shape
    page_spec = lambda r: pl.BlockSpec(
        (1, PAGE_SIZE, IDX_DIM), lambda b, s, pt: (pt[b * n_pages + s * pg + r], 0, 0))
    grid_spec = pltpu.PrefetchScalarGridSpec(
        num_scalar_prefetch=1,
        grid=(db, n_pages // pg),
        in_specs=[pl.BlockSpec((1, N_IDX_HEADS, IDX_DIM), lambda b, s, pt: (b, 0, 0)),
                  pl.BlockSpec((1, N_IDX_HEADS, LANES), lambda b, s, pt: (b, 0, 0))]
                 + [page_spec(r) for r in range(pg)],
        out_specs=pl.BlockSpec((pg, 1, 1, PAGE_SIZE), lambda b, s, pt: (s, b, 0, 0)),
    )
    return pl.pallas_call(
        functools.partial(_dsa_s_score_kernel, pg=pg),
        grid_spec=grid_spec,
        out_shape=jax.ShapeDtypeStruct((n_pages, db, 1, PAGE_SIZE), F32),
        compiler_params=_params("arbitrary", "arbitrary"),
    )(page_table.reshape(-1), qi, wi_bc, *([cache_kidx] * pg))


def _dsa_s_select_kernel(sc_ref, qi_ref, kis_ref, wi_ref, hsum_ref, rep_ref, o_ref, key_scr, *, n_top):
    n_pages, db, _ = sc_ref.shape
    idx_scale = (N_IDX_HEADS * IDX_DIM) ** -0.5

    prod = (qi_ref[...].astype(BF16).astype(F32)
            * jnp.concatenate([kis_ref[...].astype(BF16).astype(F32)] * N_IDX_HEADS, axis=1))
    d_self = _dot_exact_rhs(prod, hsum_ref[...])
    s_self = jnp.sum((jnp.maximum(d_self, 0.0) * wi_ref[...]).T, axis=0, keepdims=True) * idx_scale
    key_self = _sort_key(s_self)

    def to_keys(r, carry):
        key_scr[r] = _sort_key(sc_ref[r].T)
        return carry
    lax.fori_loop(0, n_pages, to_keys, 0)

    def fold(a):
        return jnp.sum(a.reshape(PAGE_SIZE // SUBLANES, SUBLANES, db), axis=0)

    def count(pred):
        def body(rp, c):
            return c + fold(pred(key_scr[2 * rp]).astype(I32)) + fold(pred(key_scr[2 * rp + 1]).astype(I32))
        c = lax.fori_loop(0, n_pages // 2, body, jnp.zeros((SUBLANES, db), I32))
        return jnp.sum(c, axis=0, keepdims=True) + pred(key_self).astype(I32)

    thr, cnt_thr = _radix_threshold(lambda cand: count(lambda key: key >= cand), n_top, (1, db))

    def emit(r, sel):
        rows = _dot(rep_ref[...], sel.astype(BF16))
        o_ref[r] = jnp.where(rows > 0.5, 0.0, NEG_INF).T

    def sel_plain():
        def body(r, carry):
            emit(r, key_scr[r] >= thr)
            return carry
        lax.fori_loop(0, n_pages, body, 0)
        emit(n_pages, jnp.broadcast_to(key_self >= thr, (PAGE_SIZE, db)))

    def sel_ties():
        need = (n_top - count(lambda key: key > thr)).astype(F32)
        r_io = lax.broadcasted_iota(I32, (PAGE_SIZE, PAGE_SIZE), 0)
        c_io = lax.broadcasted_iota(I32, (PAGE_SIZE, PAGE_SIZE), 1)
        tril = (r_io >= c_io).astype(BF16)

        def body(r, run_eq):
            key = key_scr[r]
            eq = key == thr
            pre = _dot(tril, eq.astype(BF16))
            emit(r, (key > thr) | (eq & (run_eq + pre <= need)))
            return run_eq + pre[PAGE_SIZE - 1:PAGE_SIZE, :]
        run_eq = lax.fori_loop(0, n_pages, body, jnp.zeros((1, db), F32))
        sel_self = (key_self > thr) | ((key_self == thr) & (run_eq + 1.0 <= need))
        emit(n_pages, jnp.broadcast_to(sel_self, (PAGE_SIZE, db)))

    lax.cond(jnp.max(cnt_thr) > n_top, sel_ties, sel_plain)


def _dsa_s_select(scores, qi, ki_s, wi, *, n_top):
    n_pages, db, _ = scores.shape
    assert db == LANES and n_pages % 2 == 0
    hsum = np.zeros((N_IDX_HEADS * IDX_DIM, LANES), np.float32)
    hsum[np.arange(N_IDX_HEADS * IDX_DIM), np.arange(N_IDX_HEADS * IDX_DIM) // IDX_DIM] = 1.0
    rep = np.zeros((PAGE_SIZE * N_KV_HEADS, PAGE_SIZE), np.float32)
    rep[np.arange(PAGE_SIZE * N_KV_HEADS), np.arange(PAGE_SIZE * N_KV_HEADS) // N_KV_HEADS] = 1.0
    return pl.pallas_call(
        functools.partial(_dsa_s_select_kernel, n_top=n_top),
        out_shape=jax.ShapeDtypeStruct((n_pages + 1, db, PAGE_SIZE * N_KV_HEADS), F32),
        scratch_shapes=[pltpu.VMEM((n_pages, PAGE_SIZE, db), I32)],
        compiler_params=pltpu.CompilerParams(vmem_limit_bytes=VMEM_LIMIT),
    )(scores, qi, ki_s, wi, jnp.asarray(hsum, BF16), jnp.asarray(rep, BF16))


def _bias_table_s_kernel(rbt_ref, bkt_ref, own_ref, o_ref):
    rbt = rbt_ref[...] * LOG2E
    n_rows = bkt_ref.shape[0]
    own = own_ref[...] > 0
    for r in range(n_rows):
        bkt = bkt_ref[r:r + 1, :]
        out = jnp.zeros((N_HEADS, bkt.shape[1]), F32)
        for b in range(N_BUCKETS):
            out = jnp.where(bkt == b, rbt[:, b:b + 1], out)
        o_ref[r] = out if r == n_rows - 1 else jnp.where(own, out, NEG_INF)


def _bias_table_s(rel_bias, n_pages):
    past = n_pages * PAGE_SIZE
    width = PAGE_SIZE * N_KV_HEADS
    pos = np.arange(n_pages + 1)[:, None] * PAGE_SIZE + np.arange(width)[None, :] // N_KV_HEADS
    pos[n_pages] = past
    own = (np.arange(width)[None, :] % N_KV_HEADS) == (np.arange(N_HEADS)[:, None] // (N_HEADS // N_KV_HEADS))
    return pl.pallas_call(
        _bias_table_s_kernel,
        out_shape=jax.ShapeDtypeStruct((n_pages + 1, N_HEADS, width), F32),
    )(rel_bias.T, jnp.asarray(_t5_bucket_np(past - pos)), jnp.asarray(own, I32))


def _dsa_s_attend_kernel(pt_ref, mb_ref, bias_ref, q_ref, ks_ref, vs_ref, *refs, pg, n_pages):
    k_refs, v_refs = refs[:pg], refs[pg:2 * pg]
    o_ref, m_scr, l_scr, acc_scr = refs[2 * pg:]
    s = pl.program_id(1)
    nq = N_HEADS // N_KV_HEADS
    kvw = N_KV_HEADS * HEAD_DIM

    @pl.when(s == 0)
    def _():
        m_scr[...] = jnp.full(m_scr.shape, -1e30, F32)
        l_scr[...] = jnp.zeros_like(l_scr)
        acc_scr[...] = jnp.zeros_like(acc_scr)

    q16 = q_ref[0]
    logits = []
    for r in range(pg):
        page = s * pg + r
        logits.append(_dot_nt(q16, k_refs[r][0].astype(BF16)) + bias_ref[page] + mb_ref[page, 0])
    m_old = m_scr[...]
    m_new = m_old
    for lg in logits:
        m_new = jnp.maximum(m_new, jnp.max(lg, axis=1, keepdims=True))
    alpha = jnp.exp2(m_old - m_new)
    l_new = alpha * l_scr[...]
    acc = alpha * acc_scr[...]
    for r, lg in enumerate(logits):
        p = jnp.exp2(lg - m_new)
        l_new = l_new + jnp.sum(p, axis=1, keepdims=True)
        acc = acc + _dot(p.astype(BF16), v_refs[r][0].astype(BF16))
    m_scr[...] = m_new
    l_scr[...] = l_new
    acc_scr[...] = acc

    @pl.when(s == pl.num_programs(1) - 1)
    def _():
        kv_of_head = lax.broadcasted_iota(I32, (N_HEADS, HEAD_DIM), 0) // nq

        def own_cols(ref):
            row = ref[0].astype(BF16).astype(F32)
            out = jnp.zeros((N_HEADS, HEAD_DIM), F32)
            for n in range(N_KV_HEADS):
                out = jnp.where(kv_of_head == n, row[:, n * HEAD_DIM:(n + 1) * HEAD_DIM], out)
            return out

        lg = jnp.sum(q16.astype(F32) * own_cols(ks_ref), axis=1, keepdims=True)
        lg = lg + bias_ref[n_pages][:, 0:1] + mb_ref[n_pages, 0][:, 0:1]
        m_fin = jnp.maximum(m_new, lg)
        a = jnp.exp2(m_new - m_fin)
        p = jnp.exp2(lg - m_fin)
        l_fin = a * l_new + p
        out = (a * acc + p.astype(BF16).astype(F32) * own_cols(vs_ref)) / l_fin
        o_ref[0] = out.astype(o_ref.dtype)


def _dsa_s_attend(page_table, mask, bias_tab, q, k_s, v_s, cache_k, cache_v, *, pg):
    db, n_pages = page_table.shape
    kvw = N_KV_HEADS * HEAD_DIM
    width = PAGE_SIZE * N_KV_HEADS
    per_seq = lambda *shape: pl.BlockSpec((1,) + shape, lambda b, s, pt: (b,) + (0,) * len(shape))
    page_spec = lambda r: pl.BlockSpec(
        (1, width, HEAD_DIM), lambda b, s, pt: (pt[b * n_pages + s * pg + r], 0, 0))
    grid_spec = pltpu.PrefetchScalarGridSpec(
        num_scalar_prefetch=1,
        grid=(db, n_pages // pg),
        in_specs=[pl.BlockSpec((n_pages + 1, 1, 1, width), lambda b, s, pt: (0, b, 0, 0)),
                  pl.BlockSpec(bias_tab.shape, lambda b, s, pt: (0, 0, 0)),
                  per_seq(N_HEADS, HEAD_DIM), per_seq(1, kvw), per_seq(1, kvw)]
                 + [page_spec(r) for r in range(pg)] * 2,
        out_specs=per_seq(N_HEADS, HEAD_DIM),
        scratch_shapes=[pltpu.VMEM((N_HEADS, 1), F32), pltpu.VMEM((N_HEADS, 1), F32),
                        pltpu.VMEM((N_HEADS, HEAD_DIM), F32)],
    )
    return pl.pallas_call(
        functools.partial(_dsa_s_attend_kernel, pg=pg, n_pages=n_pages),
        grid_spec=grid_spec,
        out_shape=jax.ShapeDtypeStruct((db, N_HEADS, HEAD_DIM), BF16),
        compiler_params=_params("arbitrary", "arbitrary"),
    )(page_table.reshape(-1), mask.reshape(n_pages + 1, db, 1, width), bias_tab, q, k_s, v_s,
      *([cache_k] * pg), *([cache_v] * pg))


def _ssd_kernel(xbc_ref, z_ref, dt_ref, cw_ref, cb_ref, alog_ref, dtb_ref, dsk_ref, ng_ref, e_ref, *refs,
                rows, has_init, d_inner):
    if has_init:
        conv0_ref, ssm0_ref = refs[:2]
        refs = refs[2:]
    y_ref, ssm_ref, conv_ref, ext_scr, ht_scr, y_scr = refs
    c = pl.program_id(1)
    cl = SSD_CHUNK
    gw = d_inner // SSD_GROUPS
    hpg = gw // SSD_HEAD_DIM
    n_heads = d_inner // SSD_HEAD_DIM
    conv_dim = d_inner + 2 * SSD_GROUPS * D_STATE
    conv_w = cw_ref.shape[0]
    top = SUBLANES
    row_io = lax.broadcasted_iota(I32, (cl, 1), 0)

    def padded(ref):
        a = ref[0]
        if rows == cl:
            return a
        return jnp.where(row_io < rows, jnp.broadcast_to(a, (cl, a.shape[1])), 0.0)

    @pl.when(c == 0)
    def _():
        ext_scr[0:top, :] = jnp.zeros((top, conv_dim), F32)
        if has_init:
            ext_scr[top - conv_w + 1:top, :] = conv0_ref[0]
            for g in range(SSD_GROUPS):
                ht_scr[g] = ssm0_ref[0, g * gw:(g + 1) * gw, :].T
        else:
            ht_scr[...] = jnp.zeros_like(ht_scr)

    ext_scr[top:top + cl, :] = padded(xbc_ref)
    conv_ref[0] = ext_scr[top + rows - conv_w + 1:top + rows, :]

    cblk = 512
    for cb in range(conv_dim // cblk):
        sl = slice(cb * cblk, (cb + 1) * cblk)
        acc = jnp.broadcast_to(cb_ref[:, sl], (cl, cblk))
        for w in range(conv_w):
            acc = acc + ext_scr[top - conv_w + 1 + w:top - conv_w + 1 + w + cl, sl] * cw_ref[w:w + 1, sl]
        y_scr[:, sl] = _silu(acc)
    ext_scr[top - conv_w + 1:top, :] = ext_scr[top + cl - conv_w + 1:top + cl, :]

    dt = jax.nn.softplus(padded(dt_ref) + dtb_ref[...])
    if rows < cl:
        dt = jnp.where(row_io < rows, dt, 0.0)
    a_neg = -jnp.exp(alog_ref[...])
    r_io = lax.broadcasted_iota(I32, (cl, cl), 0)
    c_io = lax.broadcasted_iota(I32, (cl, cl), 1)
    tril = r_io >= c_io
    acs = _dot_exact_lhs(tril.astype(BF16), dt * a_neg)
    acs_t = acs.T
    acs_last = acs[cl - 1:cl, :]
    stacked = jnp.concatenate([dt, jnp.exp(acs), jnp.exp(acs_last - acs)], axis=0)
    expanded = _dot_exact_rhs(stacked, e_ref[...])
    dt_x, ea_x, te_x = expanded[0:cl], expanded[cl:2 * cl], expanded[2 * cl:3 * cl]

    z = padded(z_ref)
    for g in range(SSD_GROUPS):
        gs = slice(g * gw, (g + 1) * gw)
        x_g = y_scr[:, gs]
        b_g = y_scr[:, d_inner + g * D_STATE:d_inner + (g + 1) * D_STATE]
        c_g = y_scr[:, d_inner + (SSD_GROUPS + g) * D_STATE:d_inner + (SSD_GROUPS + g + 1) * D_STATE]
        c16 = c_g.astype(BF16)
        cbm = _dot_nt(c16, b_g.astype(BF16))
        xdt = x_g * dt_x[:, gs]
        xdt16 = xdt.astype(BF16)
        ht = ht_scr[g]
        y_g = _dot(c16, ht.astype(BF16)) * ea_x[:, gs] + dsk_ref[:, gs] * x_g
        ht_scr[g] = ht * ea_x[cl - 1:cl, gs] + _dot(b_g.T.astype(BF16), (xdt * te_x[:, gs]).astype(BF16))
        diag = []
        for r in range(hpg):
            h = g * hpg + r
            seg = acs[:, h:h + 1] - acs_t[h:h + 1, :]
            m = (cbm * jnp.exp(jnp.where(tril, seg, NEG_INF))).astype(BF16)
            diag.append(_dot(m, xdt16[:, r * SSD_HEAD_DIM:(r + 1) * SSD_HEAD_DIM]))
        y_g = (y_g + jnp.concatenate(diag, axis=1)) * _silu(z[:, gs])
        y_g = y_g * lax.rsqrt(jnp.mean(y_g * y_g, axis=1, keepdims=True) + EPS) * ng_ref[:, gs]
        y_ref[0, :, gs] = y_g[0:rows].astype(y_ref.dtype)

    @pl.when(c == pl.num_programs(1) - 1)
    def _():
        for g in range(SSD_GROUPS):
            ssm_ref[0, g * gw:(g + 1) * gw, :] = ht_scr[g].T


def _ssd(xbc, z, dt, conv_w, conv_b, a_log, dt_bias, d_skip, norm_g, *, n_seq, n_chunks, rows,
         conv0=None, ssm0=None):
    conv_dim = xbc.shape[-1]
    d_inner = z.shape[-1]
    n_heads = d_inner // SSD_HEAD_DIM
    cw = conv_w.shape[0]
    has_init = conv0 is not None
    pad = lambda a: jnp.pad(a, (0, LANES - a.shape[0])).reshape(1, LANES)
    expand = np.zeros((LANES, d_inner), np.float32)
    expand[np.arange(d_inner) // SSD_HEAD_DIM, np.arange(d_inner)] = 1.0
    step = lambda b, c: (b * n_chunks + c, 0, 0)
    seq = lambda b, c: (b, 0, 0)
    const = lambda b, c: (0, 0)
    in_specs = [pl.BlockSpec((1, rows, conv_dim), step), pl.BlockSpec((1, rows, d_inner), step),
                pl.BlockSpec((1, rows, LANES), step),
                pl.BlockSpec((cw, conv_dim), const), pl.BlockSpec((1, conv_dim), const),
                pl.BlockSpec((1, LANES), const), pl.BlockSpec((1, LANES), const),
                pl.BlockSpec((1, d_inner), const), pl.BlockSpec((1, d_inner), const),
                pl.BlockSpec((LANES, d_inner), const)]
    args = [xbc, z, dt, conv_w, conv_b.reshape(1, conv_dim), pad(a_log), pad(dt_bias),
            jnp.repeat(d_skip, SSD_HEAD_DIM).reshape(1, d_inner), norm_g.reshape(1, d_inner),
            jnp.asarray(expand, BF16)]
    if has_init:
        in_specs += [pl.BlockSpec((1, cw - 1, conv_dim), seq), pl.BlockSpec((1, d_inner, D_STATE), seq)]
        args += [conv0, ssm0]
    return pl.pallas_call(
        functools.partial(_ssd_kernel, rows=rows, has_init=has_init, d_inner=d_inner),
        grid=(n_seq, n_chunks),
        in_specs=in_specs,
        out_specs=[pl.BlockSpec((1, rows, d_inner), step), pl.BlockSpec((1, d_inner, D_STATE), seq),
                   pl.BlockSpec((1, cw - 1, conv_dim), seq)],
        out_shape=[jax.ShapeDtypeStruct((n_seq * n_chunks, rows, d_inner), BF16),
                   jax.ShapeDtypeStruct((n_seq, d_inner, D_STATE), F32),
                   jax.ShapeDtypeStruct((n_seq, cw - 1, conv_dim), F32)],
        scratch_shapes=[pltpu.VMEM((SUBLANES + SSD_CHUNK, conv_dim), F32),
                        pltpu.VMEM((SSD_GROUPS, D_STATE, d_inner // SSD_GROUPS), F32),
                        pltpu.VMEM((SSD_CHUNK, conv_dim), F32)],
        compiler_params=_params("arbitrary", "arbitrary"),
    )(*args)


def _merge_kernel(x_ref, gt_ref, att_ref, ssd_ref, ga_ref, gs_ref, wa_ref, ws_ref, wo_ref, o_ref):
    merged = (jax.nn.sigmoid(ga_ref[...]) * _dot(att_ref[...], wa_ref[...])
              + jax.nn.sigmoid(gs_ref[...]) * _dot(ssd_ref[...], ws_ref[...]))
    o_ref[...] = x_ref[...] + gt_ref[0] * _dot(merged.astype(BF16), wo_ref[...])


def _merge(x, gate, att, ssd_y, gate_a, gate_s, w_a, w_s, w_o, *, tm):
    m, d = x.shape
    row = lambda i: (i, 0)
    const = lambda i: (0, 0)
    return pl.pallas_call(
        _merge_kernel,
        grid=(m // tm,),
        in_specs=[pl.BlockSpec((tm, d), row), gate.spec(tm),
                  pl.BlockSpec((tm, att.shape[1]), row), pl.BlockSpec((tm, ssd_y.shape[1]), row),
                  pl.BlockSpec((tm, d), row), pl.BlockSpec((tm, d), row),
                  pl.BlockSpec(w_a.shape, const), pl.BlockSpec(w_s.shape, const), pl.BlockSpec(w_o.shape, const)],
        out_specs=pl.BlockSpec((tm, d), row),
        out_shape=jax.ShapeDtypeStruct((m, d), F32),
        compiler_params=_params("parallel"),
    )(x, gate.arr, att, ssd_y, gate_a, gate_s, w_a, w_s, w_o)


def _pad_cols(w, width):
    return jnp.pad(w, ((0, 0), (0, width - w.shape[1])))


def _trunk(x, mods, rows_per_seq, p, tm, attend, ssd_fn, final_g):
    sh1, sc1, g1, sh2, sc2, g2, sh3, sc3, g3 = [_Mod(a, rows_per_seq) for a in mods]
    x = _ffn(x, sh1, sc1, g1, p["norm_ffn1"], p["w_ffn1_in"], p["w_ffn1_out"], final_g, tm=tm, final_norm=False)
    q, k, v, qi, ki, wi = _proj(x, sh2, sc2, p["norm_mix"], p["w_att"], p["seg_att"], tm=tm)
    z, xbc, dt = _proj(x, sh2, sc2, p["norm_mix"], p["w_ssd"], p["seg_ssd"], tm=tm)
    gate_a, gate_s = _proj(x, sh2, sc2, p["norm_mix"], p["w_gate"], p["seg_gate"], tm=tm)
    att = attend(q, k, v, qi, ki, wi)
    ssd_y, ssm_new, conv_new = ssd_fn(z, xbc, dt)
    x = _merge(x, g2, att, ssd_y, gate_a, gate_s, p["w_attn_out"], p["w_ssd_out"], p["w_out"], tm=tm)
    y = _ffn(x, sh3, sc3, g3, p["norm_ffn2"], p["w_ffn2_in"], p["w_ffn2_out"], final_g, tm=tm, final_norm=True)
    return y, (k, v, ki, ssm_new, conv_new)


def kernel(x_prompt, x_sample, c_prompt, c_sample, cache_k, cache_v, cache_kidx, state_ssm, state_conv, page_table,
           w_ada, b_ada, norm_ffn1, w_ffn1_in, w_ffn1_out, norm_mix, w_in, rel_bias, conv_w, conv_b, a_log, dt_bias,
           d_skip, norm_ssd, w_attn_out, w_ssd_out, w_out, norm_ffn2, w_ffn2_in, w_ffn2_out, norm_final):
    depth = w_ada.shape[0]
    assert depth == 1
    batch, seq, d = x_prompt.shape
    db, dec_seq, _ = x_sample.shape
    assert dec_seq == 1
    n_pool = cache_k.shape[1]
    n_pages = page_table.shape[1]
    d_inner = norm_ssd.shape[1]
    conv_dim = conv_w.shape[2]
    n_ssd_heads = d_inner // SSD_HEAD_DIM
    att_q = N_HEADS * HEAD_DIM
    att_kv = N_KV_HEADS * HEAD_DIM
    idx_q = N_IDX_HEADS * IDX_DIM
    l = 0

    widths = (att_q, att_kv, att_kv, idx_q, IDX_DIM, N_IDX_HEADS, d_inner, conv_dim, n_ssd_heads, d, d)
    bounds = np.concatenate([[0], np.cumsum(widths)])
    assert bounds[-1] == w_in.shape[2]
    cols = [w_in[l][:, bounds[i]:bounds[i + 1]].astype(BF16) for i in range(len(widths))]
    w_q, w_k, w_v, w_qi, w_ki, w_wi, w_z, w_xbc, w_dt, w_ga, w_gs = cols
    row1 = lambda a: a.reshape(1, -1)
    p = {
        "norm_ffn1": row1(norm_ffn1[l]), "w_ffn1_in": w_ffn1_in[l].astype(BF16), "w_ffn1_out": w_ffn1_out[l].astype(BF16),
        "norm_mix": row1(norm_mix[l]),
        "w_att": jnp.concatenate([w_q, w_k, w_v, w_qi, _pad_cols(w_ki, LANES), _pad_cols(w_wi, LANES)], axis=1),
        "seg_att": [(att_q, BF16, HEAD_DIM ** -0.5 * LOG2E), (att_kv, F32, 1.0), (att_kv, F32, 1.0),
                    (idx_q, BF16, 1.0), (IDX_DIM, F32, 1.0), (LANES, F32, 1.0)],
        "w_ssd": jnp.concatenate([w_z, w_xbc, _pad_cols(w_dt, LANES)], axis=1),
        "seg_ssd": [(d_inner, F32, 1.0), (conv_dim, F32, 1.0), (LANES, F32, 1.0)],
        "w_gate": jnp.concatenate([w_ga, w_gs], axis=1),
        "seg_gate": [(d, F32, 1.0), (d, F32, 1.0)],
        "w_attn_out": w_attn_out[l].astype(BF16), "w_ssd_out": w_ssd_out[l].astype(BF16), "w_out": w_out[l].astype(BF16),
        "norm_ffn2": row1(norm_ffn2[l]), "w_ffn2_in": w_ffn2_in[l].astype(BF16), "w_ffn2_out": w_ffn2_out[l].astype(BF16),
    }
    final_g = row1(norm_final)
    ssd_args = (conv_w[l], conv_b[l], a_log[l], dt_bias[l], d_skip[l], norm_ssd[l])

    ada = _ada(jnp.concatenate([c_prompt, c_sample], axis=0), w_ada[l], b_ada[l])
    ada_p = [a.reshape(batch, 1, d) for a in jnp.split(ada[:batch], 9, axis=1)]
    ada_s = [a.reshape(1, db, d) for a in jnp.split(ada[batch:], 9, axis=1)]

    tq = 128
    n_chunks = seq // SSD_CHUNK
    bias_tab = _bias_table(rel_bias, tq, seq)

    def attend_p(q, k, v, qi, ki, wi):
        return _dsa_prompt(q, qi, wi, k, v, ki, bias_tab, batch=batch, seq=seq, tq=tq)

    def ssd_p(z, xbc, dt):
        r3 = lambda a: a.reshape(batch * n_chunks, SSD_CHUNK, a.shape[-1])
        y, ssm, conv = _ssd(r3(xbc), r3(z), r3(dt), *ssd_args, n_seq=batch, n_chunks=n_chunks, rows=SSD_CHUNK)
        return y.reshape(batch * seq, d_inner), ssm, conv

    yp, (k_p, v_p, ki_p, ssm_p, conv_p) = _trunk(x_prompt.reshape(batch * seq, d), ada_p, seq, p, 512,
                                                 attend_p, ssd_p, final_g)

    pg = min(16, n_pages)
    n_top_s = min(TOPK_MAX, (n_pages * PAGE_SIZE + 1) // 4)
    bias_tab_s = _bias_table_s(rel_bias, n_pages)

    def attend_s(q, k, v, qi, ki, wi):
        wi_bc = jnp.broadcast_to(wi[:, :N_IDX_HEADS, None], (db, N_IDX_HEADS, LANES))
        scores = _dsa_s_scores(page_table, qi.reshape(db, N_IDX_HEADS, IDX_DIM), wi_bc, cache_kidx[l], pg=pg)
        mask = _dsa_s_select(scores.reshape(n_pages, db, PAGE_SIZE), qi, ki, wi, n_top=n_top_s)
        att = _dsa_s_attend(page_table, mask, bias_tab_s, q.reshape(db, N_HEADS, HEAD_DIM),
                            k.reshape(db, 1, att_kv), v.reshape(db, 1, att_kv),
                            cache_k[l].reshape(n_pool, PAGE_SIZE * N_KV_HEADS, HEAD_DIM),
                            cache_v[l].reshape(n_pool, PAGE_SIZE * N_KV_HEADS, HEAD_DIM), pg=pg)
        return att.reshape(db, att_q)

    def ssd_s(z, xbc, dt):
        r3 = lambda a: a.reshape(db, 1, a.shape[-1])
        y, ssm, conv = _ssd(r3(xbc), r3(z), r3(dt), *ssd_args, n_seq=db, n_chunks=1, rows=1,
                            conv0=state_conv[l], ssm0=state_ssm[l].reshape(db, d_inner, D_STATE))
        return y.reshape(db, d_inner), ssm, conv

    ys, (k_s, v_s, ki_s, ssm_s, conv_s) = _trunk(x_sample.reshape(db, d), ada_s, db, p, db,
                                                 attend_s, ssd_s, final_g)

    st = lambda a, *shape: a.reshape((1,) + shape)
    return (yp.reshape(batch, seq, d), ys.reshape(db, 1, d),
            st(k_p, batch, seq, N_KV_HEADS, HEAD_DIM), st(v_p, batch, seq, N_KV_HEADS, HEAD_DIM),
            st(ki_p, batch, seq, IDX_DIM),
            st(ssm_p, batch, n_ssd_heads, SSD_HEAD_DIM, D_STATE), st(conv_p, batch, conv_w.shape[1] - 1, conv_dim),
            st(k_s, db, 1, N_KV_HEADS, HEAD_DIM), st(v_s, db, 1, N_KV_HEADS, HEAD_DIM), st(ki_s, db, 1, IDX_DIM),
            st(ssm_s, db, n_ssd_heads, SSD_HEAD_DIM, D_STATE), st(conv_s, db, conv_w.shape[1] - 1, conv_dim))
```

```python
import functools
import math

import numpy as np
import jax
import jax.numpy as jnp
from jax import lax
from jax.experimental import pallas as pl
from jax.experimental.pallas import tpu as pltpu

N_HEADS = 16
HEAD_DIM = 64
N_KV_HEADS = 4
N_IDX_HEADS = 8
IDX_DIM = 64
TOPK_MAX = 256
N_BUCKETS = 32
MAX_DISTANCE = 128
SSD_HEAD_DIM = 64
SSD_GROUPS = 4
D_STATE = 128
SSD_CHUNK = 128
EPS = 1e-6
PAGE_SIZE = 128

LANES = 128
SUBLANES = 8
VMEM_LIMIT = 56 * 1024 * 1024

F32 = jnp.float32
BF16 = jnp.bfloat16
I32 = jnp.int32
NEG_INF = float("-inf")
INT_MIN = -(2 ** 31)
LOG2E = math.log2(math.e)

_NT = (((1,), (1,)), ((), ()))


def _dot(a, b):
    return jnp.dot(a, b, preferred_element_type=F32)


def _dot_nt(a, b):
    return lax.dot_general(a, b, _NT, preferred_element_type=F32)


def _split3(a):
    hi = a.astype(BF16)
    r = a - hi.astype(F32)
    mid = r.astype(BF16)
    lo = (r - mid.astype(F32)).astype(BF16)
    return hi, mid, lo


def _dot_exact_rhs(a, b_bf16):
    hi, mid, lo = _split3(a)
    return _dot(hi, b_bf16) + _dot(mid, b_bf16) + _dot(lo, b_bf16)


def _dot_exact_lhs(a_bf16, b):
    hi, mid, lo = _split3(b)
    return _dot(a_bf16, hi) + _dot(a_bf16, mid) + _dot(a_bf16, lo)


def _rmsnorm(x, g):
    return (x * lax.rsqrt(jnp.mean(x * x, axis=-1, keepdims=True) + EPS)) * g


def _sigmoid(x):
    return 0.5 * jnp.tanh(0.5 * x) + 0.5


def _silu(x):
    h = 0.5 * x
    return h * jnp.tanh(h) + h


def _params(*sem):
    return pltpu.CompilerParams(dimension_semantics=sem, vmem_limit_bytes=VMEM_LIMIT)


def _t5_bucket_np(dist):
    n = np.maximum(dist, 0)
    max_exact = N_BUCKETS // 2
    nf = np.maximum(n, 1).astype(np.float32)
    val = (np.log(nf / np.float32(max_exact)) / np.float32(math.log(MAX_DISTANCE / max_exact))
           * np.float32(N_BUCKETS - max_exact)).astype(np.float32)
    frac = np.abs(val - np.round(val))
    knife = (frac < 1e-3) & (n > max_exact) & (val < N_BUCKETS - max_exact - 0.5)
    assert not knife.any()
    large = np.minimum(max_exact + val.astype(np.int32), N_BUCKETS - 1)
    return np.where(n < max_exact, n, large).astype(np.int32)


def _ada_kernel(c_ref, w_ref, b_ref, o_ref):
    h = _silu(c_ref[...]).astype(BF16)
    o_ref[...] = _dot(h, w_ref[...].astype(BF16)) + b_ref[...]


def _ada(c, w, b):
    rows, d = c.shape
    n = w.shape[1]
    tn = 1024
    return pl.pallas_call(
        _ada_kernel,
        grid=(n // tn,),
        in_specs=[pl.BlockSpec((rows, d), lambda j: (0, 0)),
                  pl.BlockSpec((d, tn), lambda j: (0, j)),
                  pl.BlockSpec((1, tn), lambda j: (0, j))],
        out_specs=pl.BlockSpec((rows, tn), lambda j: (0, j)),
        out_shape=jax.ShapeDtypeStruct((rows, n), F32),
        compiler_params=_params("arbitrary"),
    )(c, w, b.reshape(1, n))


class _Mod:
    def __init__(self, arr, rows_per_seq):
        self.arr = arr
        self.rows_per_seq = rows_per_seq

    def spec(self, tm):
        r = self.arr.shape[1]
        d = self.arr.shape[2]
        if r == 1:
            per = self.rows_per_seq // tm
            return pl.BlockSpec((1, 1, d), lambda i, *_: (i // per, 0, 0))
        assert r == tm
        return pl.BlockSpec((1, r, d), lambda i, *_: (i, 0, 0))


def _ffn_kernel(x_ref, sh_ref, sc_ref, gt_ref, g_ref, wg_ref, wu_ref, wo_ref, fg_ref, o_ref, h_scr, acc_scr,
                *, final_norm):
    j = pl.program_id(1)

    @pl.when(j == 0)
    def _():
        h = _rmsnorm(x_ref[...], g_ref[...]) * (1.0 + sc_ref[0]) + sh_ref[0]
        h_scr[...] = h.astype(BF16)
        acc_scr[...] = jnp.zeros_like(acc_scr)

    h = h_scr[...]
    gate = _dot(h, wg_ref[...])
    up = _dot(h, wu_ref[...])
    act = (_silu(gate) * up).astype(BF16)
    acc_scr[...] += _dot(act, wo_ref[...])

    @pl.when(j == pl.num_programs(1) - 1)
    def _():
        out = x_ref[...] + 0.5 * gt_ref[0] * acc_scr[...]
        if final_norm:
            out = _rmsnorm(out, fg_ref[...])
        o_ref[...] = out


def _ffn(x, shift, scale, gate, norm_g, w_in, w_out, final_g, *, tm, final_norm):
    m, d = x.shape
    f = w_out.shape[0]
    nj = 2
    tf = f // nj
    assert tf % LANES == 0 and m % tm == 0
    row = lambda i, j: (i, 0)
    const = lambda i, j: (0, 0)
    return pl.pallas_call(
        functools.partial(_ffn_kernel, final_norm=final_norm),
        grid=(m // tm, nj),
        in_specs=[pl.BlockSpec((tm, d), row),
                  shift.spec(tm), scale.spec(tm), gate.spec(tm),
                  pl.BlockSpec((1, d), const),
                  pl.BlockSpec((d, tf), lambda i, j: (0, j)),
                  pl.BlockSpec((d, tf), lambda i, j: (0, j + nj)),
                  pl.BlockSpec((tf, d), lambda i, j: (j, 0)),
                  pl.BlockSpec((1, d), const)],
        out_specs=pl.BlockSpec((tm, d), row),
        out_shape=jax.ShapeDtypeStruct((m, d), F32),
        scratch_shapes=[pltpu.VMEM((tm, d), BF16), pltpu.VMEM((tm, d), F32)],
        compiler_params=_params("parallel", "arbitrary"),
    )(x, shift.arr, scale.arr, gate.arr, norm_g, w_in, w_in, w_out, final_g)


def _proj_kernel(x_ref, sh_ref, sc_ref, g_ref, w_ref, *o_refs, offsets, scales):
    h = (_rmsnorm(x_ref[...], g_ref[...]) * (1.0 + sc_ref[0]) + sh_ref[0]).astype(BF16)
    for o_ref, off, scale in zip(o_refs, offsets, scales):
        width = o_ref.shape[1]
        out = _dot(h, w_ref[:, off:off + width])
        if scale != 1.0:
            out = out * scale
        o_ref[...] = out.astype(o_ref.dtype)


def _proj(x, shift, scale, norm_g, w, segments, *, tm):
    m, d = x.shape
    offsets, off = [], 0
    for width, _, _ in segments:
        offsets.append(off)
        off += -(-width // LANES) * LANES
    assert off == w.shape[1]
    row = lambda i: (i, 0)
    const = lambda i: (0, 0)
    return pl.pallas_call(
        functools.partial(_proj_kernel, offsets=tuple(offsets), scales=tuple(s for _, _, s in segments)),
        grid=(m // tm,),
        in_specs=[pl.BlockSpec((tm, d), row), shift.spec(tm), scale.spec(tm),
                  pl.BlockSpec((1, d), const), pl.BlockSpec(w.shape, const)],
        out_specs=[pl.BlockSpec((tm, width), row) for width, _, _ in segments],
        out_shape=[jax.ShapeDtypeStruct((m, width), dt) for width, dt, _ in segments],
        compiler_params=_params("parallel"),
    )(x, shift.arr, scale.arr, norm_g, w)


def _sort_key(score):
    score = jnp.where(score == 0.0, 0.0, score)
    bits = pltpu.bitcast(score, I32)
    return jnp.where(bits >= 0, bits, bits ^ jnp.int32(0x7FFFFFFF))


def _radix_threshold(count_ge, n_top, shape):
    def bit_body(t, carry):
        thr_u, cnt_thr = carry
        cand_u = thr_u | jnp.left_shift(jnp.int32(1), 31 - t)
        cnt = count_ge(cand_u ^ jnp.int32(INT_MIN))
        take = cnt >= n_top
        return jnp.where(take, cand_u, thr_u), jnp.where(take, cnt, cnt_thr)

    thr_u, cnt_thr = lax.fori_loop(0, 32, bit_body, (jnp.zeros(shape, I32), jnp.zeros(shape, I32)))
    return thr_u ^ jnp.int32(INT_MIN), cnt_thr


def _bias_table_kernel(rb_ref, bkt_ref, o_ref):
    far = N_BUCKETS - 1
    for slot in range(bkt_ref.shape[0]):
        bkt = bkt_ref[slot]
        for h in range(N_HEADS):
            acc = jnp.zeros(bkt.shape, F32)
            for b in range(N_BUCKETS - 1):
                acc = jnp.where(bkt == b, (rb_ref[b, h] - rb_ref[far, h]) * LOG2E, acc)
            o_ref[slot, h] = acc


def _bias_table(rel_bias, tq, seq):
    s = np.arange(tq)[:, None]
    t = np.arange(tq)[None, :]
    bkt = np.stack([_t5_bucket_np(t + tq - s), _t5_bucket_np(t - s)])
    assert (_t5_bucket_np(np.arange(tq + 1, seq + 1)) == N_BUCKETS - 1).all()
    return pl.pallas_call(
        _bias_table_kernel,
        in_specs=[pl.BlockSpec(memory_space=pltpu.SMEM), pl.BlockSpec(memory_space=pltpu.VMEM)],
        out_specs=pl.BlockSpec(memory_space=pltpu.VMEM),
        out_shape=jax.ShapeDtypeStruct((2, N_HEADS, tq, tq), F32),
    )(rel_bias, jnp.asarray(bkt))


_V_ROWS = HEAD_DIM + 16


def _dsa_prompt_kernel(q_ref, qi_ref, wi_ref, k_ref, v_ref, ki_ref, bias_ref, o_ref,
                       kb_scr, vt_scr, kib_scr, key_scr, mb_scr, m_scr, acc_scr, *, n_top, tq):
    i = pl.program_id(1)
    nq = N_HEADS // N_KV_HEADS
    nblk = key_scr.shape[0]
    s_io = lax.broadcasted_iota(I32, (tq, tq), 0)
    t_io = lax.broadcasted_iota(I32, (tq, tq), 1)
    causal = s_io <= t_io

    @pl.when(i == 0)
    def _():
        kb_scr[...] = k_ref[...].astype(BF16)
        kib_scr[...] = ki_ref[...].astype(BF16)
        for c in range(nblk):
            for pair in range(N_KV_HEADS // 2):
                vt = v_ref[c * tq:(c + 1) * tq, pair * 2 * HEAD_DIM:(pair + 1) * 2 * HEAD_DIM].T.astype(BF16)
                vt_scr[c, 2 * pair, 0:HEAD_DIM, :] = vt[0:HEAD_DIM]
                vt_scr[c, 2 * pair + 1, 0:HEAD_DIM, :] = vt[HEAD_DIM:2 * HEAD_DIM]
            for n in range(N_KV_HEADS):
                vt_scr[c, n, HEAD_DIM:_V_ROWS, :] = jnp.ones((_V_ROWS - HEAD_DIM, tq), BF16)

    def chunk(j):
        return pl.ds(pl.multiple_of(j * tq, tq), tq)

    def fold(a):
        return jnp.sum(a.reshape(tq // SUBLANES, SUBLANES, tq), axis=0)

    qi = qi_ref[...]
    qi_st = jnp.concatenate([qi[:, h * IDX_DIM:(h + 1) * IDX_DIM] for h in range(N_IDX_HEADS)], axis=0)
    wi_t = wi_ref[...].T
    wi_row = jnp.concatenate([wi_t[h:h + 1, :] for h in range(N_IDX_HEADS)], axis=1)
    idx_scale = (N_IDX_HEADS * IDX_DIM) ** -0.5

    def visible(j):
        return (j < i) | ((j == i) & causal)

    n_pairs = (i + 2) // 2

    def score_body(jp, carry):
        for j in (2 * jp, 2 * jp + 1):
            d = jnp.maximum(_dot_nt(kib_scr[chunk(j), :], qi_st), 0.0) * wi_row
            s = d[:, 0:tq]
            for h in range(1, N_IDX_HEADS):
                s = s + d[:, h * tq:(h + 1) * tq]
            key_scr[j] = _sort_key(jnp.where(visible(j), s * idx_scale, NEG_INF))
        return carry

    lax.fori_loop(0, n_pairs, score_body, 0)

    def count(pred):
        def body(jp, c):
            return c + fold(pred(key_scr[2 * jp]).astype(I32)) + fold(pred(key_scr[2 * jp + 1]).astype(I32))
        c = lax.fori_loop(0, n_pairs, body, jnp.zeros((SUBLANES, tq), I32))
        return jnp.sum(c, axis=0, keepdims=True)

    thr, cnt_thr = _radix_threshold(lambda cand: count(lambda key: key >= cand), n_top, (1, tq))

    def sel_plain():
        def body(j, carry):
            mb_scr[j] = jnp.where((key_scr[j] >= thr) & visible(j), 0.0, NEG_INF)
            return carry
        lax.fori_loop(0, i + 1, body, 0)

    def sel_ties():
        need = (n_top - count(lambda key: key > thr)).astype(F32)
        tril = (s_io >= t_io).astype(BF16)

        def body(j, run_eq):
            key = key_scr[j]
            eq = key == thr
            pre = _dot(tril, eq.astype(BF16))
            sel = ((key > thr) | (eq & (run_eq + pre <= need))) & visible(j)
            mb_scr[j] = jnp.where(sel, 0.0, NEG_INF)
            return run_eq + pre[tq - 1:tq, :]
        lax.fori_loop(0, i + 1, body, jnp.zeros((1, tq), F32))

    lax.cond(jnp.max(cnt_thr) > n_top, sel_ties, sel_plain)

    q = q_ref[...]
    q_st = [jnp.concatenate([q[:, (n * nq + g) * HEAD_DIM:(n * nq + g + 1) * HEAD_DIM] for g in range(nq)], axis=0)
            for n in range(N_KV_HEADS)]
    m_scr[...] = jnp.full(m_scr.shape, -1e30, F32)
    acc_scr[...] = jnp.zeros_like(acc_scr)

    def att_chunk(j, slot):
        mbt = mb_scr[j]
        mb4 = jnp.concatenate([mbt] * nq, axis=1)
        heads = range(N_KV_HEADS)
        lgs = [_dot_nt(kb_scr[chunk(j), n * HEAD_DIM:(n + 1) * HEAD_DIM], q_st[n]) + mb4 for n in heads]
        if slot is not None:
            lgs = [lgs[n] + jnp.concatenate([bias_ref[slot, n * nq + g] for g in range(nq)], axis=1) for n in heads]
        m_old = [m_scr[n] for n in heads]
        m_new = [jnp.maximum(m_old[n], jnp.max(lgs[n], axis=0, keepdims=True)) for n in heads]
        ps = [jnp.exp2(lgs[n] - m_new[n]).astype(BF16) for n in heads]
        pvs = [_dot(vt_scr[j, n], ps[n]) for n in heads]
        for n in heads:
            acc_scr[n] = jnp.exp2(m_old[n] - m_new[n]) * acc_scr[n] + pvs[n]
            m_scr[n] = m_new[n]

    def far_body(j, carry):
        att_chunk(j, None)
        return carry

    lax.fori_loop(0, jnp.maximum(i - 1, 0), far_body, 0)

    @pl.when(i >= 1)
    def _():
        att_chunk(i - 1, 0)

    att_chunk(i, 1)

    for n in range(N_KV_HEADS):
        acc = acc_scr[n]
        out_t = acc[0:HEAD_DIM] / acc[HEAD_DIM:HEAD_DIM + 1]
        for pair in range(nq // 2):
            g0 = 2 * pair
            two = jnp.concatenate([out_t[:, g0 * tq:(g0 + 1) * tq], out_t[:, (g0 + 1) * tq:(g0 + 2) * tq]], axis=0)
            h0 = n * nq + g0
            o_ref[:, h0 * HEAD_DIM:(h0 + 2) * HEAD_DIM] = two.T.astype(o_ref.dtype)


def _dsa_prompt(q, qi, wi, k, v, ki, bias_tab, *, batch, seq, tq):
    nblk = seq // tq
    n_top = min(TOPK_MAX, seq // 4)
    nq = N_HEADS // N_KV_HEADS
    assert tq == LANES and nq % 2 == 0 and N_KV_HEADS % 2 == 0
    blk = lambda b, i: (b * nblk + i, 0)
    whole = lambda b, i: (b, 0)
    kvw = N_KV_HEADS * HEAD_DIM
    return pl.pallas_call(
        functools.partial(_dsa_prompt_kernel, n_top=n_top, tq=tq),
        grid=(batch, nblk),
        in_specs=[pl.BlockSpec((tq, N_HEADS * HEAD_DIM), blk),
                  pl.BlockSpec((tq, N_IDX_HEADS * IDX_DIM), blk),
                  pl.BlockSpec((tq, LANES), blk),
                  pl.BlockSpec((seq, kvw), whole),
                  pl.BlockSpec((seq, kvw), whole),
                  pl.BlockSpec((seq, IDX_DIM), whole),
                  pl.BlockSpec(bias_tab.shape, lambda b, i: (0, 0, 0, 0))],
        out_specs=pl.BlockSpec((tq, N_HEADS * HEAD_DIM), blk),
        out_shape=jax.ShapeDtypeStruct((batch * seq, N_HEADS * HEAD_DIM), BF16),
        scratch_shapes=[pltpu.VMEM((seq, kvw), BF16),
                        pltpu.VMEM((nblk, N_KV_HEADS, _V_ROWS, tq), BF16),
                        pltpu.VMEM((seq, IDX_DIM), BF16),
                        pltpu.VMEM((nblk, tq, tq), I32), pltpu.VMEM((nblk, tq, tq), F32),
                        pltpu.VMEM((N_KV_HEADS, 1, nq * tq), F32),
                        pltpu.VMEM((N_KV_HEADS, _V_ROWS, nq * tq), F32)],
        compiler_params=_params("arbitrary", "arbitrary"),
    )(q, qi, wi, k, v, ki, bias_tab)


def _dsa_s_score_kernel(pt_ref, qi_ref, wi_ref, *refs, pg):
    page_refs, o_ref = refs[:pg], refs[pg]
    qi = qi_ref[0].astype(BF16)
    wi = wi_ref[0]
    idx_scale = (N_IDX_HEADS * IDX_DIM) ** -0.5
    for r in range(pg):
        d = _dot_nt(qi, page_refs[r][0])
        o_ref[r, 0] = jnp.sum(jnp.maximum(d, 0.0) * wi, axis=0, keepdims=True) * idx_scale


def _dsa_s_scores(page_table, qi, wi_bc, cache_kidx, *, pg):
    db, n_pages = page_table.shape
    page_spec = lambda r: pl.BlockSpec(
        (1, PAGE_SIZE, IDX_DIM), lambda b, s, pt: (pt[b * n_pages + s * pg + r], 0, 0))
    grid_spec = pltpu.PrefetchScalarGridSpec(
        num_scalar_prefetch=1,
        grid=(db, n_pages // pg),
        in_specs=[pl.BlockSpec((1, N_IDX_HEADS, IDX_DIM), lambda b, s, pt: (b, 0, 0)),
                  pl.BlockSpec((1, N_IDX_HEADS, LANES), lambda b, s, pt: (b, 0, 0))]
                 + [page_spec(r) for r in range(pg)],
        out_specs=pl.BlockSpec((pg, 1, 1, PAGE_SIZE), lambda b, s, pt: (s, b, 0, 0)),
    )
    return pl.pallas_call(
        functools.partial(_dsa_s_score_kernel, pg=pg),
        grid_spec=grid_spec,
        out_shape=jax.ShapeDtypeStruct((n_pages, db, 1, PAGE_SIZE), F32),
        compiler_params=_params("arbitrary", "arbitrary"),
    )(page_table.reshape(-1), qi, wi_bc, *([cache_kidx] * pg))


def _dsa_s_select_kernel(sc_ref, qi_ref, kis_ref, wi_ref, hsum_ref, o_ref, key_scr, *, n_top):
    n_pages, db, _ = sc_ref.shape
    idx_scale = (N_IDX_HEADS * IDX_DIM) ** -0.5

    prod = (qi_ref[...].astype(BF16).astype(F32)
            * jnp.concatenate([kis_ref[...].astype(BF16).astype(F32)] * N_IDX_HEADS, axis=1))
    d_self = _dot_exact_rhs(prod, hsum_ref[...])
    s_self = jnp.sum((jnp.maximum(d_self, 0.0) * wi_ref[...]).T, axis=0, keepdims=True) * idx_scale
    key_self = _sort_key(s_self)

    def to_keys(r, carry):
        key_scr[r] = _sort_key(sc_ref[r].T)
        return carry
    lax.fori_loop(0, n_pages, to_keys, 0)

    def fold(a):
        return jnp.sum(a.reshape(PAGE_SIZE // SUBLANES, SUBLANES, db), axis=0)

    def count(pred):
        def body(rp, c):
            return c + fold(pred(key_scr[2 * rp]).astype(I32)) + fold(pred(key_scr[2 * rp + 1]).astype(I32))
        c = lax.fori_loop(0, n_pages // 2, body, jnp.zeros((SUBLANES, db), I32))
        return jnp.sum(c, axis=0, keepdims=True) + pred(key_self).astype(I32)

    thr, cnt_thr = _radix_threshold(lambda cand: count(lambda key: key >= cand), n_top, (1, db))

    def emit(r, sel):
        o_ref[r] = jnp.where(sel, 0.0, NEG_INF).T

    def sel_plain():
        def body(r, carry):
            emit(r, key_scr[r] >= thr)
            return carry
        lax.fori_loop(0, n_pages, body, 0)
        emit(n_pages, jnp.broadcast_to(key_self >= thr, (PAGE_SIZE, db)))

    def sel_ties():
        need = (n_top - count(lambda key: key > thr)).astype(F32)
        r_io = lax.broadcasted_iota(I32, (PAGE_SIZE, PAGE_SIZE), 0)
        c_io = lax.broadcasted_iota(I32, (PAGE_SIZE, PAGE_SIZE), 1)
        tril = (r_io >= c_io).astype(BF16)

        def body(r, run_eq):
            key = key_scr[r]
            eq = key == thr
            pre = _dot(tril, eq.astype(BF16))
            emit(r, (key > thr) | (eq & (run_eq + pre <= need)))
            return run_eq + pre[PAGE_SIZE - 1:PAGE_SIZE, :]
        run_eq = lax.fori_loop(0, n_pages, body, jnp.zeros((1, db), F32))
        sel_self = (key_self > thr) | ((key_self == thr) & (run_eq + 1.0 <= need))
        emit(n_pages, jnp.broadcast_to(sel_self, (PAGE_SIZE, db)))

    lax.cond(jnp.max(cnt_thr) > n_top, sel_ties, sel_plain)


def _dsa_s_select(scores, qi, ki_s, wi, *, n_top):
    n_pages, db, _ = scores.shape
    assert db == LANES and n_pages % 2 == 0
    hsum = np.zeros((N_IDX_HEADS * IDX_DIM, LANES), np.float32)
    hsum[np.arange(N_IDX_HEADS * IDX_DIM), np.arange(N_IDX_HEADS * IDX_DIM) // IDX_DIM] = 1.0
    return pl.pallas_call(
        functools.partial(_dsa_s_select_kernel, n_top=n_top),
        out_shape=jax.ShapeDtypeStruct((n_pages + 1, db, PAGE_SIZE), F32),
        scratch_shapes=[pltpu.VMEM((n_pages, PAGE_SIZE, db), I32)],
        compiler_params=pltpu.CompilerParams(vmem_limit_bytes=VMEM_LIMIT),
    )(scores, qi, ki_s, wi, jnp.asarray(hsum, BF16))


def _bias_table_s_kernel(rbt_ref, bkt_ref, o_ref):
    rbt = rbt_ref[...] * LOG2E
    for r in range(bkt_ref.shape[0]):
        bkt = bkt_ref[r:r + 1, :]
        out = jnp.zeros((N_HEADS, bkt.shape[1]), F32)
        for b in range(N_BUCKETS):
            out = jnp.where(bkt == b, rbt[:, b:b + 1], out)
        o_ref[r] = out


def _bias_table_s(rel_bias, n_pages):
    past = n_pages * PAGE_SIZE
    pos = np.arange((n_pages + 1) * PAGE_SIZE).reshape(n_pages + 1, PAGE_SIZE)
    pos[n_pages] = past
    return pl.pallas_call(
        _bias_table_s_kernel,
        out_shape=jax.ShapeDtypeStruct((n_pages + 1, N_HEADS, PAGE_SIZE), F32),
    )(rel_bias.T, jnp.asarray(_t5_bucket_np(past - pos)))


def _dsa_s_attend_kernel(pt_ref, mb_ref, bias_ref, q_ref, ks_ref, vs_ref, *refs, pg, n_pages):
    k_refs, v_refs = refs[:pg], refs[pg:2 * pg]
    o_ref, m_scr, l_scr, acc_scr = refs[2 * pg:]
    s = pl.program_id(1)
    nq = N_HEADS // N_KV_HEADS
    kvw = N_KV_HEADS * HEAD_DIM

    @pl.when(s == 0)
    def _():
        m_scr[...] = jnp.full(m_scr.shape, -1e30, F32)
        l_scr[...] = jnp.zeros_like(l_scr)
        acc_scr[...] = jnp.zeros_like(acc_scr)

    q = q_ref[0]
    h_io = lax.broadcasted_iota(I32, (N_HEADS, kvw), 0)
    c_io = lax.broadcasted_iota(I32, (N_HEADS, kvw), 1)
    band = (c_io // HEAD_DIM) == (h_io // nq)
    q_bd = jnp.where(band, jnp.concatenate([q] * N_KV_HEADS, axis=1), 0.0)
    logits = []
    for r in range(pg):
        page = s * pg + r
        logits.append(_dot_nt(q_bd, k_refs[r][0]) + bias_ref[page] + mb_ref[page, 0])
    m_old = m_scr[...]
    m_new = m_old
    for lg in logits:
        m_new = jnp.maximum(m_new, jnp.max(lg, axis=1, keepdims=True))
    alpha = jnp.exp2(m_old - m_new)
    l_new = alpha * l_scr[...]
    acc = alpha * acc_scr[...]
    for r, lg in enumerate(logits):
        p = jnp.exp2(lg - m_new)
        l_new = l_new + jnp.sum(p, axis=1, keepdims=True)
        acc = acc + _dot(p.astype(BF16), v_refs[r][0])
    m_scr[...] = m_new
    l_scr[...] = l_new
    acc_scr[...] = acc

    @pl.when(s == pl.num_programs(1) - 1)
    def _():
        ks = ks_ref[0].astype(BF16).astype(F32)
        lg = jnp.sum(q_bd.astype(F32) * ks, axis=1, keepdims=True)
        lg = lg + bias_ref[n_pages][:, 0:1] + mb_ref[n_pages, 0][:, 0:1]
        m_fin = jnp.maximum(m_new, lg)
        a = jnp.exp2(m_new - m_fin)
        p = jnp.exp2(lg - m_fin)
        l_fin = a * l_new + p
        out = (a * acc + p.astype(BF16).astype(F32) * vs_ref[0].astype(BF16).astype(F32)) / l_fin
        out = jnp.where(band, out, 0.0)
        res = out[:, 0:HEAD_DIM]
        for n in range(1, N_KV_HEADS):
            res = res + out[:, n * HEAD_DIM:(n + 1) * HEAD_DIM]
        o_ref[0] = res.astype(o_ref.dtype)


def _dsa_s_attend(page_table, mask, bias_tab, q, k_s, v_s, cache_k, cache_v, *, pg):
    db, n_pages = page_table.shape
    kvw = N_KV_HEADS * HEAD_DIM
    width = PAGE_SIZE
    per_seq = lambda *shape: pl.BlockSpec((1,) + shape, lambda b, s, pt: (b,) + (0,) * len(shape))
    page_spec = lambda r: pl.BlockSpec(
        (1, PAGE_SIZE, kvw), lambda b, s, pt: (pt[b * n_pages + s * pg + r], 0, 0))
    grid_spec = pltpu.PrefetchScalarGridSpec(
        num_scalar_prefetch=1,
        grid=(db, n_pages // pg),
        in_specs=[pl.BlockSpec((n_pages + 1, 1, 1, width), lambda b, s, pt: (0, b, 0, 0)),
                  pl.BlockSpec(bias_tab.shape, lambda b, s, pt: (0, 0, 0)),
                  per_seq(N_HEADS, HEAD_DIM), per_seq(1, kvw), per_seq(1, kvw)]
                 + [page_spec(r) for r in range(pg)] * 2,
        out_specs=per_seq(N_HEADS, HEAD_DIM),
        scratch_shapes=[pltpu.VMEM((N_HEADS, 1), F32), pltpu.VMEM((N_HEADS, 1), F32),
                        pltpu.VMEM((N_HEADS, kvw), F32)],
    )
    return pl.pallas_call(
        functools.partial(_dsa_s_attend_kernel, pg=pg, n_pages=n_pages),
        grid_spec=grid_spec,
        out_shape=jax.ShapeDtypeStruct((db, N_HEADS, HEAD_DIM), BF16),
        compiler_params=_params("arbitrary", "arbitrary"),
    )(page_table.reshape(-1), mask.reshape(n_pages + 1, db, 1, width), bias_tab, q, k_s, v_s,
      *([cache_k] * pg), *([cache_v] * pg))


def _ssd_kernel(xbc_ref, z_ref, dt_ref, cw_ref, cb_ref, alog_ref, dtb_ref, dsk_ref, ng_ref, e_ref, *refs,
                rows, has_init, d_inner):
    if has_init:
        conv0_ref, ssm0_ref = refs[:2]
        refs = refs[2:]
    y_ref, ssm_ref, conv_ref, ext_scr, ht_scr, y_scr = refs
    c = pl.program_id(1)
    cl = SSD_CHUNK
    gw = d_inner // SSD_GROUPS
    hpg = gw // SSD_HEAD_DIM
    n_heads = d_inner // SSD_HEAD_DIM
    conv_dim = d_inner + 2 * SSD_GROUPS * D_STATE
    conv_w = cw_ref.shape[0]
    top = SUBLANES
    row_io = lax.broadcasted_iota(I32, (cl, 1), 0)

    def padded(ref):
        a = ref[0]
        if rows == cl:
            return a
        return jnp.where(row_io < rows, jnp.broadcast_to(a, (cl, a.shape[1])), 0.0)

    @pl.when(c == 0)
    def _():
        ext_scr[0:top, :] = jnp.zeros((top, conv_dim), F32)
        if has_init:
            ext_scr[top - conv_w + 1:top, :] = conv0_ref[0]
            for g in range(SSD_GROUPS):
                ht_scr[g] = ssm0_ref[0, g * gw:(g + 1) * gw, :].T
        else:
            ht_scr[...] = jnp.zeros_like(ht_scr)

    ext_scr[top:top + cl, :] = padded(xbc_ref)
    conv_ref[0] = ext_scr[top + rows - conv_w + 1:top + rows, :]

    cblk = 512
    for cb in range(conv_dim // cblk):
        sl = slice(cb * cblk, (cb + 1) * cblk)
        acc = jnp.broadcast_to(cb_ref[:, sl], (cl, cblk))
        for w in range(conv_w):
            acc = acc + ext_scr[top - conv_w + 1 + w:top - conv_w + 1 + w + cl, sl] * cw_ref[w:w + 1, sl]
        y_scr[:, sl] = _silu(acc)
    ext_scr[top - conv_w + 1:top, :] = ext_scr[top + cl - conv_w + 1:top + cl, :]

    dt = jax.nn.softplus(padded(dt_ref) + dtb_ref[...])
    if rows < cl:
        dt = jnp.where(row_io < rows, dt, 0.0)
    a_neg2 = -jnp.exp(alog_ref[...]) * LOG2E
    r_io = lax.broadcasted_iota(I32, (cl, cl), 0)
    c_io = lax.broadcasted_iota(I32, (cl, cl), 1)
    tril = r_io >= c_io
    acs = _dot_exact_lhs(tril.astype(BF16), dt * a_neg2)
    acs_t = acs.T
    acs_last = acs[cl - 1:cl, :]
    stacked = jnp.concatenate([dt, jnp.exp2(acs), jnp.exp2(acs_last - acs)], axis=0)
    expanded = _dot_exact_rhs(stacked, e_ref[...])
    dt_x, ea_x, te_x = expanded[0:cl], expanded[cl:2 * cl], expanded[2 * cl:3 * cl]

    z = padded(z_ref)
    for g in range(SSD_GROUPS):
        gs = slice(g * gw, (g + 1) * gw)
        x_g = y_scr[:, gs]
        b_g = y_scr[:, d_inner + g * D_STATE:d_inner + (g + 1) * D_STATE]
        c_g = y_scr[:, d_inner + (SSD_GROUPS + g) * D_STATE:d_inner + (SSD_GROUPS + g + 1) * D_STATE]
        c16 = c_g.astype(BF16)
        cbm = _dot_nt(c16, b_g.astype(BF16))
        xdt = x_g * dt_x[:, gs]
        xdt16 = xdt.astype(BF16)
        ht = ht_scr[g]
        y_g = _dot(c16, ht.astype(BF16)) * ea_x[:, gs] + dsk_ref[:, gs] * x_g
        ht_scr[g] = ht * ea_x[cl - 1:cl, gs] + _dot(b_g.T.astype(BF16), (xdt * te_x[:, gs]).astype(BF16))
        diag = []
        for r in range(hpg):
            h = g * hpg + r
            seg = acs[:, h:h + 1] - acs_t[h:h + 1, :]
            m = (cbm * jnp.exp2(jnp.where(tril, seg, NEG_INF))).astype(BF16)
            diag.append(_dot(m, xdt16[:, r * SSD_HEAD_DIM:(r + 1) * SSD_HEAD_DIM]))
        y_g = (y_g + jnp.concatenate(diag, axis=1)) * _silu(z[:, gs])
        y_g = y_g * lax.rsqrt(jnp.mean(y_g * y_g, axis=1, keepdims=True) + EPS) * ng_ref[:, gs]
        y_ref[0, :, gs] = y_g[0:rows].astype(y_ref.dtype)

    @pl.when(c == pl.num_programs(1) - 1)
    def _():
        for g in range(SSD_GROUPS):
            ssm_ref[0, g * gw:(g + 1) * gw, :] = ht_scr[g].T


def _ssd(xbc, z, dt, conv_w, conv_b, a_log, dt_bias, d_skip, norm_g, *, n_seq, n_chunks, rows,
         conv0=None, ssm0=None):
    conv_dim = xbc.shape[-1]
    d_inner = z.shape[-1]
    n_heads = d_inner // SSD_HEAD_DIM
    cw = conv_w.shape[0]
    has_init = conv0 is not None
    pad = lambda a: jnp.pad(a, (0, LANES - a.shape[0])).reshape(1, LANES)
    expand = np.zeros((LANES, d_inner), np.float32)
    expand[np.arange(d_inner) // SSD_HEAD_DIM, np.arange(d_inner)] = 1.0
    step = lambda b, c: (b * n_chunks + c, 0, 0)
    seq = lambda b, c: (b, 0, 0)
    const = lambda b, c: (0, 0)
    in_specs = [pl.BlockSpec((1, rows, conv_dim), step), pl.BlockSpec((1, rows, d_inner), step),
                pl.BlockSpec((1, rows, LANES), step),
                pl.BlockSpec((cw, conv_dim), const), pl.BlockSpec((1, conv_dim), const),
                pl.BlockSpec((1, LANES), const), pl.BlockSpec((1, LANES), const),
                pl.BlockSpec((1, d_inner), const), pl.BlockSpec((1, d_inner), const),
                pl.BlockSpec((LANES, d_inner), const)]
    args = [xbc, z, dt, conv_w, conv_b.reshape(1, conv_dim), pad(a_log), pad(dt_bias),
            jnp.repeat(d_skip, SSD_HEAD_DIM).reshape(1, d_inner), norm_g.reshape(1, d_inner),
            jnp.asarray(expand, BF16)]
    if has_init:
        in_specs += [pl.BlockSpec((1, cw - 1, conv_dim), seq), pl.BlockSpec((1, d_inner, D_STATE), seq)]
        args += [conv0, ssm0]
    return pl.pallas_call(
        functools.partial(_ssd_kernel, rows=rows, has_init=has_init, d_inner=d_inner),
        grid=(n_seq, n_chunks),
        in_specs=in_specs,
        out_specs=[pl.BlockSpec((1, rows, d_inner), step), pl.BlockSpec((1, d_inner, D_STATE), seq),
                   pl.BlockSpec((1, cw - 1, conv_dim), seq)],
        out_shape=[jax.ShapeDtypeStruct((n_seq * n_chunks, rows, d_inner), BF16),
                   jax.ShapeDtypeStruct((n_seq, d_inner, D_STATE), F32),
                   jax.ShapeDtypeStruct((n_seq, cw - 1, conv_dim), F32)],
        scratch_shapes=[pltpu.VMEM((SUBLANES + SSD_CHUNK, conv_dim), F32),
                        pltpu.VMEM((SSD_GROUPS, D_STATE, d_inner // SSD_GROUPS), F32),
                        pltpu.VMEM((SSD_CHUNK, conv_dim), F32)],
        compiler_params=_params("arbitrary", "arbitrary"),
    )(*args)


def _merge_kernel(x_ref, gt_ref, att_ref, ssd_ref, ga_ref, gs_ref, wa_ref, ws_ref, wo_ref, o_ref):
    merged = (_sigmoid(ga_ref[...]) * _dot(att_ref[...], wa_ref[...])
              + _sigmoid(gs_ref[...]) * _dot(ssd_ref[...], ws_ref[...]))
    o_ref[...] = x_ref[...] + gt_ref[0] * _dot(merged.astype(BF16), wo_ref[...])


def _merge(x, gate, att, ssd_y, gate_a, gate_s, w_a, w_s, w_o, *, tm):
    m, d = x.shape
    row = lambda i: (i, 0)
    const = lambda i: (0, 0)
    return pl.pallas_call(
        _merge_kernel,
        grid=(m // tm,),
        in_specs=[pl.BlockSpec((tm, d), row), gate.spec(tm),
                  pl.BlockSpec((tm, att.shape[1]), row), pl.BlockSpec((tm, ssd_y.shape[1]), row),
                  pl.BlockSpec((tm, d), row), pl.BlockSpec((tm, d), row),
                  pl.BlockSpec(w_a.shape, const), pl.BlockSpec(w_s.shape, const), pl.BlockSpec(w_o.shape, const)],
        out_specs=pl.BlockSpec((tm, d), row),
        out_shape=jax.ShapeDtypeStruct((m, d), F32),
        compiler_params=_params("parallel"),
    )(x, gate.arr, att, ssd_y, gate_a, gate_s, w_a, w_s, w_o)


def _pad_cols(w, width):
    return jnp.pad(w, ((0, 0), (0, width - w.shape[1])))


def _trunk(x, mods, rows_per_seq, p, tm, attend, ssd_fn, final_g):
    sh1, sc1, g1, sh2, sc2, g2, sh3, sc3, g3 = [_Mod(a, rows_per_seq) for a in mods]
    x = _ffn(x, sh1, sc1, g1, p["norm_ffn1"], p["w_ffn1_in"], p["w_ffn1_out"], final_g, tm=tm, final_norm=False)
    q, k, v, qi, ki, wi = _proj(x, sh2, sc2, p["norm_mix"], p["w_att"], p["seg_att"], tm=tm)
    z, xbc, dt = _proj(x, sh2, sc2, p["norm_mix"], p["w_ssd"], p["seg_ssd"], tm=tm)
    gate_a, gate_s = _proj(x, sh2, sc2, p["norm_mix"], p["w_gate"], p["seg_gate"], tm=tm)
    att = attend(q, k, v, qi, ki, wi)
    ssd_y, ssm_new, conv_new = ssd_fn(z, xbc, dt)
    x = _merge(x, g2, att, ssd_y, gate_a, gate_s, p["w_attn_out"], p["w_ssd_out"], p["w_out"], tm=tm)
    y = _ffn(x, sh3, sc3, g3, p["norm_ffn2"], p["w_ffn2_in"], p["w_ffn2_out"], final_g, tm=tm, final_norm=True)
    return y, (k, v, ki, ssm_new, conv_new)


def kernel(x_prompt, x_sample, c_prompt, c_sample, cache_k, cache_v, cache_kidx, state_ssm, state_conv, page_table,
           w_ada, b_ada, norm_ffn1, w_ffn1_in, w_ffn1_out, norm_mix, w_in, rel_bias, conv_w, conv_b, a_log, dt_bias,
           d_skip, norm_ssd, w_attn_out, w_ssd_out, w_out, norm_ffn2, w_ffn2_in, w_ffn2_out, norm_final):
    depth = w_ada.shape[0]
    assert depth == 1
    batch, seq, d = x_prompt.shape
    db, dec_seq, _ = x_sample.shape
    assert dec_seq == 1
    n_pool = cache_k.shape[1]
    n_pages = page_table.shape[1]
    d_inner = norm_ssd.shape[1]
    conv_dim = conv_w.shape[2]
    n_ssd_heads = d_inner // SSD_HEAD_DIM
    att_q = N_HEADS * HEAD_DIM
    att_kv = N_KV_HEADS * HEAD_DIM
    idx_q = N_IDX_HEADS * IDX_DIM
    l = 0

    widths = (att_q, att_kv, att_kv, idx_q, IDX_DIM, N_IDX_HEADS, d_inner, conv_dim, n_ssd_heads, d, d)
    bounds = np.concatenate([[0], np.cumsum(widths)])
    assert bounds[-1] == w_in.shape[2]
    cols = [w_in[l][:, bounds[i]:bounds[i + 1]].astype(BF16) for i in range(len(widths))]
    w_q, w_k, w_v, w_qi, w_ki, w_wi, w_z, w_xbc, w_dt, w_ga, w_gs = cols
    row1 = lambda a: a.reshape(1, -1)
    p = {
        "norm_ffn1": row1(norm_ffn1[l]), "w_ffn1_in": w_ffn1_in[l].astype(BF16), "w_ffn1_out": w_ffn1_out[l].astype(BF16),
        "norm_mix": row1(norm_mix[l]),
        "w_att": jnp.concatenate([w_q, w_k, w_v, w_qi, _pad_cols(w_ki, LANES), _pad_cols(w_wi, LANES)], axis=1),
        "seg_att": [(att_q, BF16, HEAD_DIM ** -0.5 * LOG2E), (att_kv, F32, 1.0), (att_kv, F32, 1.0),
                    (idx_q, BF16, 1.0), (IDX_DIM, F32, 1.0), (LANES, F32, 1.0)],
        "w_ssd": jnp.concatenate([w_z, w_xbc, _pad_cols(w_dt, LANES)], axis=1),
        "seg_ssd": [(d_inner, F32, 1.0), (conv_dim, F32, 1.0), (LANES, F32, 1.0)],
        "w_gate": jnp.concatenate([w_ga, w_gs], axis=1),
        "seg_gate": [(d, F32, 1.0), (d, F32, 1.0)],
        "w_attn_out": w_attn_out[l].astype(BF16), "w_ssd_out": w_ssd_out[l].astype(BF16), "w_out": w_out[l].astype(BF16),
        "norm_ffn2": row1(norm_ffn2[l]), "w_ffn2_in": w_ffn2_in[l].astype(BF16), "w_ffn2_out": w_ffn2_out[l].astype(BF16),
    }
    final_g = row1(norm_final)
    ssd_args = (conv_w[l], conv_b[l], a_log[l], dt_bias[l], d_skip[l], norm_ssd[l])

    ada = _ada(jnp.concatenate([c_prompt, c_sample], axis=0), w_ada[l], b_ada[l])
    ada_p = [a.reshape(batch, 1, d) for a in jnp.split(ada[:batch], 9, axis=1)]
    ada_s = [a.reshape(1, db, d) for a in jnp.split(ada[batch:], 9, axis=1)]

    tq = 128
    n_chunks = seq // SSD_CHUNK
    bias_tab = _bias_table(rel_bias, tq, seq)

    def attend_p(q, k, v, qi, ki, wi):
        return _dsa_prompt(q, qi, wi, k, v, ki, bias_tab, batch=batch, seq=seq, tq=tq)

    def ssd_p(z, xbc, dt):
        r3 = lambda a: a.reshape(batch * n_chunks, SSD_CHUNK, a.shape[-1])
        y, ssm, conv = _ssd(r3(xbc), r3(z), r3(dt), *ssd_args, n_seq=batch, n_chunks=n_chunks, rows=SSD_CHUNK)
        return y.reshape(batch * seq, d_inner), ssm, conv

    yp, (k_p, v_p, ki_p, ssm_p, conv_p) = _trunk(x_prompt.reshape(batch * seq, d), ada_p, seq, p, 512,
                                                 attend_p, ssd_p, final_g)

    pg = min(16, n_pages)
    n_top_s = min(TOPK_MAX, (n_pages * PAGE_SIZE + 1) // 4)
    bias_tab_s = _bias_table_s(rel_bias, n_pages)

    def attend_s(q, k, v, qi, ki, wi):
        wi_bc = jnp.broadcast_to(wi[:, :N_IDX_HEADS, None], (db, N_IDX_HEADS, LANES))
        scores = _dsa_s_scores(page_table, qi.reshape(db, N_IDX_HEADS, IDX_DIM), wi_bc,
                               cache_kidx[l].astype(BF16), pg=min(2 * pg, n_pages))
        mask = _dsa_s_select(scores.reshape(n_pages, db, PAGE_SIZE), qi, ki, wi, n_top=n_top_s)
        att = _dsa_s_attend(page_table, mask, bias_tab_s, q.reshape(db, N_HEADS, HEAD_DIM),
                            k.reshape(db, 1, att_kv), v.reshape(db, 1, att_kv),
                            cache_k[l].reshape(n_pool, PAGE_SIZE, att_kv).astype(BF16),
                            cache_v[l].reshape(n_pool, PAGE_SIZE, att_kv).astype(BF16), pg=pg)
        return att.reshape(db, att_q)

    def ssd_s(z, xbc, dt):
        r3 = lambda a: a.reshape(db, 1, a.shape[-1])
        y, ssm, conv = _ssd(r3(xbc), r3(z), r3(dt), *ssd_args, n_seq=db, n_chunks=1, rows=1,
                            conv0=state_conv[l], ssm0=state_ssm[l].reshape(db, d_inner, D_STATE))
        return y.reshape(db, d_inner), ssm, conv

    ys, (k_s, v_s, ki_s, ssm_s, conv_s) = _trunk(x_sample.reshape(db, d), ada_s, db, p, db,
                                                 attend_s, ssd_s, final_g)

    st = lambda a, *shape: a.reshape((1,) + shape)
    return (yp.reshape(batch, seq, d), ys.reshape(db, 1, d),
            st(k_p, batch, seq, N_KV_HEADS, HEAD_DIM), st(v_p, batch, seq, N_KV_HEADS, HEAD_DIM),
            st(ki_p, batch, seq, IDX_DIM),
            st(ssm_p, batch, n_ssd_heads, SSD_HEAD_DIM, D_STATE), st(conv_p, batch, conv_w.shape[1] - 1, conv_dim),
            st(k_s, db, 1, N_KV_HEADS, HEAD_DIM), st(v_s, db, 1, N_KV_HEADS, HEAD_DIM), st(ki_s, db, 1, IDX_DIM),
            st(ssm_s, db, n_ssd_heads, SSD_HEAD_DIM, D_STATE), st(conv_s, db, conv_w.shape[1] - 1, conv_dim))
```

```python
import functools
import math

import numpy as np
import jax
import jax.numpy as jnp
from jax import lax
from jax.experimental import pallas as pl
from jax.experimental.pallas import tpu as pltpu

N_HEADS = 16
HEAD_DIM = 64
N_KV_HEADS = 4
N_IDX_HEADS = 8
IDX_DIM = 64
TOPK_MAX = 256
N_BUCKETS = 32
MAX_DISTANCE = 128
SSD_HEAD_DIM = 64
SSD_GROUPS = 4
D_STATE = 128
SSD_CHUNK = 128
EPS = 1e-6
PAGE_SIZE = 128

LANES = 128
SUBLANES = 8
VMEM_LIMIT = 56 * 1024 * 1024

F32 = jnp.float32
BF16 = jnp.bfloat16
I32 = jnp.int32
NEG_INF = float("-inf")
INT_MIN = -(2 ** 31)
LOG2E = math.log2(math.e)

_NT = (((1,), (1,)), ((), ()))


def _dot(a, b):
    return jnp.dot(a, b, preferred_element_type=F32)


def _dot_nt(a, b):
    return lax.dot_general(a, b, _NT, preferred_element_type=F32)


def _split3(a):
    hi = a.astype(BF16)
    r = a - hi.astype(F32)
    mid = r.astype(BF16)
    lo = (r - mid.astype(F32)).astype(BF16)
    return hi, mid, lo


def _dot_exact_rhs(a, b_bf16):
    hi, mid, lo = _split3(a)
    return _dot(hi, b_bf16) + _dot(mid, b_bf16) + _dot(lo, b_bf16)


def _dot_exact_lhs(a_bf16, b):
    hi, mid, lo = _split3(b)
    return _dot(a_bf16, hi) + _dot(a_bf16, mid) + _dot(a_bf16, lo)


def _rmsnorm(x, g):
    return (x * lax.rsqrt(jnp.mean(x * x, axis=-1, keepdims=True) + EPS)) * g


def _sigmoid(x):
    return 0.5 * jnp.tanh(0.5 * x) + 0.5


def _silu(x):
    h = 0.5 * x
    return h * jnp.tanh(h) + h


def _params(*sem):
    return pltpu.CompilerParams(dimension_semantics=sem, vmem_limit_bytes=VMEM_LIMIT)


def _t5_bucket_np(dist):
    n = np.maximum(dist, 0)
    max_exact = N_BUCKETS // 2
    nf = np.maximum(n, 1).astype(np.float32)
    val = (np.log(nf / np.float32(max_exact)) / np.float32(math.log(MAX_DISTANCE / max_exact))
           * np.float32(N_BUCKETS - max_exact)).astype(np.float32)
    frac = np.abs(val - np.round(val))
    knife = (frac < 1e-3) & (n > max_exact) & (val < N_BUCKETS - max_exact - 0.5)
    assert not knife.any()
    large = np.minimum(max_exact + val.astype(np.int32), N_BUCKETS - 1)
    return np.where(n < max_exact, n, large).astype(np.int32)


def _ada_kernel(c_ref, w_ref, b_ref, o_ref):
    h = _silu(c_ref[...]).astype(BF16)
    o_ref[...] = _dot(h, w_ref[...].astype(BF16)) + b_ref[...]


def _ada(c, w, b):
    rows, d = c.shape
    n = w.shape[1]
    tn = 1024
    return pl.pallas_call(
        _ada_kernel,
        grid=(n // tn,),
        in_specs=[pl.BlockSpec((rows, d), lambda j: (0, 0)),
                  pl.BlockSpec((d, tn), lambda j: (0, j)),
                  pl.BlockSpec((1, tn), lambda j: (0, j))],
        out_specs=pl.BlockSpec((rows, tn), lambda j: (0, j)),
        out_shape=jax.ShapeDtypeStruct((rows, n), F32),
        compiler_params=_params("arbitrary"),
    )(c, w, b.reshape(1, n))


class _Mod:
    def __init__(self, arr, rows_per_seq):
        self.arr = arr
        self.rows_per_seq = rows_per_seq

    def spec(self, tm):
        r = self.arr.shape[1]
        d = self.arr.shape[2]
        if r == 1:
            per = self.rows_per_seq // tm
            return pl.BlockSpec((1, 1, d), lambda i, *_: (i // per, 0, 0))
        assert r == tm
        return pl.BlockSpec((1, r, d), lambda i, *_: (i, 0, 0))


def _ffn_kernel(x_ref, sh_ref, sc_ref, gt_ref, g_ref, wg_ref, wu_ref, wo_ref, fg_ref, o_ref, h_scr, acc_scr,
                *, final_norm):
    j = pl.program_id(1)

    @pl.when(j == 0)
    def _():
        h = _rmsnorm(x_ref[...], g_ref[...]) * (1.0 + sc_ref[0]) + sh_ref[0]
        h_scr[...] = h.astype(BF16)
        acc_scr[...] = jnp.zeros_like(acc_scr)

    h = h_scr[...]
    gate = _dot(h, wg_ref[...])
    up = _dot(h, wu_ref[...])
    act = (_silu(gate) * up).astype(BF16)
    acc_scr[...] += _dot(act, wo_ref[...])

    @pl.when(j == pl.num_programs(1) - 1)
    def _():
        out = x_ref[...] + 0.5 * gt_ref[0] * acc_scr[...]
        if final_norm:
            out = _rmsnorm(out, fg_ref[...])
        o_ref[...] = out


def _ffn(x, shift, scale, gate, norm_g, w_in, w_out, final_g, *, tm, final_norm):
    m, d = x.shape
    f = w_out.shape[0]
    nj = 2
    tf = f // nj
    assert tf % LANES == 0 and m % tm == 0
    row = lambda i, j: (i, 0)
    const = lambda i, j: (0, 0)
    return pl.pallas_call(
        functools.partial(_ffn_kernel, final_norm=final_norm),
        grid=(m // tm, nj),
        in_specs=[pl.BlockSpec((tm, d), row),
                  shift.spec(tm), scale.spec(tm), gate.spec(tm),
                  pl.BlockSpec((1, d), const),
                  pl.BlockSpec((d, tf), lambda i, j: (0, j)),
                  pl.BlockSpec((d, tf), lambda i, j: (0, j + nj)),
                  pl.BlockSpec((tf, d), lambda i, j: (j, 0)),
                  pl.BlockSpec((1, d), const)],
        out_specs=pl.BlockSpec((tm, d), row),
        out_shape=jax.ShapeDtypeStruct((m, d), F32),
        scratch_shapes=[pltpu.VMEM((tm, d), BF16), pltpu.VMEM((tm, d), F32)],
        compiler_params=_params("parallel", "arbitrary"),
    )(x, shift.arr, scale.arr, gate.arr, norm_g, w_in, w_in, w_out, final_g)


def _proj_kernel(x_ref, sh_ref, sc_ref, g_ref, w_ref, *o_refs, offsets, scales):
    h = (_rmsnorm(x_ref[...], g_ref[...]) * (1.0 + sc_ref[0]) + sh_ref[0]).astype(BF16)
    for o_ref, off, scale in zip(o_refs, offsets, scales):
        width = o_ref.shape[1]
        out = _dot(h, w_ref[:, off:off + width])
        if scale != 1.0:
            out = out * scale
        o_ref[...] = out.astype(o_ref.dtype)


def _proj(x, shift, scale, norm_g, w, segments, *, tm):
    m, d = x.shape
    offsets, off = [], 0
    for width, _, _ in segments:
        offsets.append(off)
        off += -(-width // LANES) * LANES
    assert off == w.shape[1]
    row = lambda i: (i, 0)
    const = lambda i: (0, 0)
    return pl.pallas_call(
        functools.partial(_proj_kernel, offsets=tuple(offsets), scales=tuple(s for _, _, s in segments)),
        grid=(m // tm,),
        in_specs=[pl.BlockSpec((tm, d), row), shift.spec(tm), scale.spec(tm),
                  pl.BlockSpec((1, d), const), pl.BlockSpec(w.shape, const)],
        out_specs=[pl.BlockSpec((tm, width), row) for width, _, _ in segments],
        out_shape=[jax.ShapeDtypeStruct((m, width), dt) for width, dt, _ in segments],
        compiler_params=_params("parallel"),
    )(x, shift.arr, scale.arr, norm_g, w)


def _sort_key(score):
    score = jnp.where(score == 0.0, 0.0, score)
    bits = pltpu.bitcast(score, I32)
    return jnp.where(bits >= 0, bits, bits ^ jnp.int32(0x7FFFFFFF))


def _radix_threshold(count_ge, n_top, shape):
    def bit_body(t, carry):
        thr_u, cnt_thr = carry
        cand_u = thr_u | jnp.left_shift(jnp.int32(1), 31 - t)
        cnt = count_ge(cand_u ^ jnp.int32(INT_MIN))
        take = cnt >= n_top
        return jnp.where(take, cand_u, thr_u), jnp.where(take, cnt, cnt_thr)

    thr_u, cnt_thr = lax.fori_loop(0, 32, bit_body, (jnp.zeros(shape, I32), jnp.zeros(shape, I32)))
    return thr_u ^ jnp.int32(INT_MIN), cnt_thr


def _bias_table_kernel(rb_ref, bkt_ref, o_ref):
    far = N_BUCKETS - 1
    for slot in range(bkt_ref.shape[0]):
        bkt = bkt_ref[slot]
        for h in range(N_HEADS):
            acc = jnp.zeros(bkt.shape, F32)
            for b in range(N_BUCKETS - 1):
                acc = jnp.where(bkt == b, (rb_ref[b, h] - rb_ref[far, h]) * LOG2E, acc)
            o_ref[slot, h] = acc


def _bias_table(rel_bias, tq, seq):
    s = np.arange(tq)[:, None]
    t = np.arange(tq)[None, :]
    bkt = np.stack([_t5_bucket_np(t + tq - s), _t5_bucket_np(t - s)])
    assert (_t5_bucket_np(np.arange(tq + 1, seq + 1)) == N_BUCKETS - 1).all()
    return pl.pallas_call(
        _bias_table_kernel,
        in_specs=[pl.BlockSpec(memory_space=pltpu.SMEM), pl.BlockSpec(memory_space=pltpu.VMEM)],
        out_specs=pl.BlockSpec(memory_space=pltpu.VMEM),
        out_shape=jax.ShapeDtypeStruct((2, N_HEADS, tq, tq), F32),
    )(rel_bias, jnp.asarray(bkt))


_V_ROWS = HEAD_DIM + 16


def _dsa_prompt_kernel(q_ref, qi_ref, wi_ref, k_ref, v_ref, ki_ref, bias_ref, o_ref,
                       kb_scr, vt_scr, kib_scr, key_scr, mb_scr, m_scr, al_scr, lg_scr, acc_scr, *, n_top, tq):
    i = pl.program_id(1)
    nq = N_HEADS // N_KV_HEADS
    nblk = key_scr.shape[0]
    s_io = lax.broadcasted_iota(I32, (tq, tq), 0)
    t_io = lax.broadcasted_iota(I32, (tq, tq), 1)
    causal = s_io <= t_io

    @pl.when(i == 0)
    def _():
        kb_scr[...] = k_ref[...].astype(BF16)
        kib_scr[...] = ki_ref[...].astype(BF16)
        for c in range(nblk):
            for pair in range(N_KV_HEADS // 2):
                vt = v_ref[c * tq:(c + 1) * tq, pair * 2 * HEAD_DIM:(pair + 1) * 2 * HEAD_DIM].T.astype(BF16)
                vt_scr[c, 2 * pair, 0:HEAD_DIM, :] = vt[0:HEAD_DIM]
                vt_scr[c, 2 * pair + 1, 0:HEAD_DIM, :] = vt[HEAD_DIM:2 * HEAD_DIM]
            for n in range(N_KV_HEADS):
                vt_scr[c, n, HEAD_DIM:_V_ROWS, :] = jnp.ones((_V_ROWS - HEAD_DIM, tq), BF16)

    def chunk(j):
        return pl.ds(pl.multiple_of(j * tq, tq), tq)

    def fold(a):
        return jnp.sum(a.reshape(tq // SUBLANES, SUBLANES, tq), axis=0)

    qi = qi_ref[...]
    qi_st = jnp.concatenate([qi[:, h * IDX_DIM:(h + 1) * IDX_DIM] for h in range(N_IDX_HEADS)], axis=0)
    wi_t = wi_ref[...].T
    wi_row = jnp.concatenate([wi_t[h:h + 1, :] for h in range(N_IDX_HEADS)], axis=1)
    idx_scale = (N_IDX_HEADS * IDX_DIM) ** -0.5

    def visible(j):
        return (j < i) | ((j == i) & causal)

    n_pairs = (i + 2) // 2

    def score_body(jp, carry):
        for j in (2 * jp, 2 * jp + 1):
            d = jnp.maximum(_dot_nt(kib_scr[chunk(j), :], qi_st), 0.0) * wi_row
            s = d[:, 0:tq]
            for h in range(1, N_IDX_HEADS):
                s = s + d[:, h * tq:(h + 1) * tq]
            key_scr[j] = _sort_key(jnp.where(visible(j), s * idx_scale, NEG_INF))
        return carry

    lax.fori_loop(0, n_pairs, score_body, 0)

    def count(pred):
        def body(jp, c):
            return c + fold(pred(key_scr[2 * jp]).astype(I32)) + fold(pred(key_scr[2 * jp + 1]).astype(I32))
        c = lax.fori_loop(0, n_pairs, body, jnp.zeros((SUBLANES, tq), I32))
        return jnp.sum(c, axis=0, keepdims=True)

    thr, cnt_thr = _radix_threshold(lambda cand: count(lambda key: key >= cand), n_top, (1, tq))

    def sel_plain():
        def body(j, carry):
            mb_scr[j] = jnp.where((key_scr[j] >= thr) & visible(j), 0.0, NEG_INF)
            return carry
        lax.fori_loop(0, i + 1, body, 0)

    def sel_ties():
        need = (n_top - count(lambda key: key > thr)).astype(F32)
        tril = (s_io >= t_io).astype(BF16)

        def body(j, run_eq):
            key = key_scr[j]
            eq = key == thr
            pre = _dot(tril, eq.astype(BF16))
            sel = ((key > thr) | (eq & (run_eq + pre <= need))) & visible(j)
            mb_scr[j] = jnp.where(sel, 0.0, NEG_INF)
            return run_eq + pre[tq - 1:tq, :]
        lax.fori_loop(0, i + 1, body, jnp.zeros((1, tq), F32))

    lax.cond(jnp.max(cnt_thr) > n_top, sel_ties, sel_plain)

    q = q_ref[...]
    q_st = [jnp.concatenate([q[:, (n * nq + g) * HEAD_DIM:(n * nq + g + 1) * HEAD_DIM] for g in range(nq)], axis=0)
            for n in range(N_KV_HEADS)]
    m_scr[...] = jnp.full(m_scr.shape, -1e30, F32)
    acc_scr[...] = jnp.zeros_like(acc_scr)

    heads = range(N_KV_HEADS)

    def stage_a(j, slot):
        mb4 = jnp.concatenate([mb_scr[j]] * nq, axis=1)
        lgs = [_dot_nt(kb_scr[chunk(j), n * HEAD_DIM:(n + 1) * HEAD_DIM], q_st[n]) + mb4 for n in heads]
        if slot is not None:
            lgs = [lgs[n] + jnp.concatenate([bias_ref[slot, n * nq + g] for g in range(nq)], axis=1) for n in heads]
        for n in heads:
            m_old = m_scr[n]
            m_new = jnp.maximum(m_old, jnp.max(lgs[n], axis=0, keepdims=True))
            al_scr[n] = jnp.exp2(m_old - m_new)
            m_scr[n] = m_new
            lg_scr[j % 2, n] = lgs[n]

    def stage_b(j):
        ps = [jnp.exp2(lg_scr[j % 2, n] - m_scr[n]).astype(BF16) for n in heads]
        pvs = [_dot(vt_scr[j, n], ps[n]) for n in heads]
        for n in heads:
            acc_scr[n] = al_scr[n] * acc_scr[n] + pvs[n]

    @pl.when(i >= 2)
    def _():
        stage_a(0, None)

        def far_body(j, carry):
            stage_b(j - 1)
            stage_a(j, None)
            return carry

        lax.fori_loop(1, i - 1, far_body, 0)
        stage_b(i - 2)
        stage_a(i - 1, 0)

    @pl.when(i == 1)
    def _():
        stage_a(0, 0)

    @pl.when(i >= 1)
    def _():
        stage_b(i - 1)
        stage_a(i, 1)

    @pl.when(i == 0)
    def _():
        stage_a(0, 1)

    stage_b(i)

    for n in range(N_KV_HEADS):
        acc = acc_scr[n]
        out_t = acc[0:HEAD_DIM] / acc[HEAD_DIM:HEAD_DIM + 1]
        for pair in range(nq // 2):
            g0 = 2 * pair
            two = jnp.concatenate([out_t[:, g0 * tq:(g0 + 1) * tq], out_t[:, (g0 + 1) * tq:(g0 + 2) * tq]], axis=0)
            h0 = n * nq + g0
            o_ref[:, h0 * HEAD_DIM:(h0 + 2) * HEAD_DIM] = two.T.astype(o_ref.dtype)


def _dsa_prompt(q, qi, wi, k, v, ki, bias_tab, *, batch, seq, tq):
    nblk = seq // tq
    n_top = min(TOPK_MAX, seq // 4)
    nq = N_HEADS // N_KV_HEADS
    assert tq == LANES and nq % 2 == 0 and N_KV_HEADS % 2 == 0
    blk = lambda b, i: (b * nblk + i, 0)
    whole = lambda b, i: (b, 0)
    kvw = N_KV_HEADS * HEAD_DIM
    return pl.pallas_call(
        functools.partial(_dsa_prompt_kernel, n_top=n_top, tq=tq),
        grid=(batch, nblk),
        in_specs=[pl.BlockSpec((tq, N_HEADS * HEAD_DIM), blk),
                  pl.BlockSpec((tq, N_IDX_HEADS * IDX_DIM), blk),
                  pl.BlockSpec((tq, LANES), blk),
                  pl.BlockSpec((seq, kvw), whole),
                  pl.BlockSpec((seq, kvw), whole),
                  pl.BlockSpec((seq, IDX_DIM), whole),
                  pl.BlockSpec(bias_tab.shape, lambda b, i: (0, 0, 0, 0))],
        out_specs=pl.BlockSpec((tq, N_HEADS * HEAD_DIM), blk),
        out_shape=jax.ShapeDtypeStruct((batch * seq, N_HEADS * HEAD_DIM), BF16),
        scratch_shapes=[pltpu.VMEM((seq, kvw), BF16),
                        pltpu.VMEM((nblk, N_KV_HEADS, _V_ROWS, tq), BF16),
                        pltpu.VMEM((seq, IDX_DIM), BF16),
                        pltpu.VMEM((nblk, tq, tq), I32), pltpu.VMEM((nblk, tq, tq), F32),
                        pltpu.VMEM((N_KV_HEADS, 1, nq * tq), F32), pltpu.VMEM((N_KV_HEADS, 1, nq * tq), F32),
                        pltpu.VMEM((2, N_KV_HEADS, tq, nq * tq), F32),
                        pltpu.VMEM((N_KV_HEADS, _V_ROWS, nq * tq), F32)],
        compiler_params=_params("arbitrary", "arbitrary"),
    )(q, qi, wi, k, v, ki, bias_tab)


def _dsa_s_score_kernel(pt_ref, qi_ref, wi_ref, *refs, pg):
    page_refs, o_ref = refs[:pg], refs[pg]
    qi = qi_ref[0].astype(BF16)
    wi = wi_ref[0]
    idx_scale = (N_IDX_HEADS * IDX_DIM) ** -0.5
    for r in range(pg):
        d = _dot_nt(qi, page_refs[r][0])
        o_ref[r, 0] = jnp.sum(jnp.maximum(d, 0.0) * wi, axis=0, keepdims=True) * idx_scale


def _dsa_s_scores(page_table, qi, wi_bc, cache_kidx, *, pg):
    db, n_pages = page_table.shape
    page_spec = lambda r: pl.BlockSpec(
        (1, PAGE_SIZE, IDX_DIM), lambda b, s, pt: (pt[b * n_pages + s * pg + r], 0, 0))
    grid_spec = pltpu.PrefetchScalarGridSpec(
        num_scalar_prefetch=1,
        grid=(db, n_pages // pg),
        in_specs=[pl.BlockSpec((1, N_IDX_HEADS, IDX_DIM), lambda b, s, pt: (b, 0, 0)),
                  pl.BlockSpec((1, N_IDX_HEADS, LANES), lambda b, s, pt: (b, 0, 0))]
                 + [page_spec(r) for r in range(pg)],
        out_specs=pl.BlockSpec((pg, 1, 1, PAGE_SIZE), lambda b, s, pt: (s, b, 0, 0)),
    )
    return pl.pallas_call(
        functools.partial(_dsa_s_score_kernel, pg=pg),
        grid_spec=grid_spec,
        out_shape=jax.ShapeDtypeStruct((n_pages, db, 1, PAGE_SIZE), F32),
        compiler_params=_params("arbitrary", "arbitrary"),
    )(page_table.reshape(-1), qi, wi_bc, *([cache_kidx] * pg))


def _dsa_s_select_kernel(sc_ref, qi_ref, kis_ref, wi_ref, hsum_ref, o_ref, key_scr, *, n_top):
    n_pages, db, _ = sc_ref.shape
    idx_scale = (N_IDX_HEADS * IDX_DIM) ** -0.5

    prod = (qi_ref[...].astype(BF16).astype(F32)
            * jnp.concatenate([kis_ref[...].astype(BF16).astype(F32)] * N_IDX_HEADS, axis=1))
    d_self = _dot_exact_rhs(prod, hsum_ref[...])
    s_self = jnp.sum((jnp.maximum(d_self, 0.0) * wi_ref[...]).T, axis=0, keepdims=True) * idx_scale
    key_self = _sort_key(s_self)

    def to_keys(r, carry):
        key_scr[r] = _sort_key(sc_ref[r].T)
        return carry
    lax.fori_loop(0, n_pages, to_keys, 0)

    def fold(a):
        return jnp.sum(a.reshape(PAGE_SIZE // SUBLANES, SUBLANES, db), axis=0)

    def count(pred):
        def body(rp, c):
            return c + fold(pred(key_scr[2 * rp]).astype(I32)) + fold(pred(key_scr[2 * rp + 1]).astype(I32))
        c = lax.fori_loop(0, n_pages // 2, body, jnp.zeros((SUBLANES, db), I32))
        return jnp.sum(c, axis=0, keepdims=True) + pred(key_self).astype(I32)

    thr, cnt_thr = _radix_threshold(lambda cand: count(lambda key: key >= cand), n_top, (1, db))

    def emit(r, sel):
        o_ref[r] = jnp.where(sel, 0.0, NEG_INF).T

    def sel_plain():
        def body(r, carry):
            emit(r, key_scr[r] >= thr)
            return carry
        lax.fori_loop(0, n_pages, body, 0)
        emit(n_pages, jnp.broadcast_to(key_self >= thr, (PAGE_SIZE, db)))

    def sel_ties():
        need = (n_top - count(lambda key: key > thr)).astype(F32)
        r_io = lax.broadcasted_iota(I32, (PAGE_SIZE, PAGE_SIZE), 0)
        c_io = lax.broadcasted_iota(I32, (PAGE_SIZE, PAGE_SIZE), 1)
        tril = (r_io >= c_io).astype(BF16)

        def body(r, run_eq):
            key = key_scr[r]
            eq = key == thr
            pre = _dot(tril, eq.astype(BF16))
            emit(r, (key > thr) | (eq & (run_eq + pre <= need)))
            return run_eq + pre[PAGE_SIZE - 1:PAGE_SIZE, :]
        run_eq = lax.fori_loop(0, n_pages, body, jnp.zeros((1, db), F32))
        sel_self = (key_self > thr) | ((key_self == thr) & (run_eq + 1.0 <= need))
        emit(n_pages, jnp.broadcast_to(sel_self, (PAGE_SIZE, db)))

    lax.cond(jnp.max(cnt_thr) > n_top, sel_ties, sel_plain)


def _dsa_s_select(scores, qi, ki_s, wi, *, n_top):
    n_pages, db, _ = scores.shape
    assert db == LANES and n_pages % 2 == 0
    hsum = np.zeros((N_IDX_HEADS * IDX_DIM, LANES), np.float32)
    hsum[np.arange(N_IDX_HEADS * IDX_DIM), np.arange(N_IDX_HEADS * IDX_DIM) // IDX_DIM] = 1.0
    return pl.pallas_call(
        functools.partial(_dsa_s_select_kernel, n_top=n_top),
        out_shape=jax.ShapeDtypeStruct((n_pages + 1, db, PAGE_SIZE), F32),
        scratch_shapes=[pltpu.VMEM((n_pages, PAGE_SIZE, db), I32)],
        compiler_params=pltpu.CompilerParams(vmem_limit_bytes=VMEM_LIMIT),
    )(scores, qi, ki_s, wi, jnp.asarray(hsum, BF16))


def _bias_table_s_kernel(rbt_ref, bkt_ref, o_ref):
    rbt = rbt_ref[...] * LOG2E
    for r in range(bkt_ref.shape[0]):
        bkt = bkt_ref[r:r + 1, :]
        out = jnp.zeros((N_HEADS, bkt.shape[1]), F32)
        for b in range(N_BUCKETS):
            out = jnp.where(bkt == b, rbt[:, b:b + 1], out)
        o_ref[r] = out


def _bias_table_s(rel_bias, n_pages):
    past = n_pages * PAGE_SIZE
    pos = np.arange((n_pages + 1) * PAGE_SIZE).reshape(n_pages + 1, PAGE_SIZE)
    pos[n_pages] = past
    return pl.pallas_call(
        _bias_table_s_kernel,
        out_shape=jax.ShapeDtypeStruct((n_pages + 1, N_HEADS, PAGE_SIZE), F32),
    )(rel_bias.T, jnp.asarray(_t5_bucket_np(past - pos)))


def _dsa_s_attend_kernel(pt_ref, mb_ref, bias_ref, q_ref, ks_ref, vs_ref, *refs, pg, n_pages):
    k_refs, v_refs = refs[:pg], refs[pg:2 * pg]
    o_ref, m_scr, l_scr, acc_scr = refs[2 * pg:]
    s = pl.program_id(1)
    nq = N_HEADS // N_KV_HEADS
    kvw = N_KV_HEADS * HEAD_DIM

    @pl.when(s == 0)
    def _():
        m_scr[...] = jnp.full(m_scr.shape, -1e30, F32)
        l_scr[...] = jnp.zeros_like(l_scr)
        acc_scr[...] = jnp.zeros_like(acc_scr)

    q = q_ref[0]
    h_io = lax.broadcasted_iota(I32, (N_HEADS, kvw), 0)
    c_io = lax.broadcasted_iota(I32, (N_HEADS, kvw), 1)
    band = (c_io // HEAD_DIM) == (h_io // nq)
    q_bd = jnp.where(band, jnp.concatenate([q] * N_KV_HEADS, axis=1), 0.0)
    logits = []
    for r in range(pg):
        page = s * pg + r
        logits.append(_dot_nt(q_bd, k_refs[r][0].astype(BF16)) + bias_ref[page] + mb_ref[page, 0])
    m_old = m_scr[...]
    m_new = m_old
    for lg in logits:
        m_new = jnp.maximum(m_new, jnp.max(lg, axis=1, keepdims=True))
    alpha = jnp.exp2(m_old - m_new)
    l_new = alpha * l_scr[...]
    acc = alpha * acc_scr[...]
    for r, lg in enumerate(logits):
        p = jnp.exp2(lg - m_new)
        l_new = l_new + jnp.sum(p, axis=1, keepdims=True)
        acc = acc + _dot(p.astype(BF16), v_refs[r][0].astype(BF16))
    m_scr[...] = m_new
    l_scr[...] = l_new
    acc_scr[...] = acc

    @pl.when(s == pl.num_programs(1) - 1)
    def _():
        ks = ks_ref[0].astype(BF16).astype(F32)
        lg = jnp.sum(q_bd.astype(F32) * ks, axis=1, keepdims=True)
        lg = lg + bias_ref[n_pages][:, 0:1] + mb_ref[n_pages, 0][:, 0:1]
        m_fin = jnp.maximum(m_new, lg)
        a = jnp.exp2(m_new - m_fin)
        p = jnp.exp2(lg - m_fin)
        l_fin = a * l_new + p
        out = (a * acc + p.astype(BF16).astype(F32) * vs_ref[0].astype(BF16).astype(F32)) / l_fin
        out = jnp.where(band, out, 0.0)
        res = out[:, 0:HEAD_DIM]
        for n in range(1, N_KV_HEADS):
            res = res + out[:, n * HEAD_DIM:(n + 1) * HEAD_DIM]
        o_ref[0] = res.astype(o_ref.dtype)


def _dsa_s_attend(page_table, mask, bias_tab, q, k_s, v_s, cache_k, cache_v, *, pg):
    db, n_pages = page_table.shape
    kvw = N_KV_HEADS * HEAD_DIM
    width = PAGE_SIZE
    per_seq = lambda *shape: pl.BlockSpec((1,) + shape, lambda b, s, pt: (b,) + (0,) * len(shape))
    page_spec = lambda r: pl.BlockSpec(
        (1, PAGE_SIZE, kvw), lambda b, s, pt: (pt[b * n_pages + s * pg + r], 0, 0))
    grid_spec = pltpu.PrefetchScalarGridSpec(
        num_scalar_prefetch=1,
        grid=(db, n_pages // pg),
        in_specs=[pl.BlockSpec((n_pages + 1, 1, 1, width), lambda b, s, pt: (0, b, 0, 0)),
                  pl.BlockSpec(bias_tab.shape, lambda b, s, pt: (0, 0, 0)),
                  per_seq(N_HEADS, HEAD_DIM), per_seq(1, kvw), per_seq(1, kvw)]
                 + [page_spec(r) for r in range(pg)] * 2,
        out_specs=per_seq(N_HEADS, HEAD_DIM),
        scratch_shapes=[pltpu.VMEM((N_HEADS, 1), F32), pltpu.VMEM((N_HEADS, 1), F32),
                        pltpu.VMEM((N_HEADS, kvw), F32)],
    )
    return pl.pallas_call(
        functools.partial(_dsa_s_attend_kernel, pg=pg, n_pages=n_pages),
        grid_spec=grid_spec,
        out_shape=jax.ShapeDtypeStruct((db, N_HEADS, HEAD_DIM), BF16),
        compiler_params=_params("arbitrary", "arbitrary"),
    )(page_table.reshape(-1), mask.reshape(n_pages + 1, db, 1, width), bias_tab, q, k_s, v_s,
      *([cache_k] * pg), *([cache_v] * pg))


def _ssd_kernel(xbc_ref, z_ref, dt_ref, cw_ref, cb_ref, alog_ref, dtb_ref, dsk_ref, ng_ref, e_ref, *refs,
                rows, has_init, d_inner):
    if has_init:
        conv0_ref, ssm0_ref = refs[:2]
        refs = refs[2:]
    y_ref, ssm_ref, conv_ref, ext_scr, ht_scr, y_scr = refs
    c = pl.program_id(1)
    cl = SSD_CHUNK
    gw = d_inner // SSD_GROUPS
    hpg = gw // SSD_HEAD_DIM
    n_heads = d_inner // SSD_HEAD_DIM
    conv_dim = d_inner + 2 * SSD_GROUPS * D_STATE
    conv_w = cw_ref.shape[0]
    top = SUBLANES
    row_io = lax.broadcasted_iota(I32, (cl, 1), 0)

    def padded(ref):
        a = ref[0]
        if rows == cl:
            return a
        return jnp.where(row_io < rows, jnp.broadcast_to(a, (cl, a.shape[1])), 0.0)

    @pl.when(c == 0)
    def _():
        ext_scr[0:top, :] = jnp.zeros((top, conv_dim), F32)
        if has_init:
            ext_scr[top - conv_w + 1:top, :] = conv0_ref[0]
            for g in range(SSD_GROUPS):
                ht_scr[g] = ssm0_ref[0, g * gw:(g + 1) * gw, :].T
        else:
            ht_scr[...] = jnp.zeros_like(ht_scr)

    ext_scr[top:top + cl, :] = padded(xbc_ref)
    conv_ref[0] = ext_scr[top + rows - conv_w + 1:top + rows, :]

    cblk = 512
    for cb in range(conv_dim // cblk):
        sl = slice(cb * cblk, (cb + 1) * cblk)
        acc = jnp.broadcast_to(cb_ref[:, sl], (cl, cblk))
        for w in range(conv_w):
            acc = acc + ext_scr[top - conv_w + 1 + w:top - conv_w + 1 + w + cl, sl] * cw_ref[w:w + 1, sl]
        y_scr[:, sl] = _silu(acc)
    ext_scr[top - conv_w + 1:top, :] = ext_scr[top + cl - conv_w + 1:top + cl, :]

    dt = jax.nn.softplus(padded(dt_ref) + dtb_ref[...])
    if rows < cl:
        dt = jnp.where(row_io < rows, dt, 0.0)
    a_neg2 = -jnp.exp(alog_ref[...]) * LOG2E
    r_io = lax.broadcasted_iota(I32, (cl, cl), 0)
    c_io = lax.broadcasted_iota(I32, (cl, cl), 1)
    tril = r_io >= c_io
    acs = _dot_exact_lhs(tril.astype(BF16), dt * a_neg2)
    acs_t = acs.T
    acs_last = acs[cl - 1:cl, :]
    stacked = jnp.concatenate([dt, jnp.exp2(acs), jnp.exp2(acs_last - acs)], axis=0)
    hi = stacked.astype(BF16)
    lo = (stacked - hi.astype(F32)).astype(BF16)
    expanded = _dot(hi, e_ref[...]) + _dot(lo, e_ref[...])
    dt_x, ea_x, te_x = expanded[0:cl], expanded[cl:2 * cl], expanded[2 * cl:3 * cl]

    z = padded(z_ref)
    for g in range(SSD_GROUPS):
        gs = slice(g * gw, (g + 1) * gw)
        x_g = y_scr[:, gs]
        b_g = y_scr[:, d_inner + g * D_STATE:d_inner + (g + 1) * D_STATE]
        c_g = y_scr[:, d_inner + (SSD_GROUPS + g) * D_STATE:d_inner + (SSD_GROUPS + g + 1) * D_STATE]
        c16 = c_g.astype(BF16)
        cbm = _dot_nt(c16, b_g.astype(BF16))
        xdt = x_g * dt_x[:, gs]
        xdt16 = xdt.astype(BF16)
        ht = ht_scr[g]
        y_g = _dot(c16, ht.astype(BF16)) * ea_x[:, gs] + dsk_ref[:, gs] * x_g
        ht_scr[g] = ht * ea_x[cl - 1:cl, gs] + _dot(b_g.T.astype(BF16), (xdt * te_x[:, gs]).astype(BF16))
        diag = []
        for r in range(hpg):
            h = g * hpg + r
            seg = acs[:, h:h + 1] - acs_t[h:h + 1, :]
            m = (cbm * jnp.exp2(jnp.where(tril, seg, NEG_INF))).astype(BF16)
            diag.append(_dot(m, xdt16[:, r * SSD_HEAD_DIM:(r + 1) * SSD_HEAD_DIM]))
        y_g = (y_g + jnp.concatenate(diag, axis=1)) * _silu(z[:, gs])
        y_g = y_g * lax.rsqrt(jnp.mean(y_g * y_g, axis=1, keepdims=True) + EPS) * ng_ref[:, gs]
        y_ref[0, :, gs] = y_g[0:rows].astype(y_ref.dtype)

    @pl.when(c == pl.num_programs(1) - 1)
    def _():
        for g in range(SSD_GROUPS):
            ssm_ref[0, g * gw:(g + 1) * gw, :] = ht_scr[g].T


def _ssd(xbc, z, dt, conv_w, conv_b, a_log, dt_bias, d_skip, norm_g, *, n_seq, n_chunks, rows,
         conv0=None, ssm0=None):
    conv_dim = xbc.shape[-1]
    d_inner = z.shape[-1]
    n_heads = d_inner // SSD_HEAD_DIM
    cw = conv_w.shape[0]
    has_init = conv0 is not None
    pad = lambda a: jnp.pad(a, (0, LANES - a.shape[0])).reshape(1, LANES)
    expand = np.zeros((LANES, d_inner), np.float32)
    expand[np.arange(d_inner) // SSD_HEAD_DIM, np.arange(d_inner)] = 1.0
    step = lambda b, c: (b * n_chunks + c, 0, 0)
    seq = lambda b, c: (b, 0, 0)
    const = lambda b, c: (0, 0)
    in_specs = [pl.BlockSpec((1, rows, conv_dim), step), pl.BlockSpec((1, rows, d_inner), step),
                pl.BlockSpec((1, rows, LANES), step),
                pl.BlockSpec((cw, conv_dim), const), pl.BlockSpec((1, conv_dim), const),
                pl.BlockSpec((1, LANES), const), pl.BlockSpec((1, LANES), const),
                pl.BlockSpec((1, d_inner), const), pl.BlockSpec((1, d_inner), const),
                pl.BlockSpec((LANES, d_inner), const)]
    args = [xbc, z, dt, conv_w, conv_b.reshape(1, conv_dim), pad(a_log), pad(dt_bias),
            jnp.repeat(d_skip, SSD_HEAD_DIM).reshape(1, d_inner), norm_g.reshape(1, d_inner),
            jnp.asarray(expand, BF16)]
    if has_init:
        in_specs += [pl.BlockSpec((1, cw - 1, conv_dim), seq), pl.BlockSpec((1, d_inner, D_STATE), seq)]
        args += [conv0, ssm0]
    return pl.pallas_call(
        functools.partial(_ssd_kernel, rows=rows, has_init=has_init, d_inner=d_inner),
        grid=(n_seq, n_chunks),
        in_specs=in_specs,
        out_specs=[pl.BlockSpec((1, rows, d_inner), step), pl.BlockSpec((1, d_inner, D_STATE), seq),
                   pl.BlockSpec((1, cw - 1, conv_dim), seq)],
        out_shape=[jax.ShapeDtypeStruct((n_seq * n_chunks, rows, d_inner), BF16),
                   jax.ShapeDtypeStruct((n_seq, d_inner, D_STATE), F32),
                   jax.ShapeDtypeStruct((n_seq, cw - 1, conv_dim), F32)],
        scratch_shapes=[pltpu.VMEM((SUBLANES + SSD_CHUNK, conv_dim), F32),
                        pltpu.VMEM((SSD_GROUPS, D_STATE, d_inner // SSD_GROUPS), F32),
                        pltpu.VMEM((SSD_CHUNK, conv_dim), F32)],
        compiler_params=_params("arbitrary", "arbitrary"),
    )(*args)


def _merge_kernel(x_ref, sh_ref, sc_ref, gt_ref, g_ref, att_ref, ssd_ref, wg_ref, wa_ref, ws_ref, wo_ref, o_ref):
    x = x_ref[...]
    d = x.shape[1]
    h = (_rmsnorm(x, g_ref[...]) * (1.0 + sc_ref[0]) + sh_ref[0]).astype(BF16)
    gates = _dot(h, wg_ref[...])
    merged = (_sigmoid(gates[:, 0:d]) * _dot(att_ref[...], wa_ref[...])
              + _sigmoid(gates[:, d:2 * d]) * _dot(ssd_ref[...], ws_ref[...]))
    o_ref[...] = x + gt_ref[0] * _dot(merged.astype(BF16), wo_ref[...])


def _merge(x, shift, scale, gate, norm_g, att, ssd_y, w_g, w_a, w_s, w_o, *, tm):
    m, d = x.shape
    row = lambda i: (i, 0)
    const = lambda i: (0, 0)
    return pl.pallas_call(
        _merge_kernel,
        grid=(m // tm,),
        in_specs=[pl.BlockSpec((tm, d), row), shift.spec(tm), scale.spec(tm), gate.spec(tm),
                  pl.BlockSpec((1, d), const),
                  pl.BlockSpec((tm, att.shape[1]), row), pl.BlockSpec((tm, ssd_y.shape[1]), row),
                  pl.BlockSpec(w_g.shape, const), pl.BlockSpec(w_a.shape, const),
                  pl.BlockSpec(w_s.shape, const), pl.BlockSpec(w_o.shape, const)],
        out_specs=pl.BlockSpec((tm, d), row),
        out_shape=jax.ShapeDtypeStruct((m, d), F32),
        compiler_params=_params("parallel"),
    )(x, shift.arr, scale.arr, gate.arr, norm_g, att, ssd_y, w_g, w_a, w_s, w_o)


def _pad_cols(w, width):
    return jnp.pad(w, ((0, 0), (0, width - w.shape[1])))


def _trunk(x, mods, rows_per_seq, p, tm, attend, ssd_fn, final_g):
    sh1, sc1, g1, sh2, sc2, g2, sh3, sc3, g3 = [_Mod(a, rows_per_seq) for a in mods]
    x = _ffn(x, sh1, sc1, g1, p["norm_ffn1"], p["w_ffn1_in"], p["w_ffn1_out"], final_g, tm=tm, final_norm=False)
    q, k, v, qi, ki, wi = _proj(x, sh2, sc2, p["norm_mix"], p["w_att"], p["seg_att"], tm=tm)
    z, xbc, dt = _proj(x, sh2, sc2, p["norm_mix"], p["w_ssd"], p["seg_ssd"], tm=tm)
    att = attend(q, k, v, qi, ki, wi)
    ssd_y, ssm_new, conv_new = ssd_fn(z, xbc, dt)
    x = _merge(x, sh2, sc2, g2, p["norm_mix"], att, ssd_y, p["w_gate"], p["w_attn_out"], p["w_ssd_out"], p["w_out"],
               tm=min(tm, 256))
    y = _ffn(x, sh3, sc3, g3, p["norm_ffn2"], p["w_ffn2_in"], p["w_ffn2_out"], final_g, tm=tm, final_norm=True)
    return y, (k, v, ki, ssm_new, conv_new)


def kernel(x_prompt, x_sample, c_prompt, c_sample, cache_k, cache_v, cache_kidx, state_ssm, state_conv, page_table,
           w_ada, b_ada, norm_ffn1, w_ffn1_in, w_ffn1_out, norm_mix, w_in, rel_bias, conv_w, conv_b, a_log, dt_bias,
           d_skip, norm_ssd, w_attn_out, w_ssd_out, w_out, norm_ffn2, w_ffn2_in, w_ffn2_out, norm_final):
    depth = w_ada.shape[0]
    assert depth == 1
    batch, seq, d = x_prompt.shape
    db, dec_seq, _ = x_sample.shape
    assert dec_seq == 1
    n_pool = cache_k.shape[1]
    n_pages = page_table.shape[1]
    d_inner = norm_ssd.shape[1]
    conv_dim = conv_w.shape[2]
    n_ssd_heads = d_inner // SSD_HEAD_DIM
    att_q = N_HEADS * HEAD_DIM
    att_kv = N_KV_HEADS * HEAD_DIM
    idx_q = N_IDX_HEADS * IDX_DIM
    l = 0

    widths = (att_q, att_kv, att_kv, idx_q, IDX_DIM, N_IDX_HEADS, d_inner, conv_dim, n_ssd_heads, d, d)
    bounds = np.concatenate([[0], np.cumsum(widths)])
    assert bounds[-1] == w_in.shape[2]
    cols = [w_in[l][:, bounds[i]:bounds[i + 1]].astype(BF16) for i in range(len(widths))]
    w_q, w_k, w_v, w_qi, w_ki, w_wi, w_z, w_xbc, w_dt, w_ga, w_gs = cols
    row1 = lambda a: a.reshape(1, -1)
    p = {
        "norm_ffn1": row1(norm_ffn1[l]), "w_ffn1_in": w_ffn1_in[l].astype(BF16), "w_ffn1_out": w_ffn1_out[l].astype(BF16),
        "norm_mix": row1(norm_mix[l]),
        "w_att": jnp.concatenate([w_q, w_k, w_v, w_qi, _pad_cols(w_ki, LANES), _pad_cols(w_wi, LANES)], axis=1),
        "seg_att": [(att_q, BF16, HEAD_DIM ** -0.5 * LOG2E), (att_kv, F32, 1.0), (att_kv, F32, 1.0),
                    (idx_q, BF16, 1.0), (IDX_DIM, F32, 1.0), (LANES, F32, 1.0)],
        "w_ssd": jnp.concatenate([w_z, w_xbc, _pad_cols(w_dt, LANES)], axis=1),
        "seg_ssd": [(d_inner, F32, 1.0), (conv_dim, F32, 1.0), (LANES, F32, 1.0)],
        "w_gate": jnp.concatenate([w_ga, w_gs], axis=1),
        "w_attn_out": w_attn_out[l].astype(BF16), "w_ssd_out": w_ssd_out[l].astype(BF16), "w_out": w_out[l].astype(BF16),
        "norm_ffn2": row1(norm_ffn2[l]), "w_ffn2_in": w_ffn2_in[l].astype(BF16), "w_ffn2_out": w_ffn2_out[l].astype(BF16),
    }
    final_g = row1(norm_final)
    ssd_args = (conv_w[l], conv_b[l], a_log[l], dt_bias[l], d_skip[l], norm_ssd[l])

    ada = _ada(jnp.concatenate([c_prompt, c_sample], axis=0), w_ada[l], b_ada[l])
    ada_p = [a.reshape(batch, 1, d) for a in jnp.split(ada[:batch], 9, axis=1)]
    ada_s = [a.reshape(1, db, d) for a in jnp.split(ada[batch:], 9, axis=1)]

    tq = 128
    n_chunks = seq // SSD_CHUNK
    bias_tab = _bias_table(rel_bias, tq, seq)

    def attend_p(q, k, v, qi, ki, wi):
        return _dsa_prompt(q, qi, wi, k, v, ki, bias_tab, batch=batch, seq=seq, tq=tq)

    def ssd_p(z, xbc, dt):
        r3 = lambda a: a.reshape(batch * n_chunks, SSD_CHUNK, a.shape[-1])
        y, ssm, conv = _ssd(r3(xbc), r3(z), r3(dt), *ssd_args, n_seq=batch, n_chunks=n_chunks, rows=SSD_CHUNK)
        return y.reshape(batch * seq, d_inner), ssm, conv

    yp, (k_p, v_p, ki_p, ssm_p, conv_p) = _trunk(x_prompt.reshape(batch * seq, d), ada_p, seq, p, 512,
                                                 attend_p, ssd_p, final_g)

    pg = min(16, n_pages)
    n_top_s = min(TOPK_MAX, (n_pages * PAGE_SIZE + 1) // 4)
    bias_tab_s = _bias_table_s(rel_bias, n_pages)

    def attend_s(q, k, v, qi, ki, wi):
        wi_bc = jnp.broadcast_to(wi[:, :N_IDX_HEADS, None], (db, N_IDX_HEADS, LANES))
        scores = _dsa_s_scores(page_table, qi.reshape(db, N_IDX_HEADS, IDX_DIM), wi_bc,
                               cache_kidx[l].astype(BF16), pg=min(2 * pg, n_pages))
        mask = _dsa_s_select(scores.reshape(n_pages, db, PAGE_SIZE), qi, ki, wi, n_top=n_top_s)
        att = _dsa_s_attend(page_table, mask, bias_tab_s, q.reshape(db, N_HEADS, HEAD_DIM),
                            k.reshape(db, 1, att_kv), v.reshape(db, 1, att_kv),
                            cache_k[l].reshape(n_pool, PAGE_SIZE, att_kv),
                            cache_v[l].reshape(n_pool, PAGE_SIZE, att_kv), pg=pg)
        return att.reshape(db, att_q)

    def ssd_s(z, xbc, dt):
        r3 = lambda a: a.reshape(db, 1, a.shape[-1])
        y, ssm, conv = _ssd(r3(xbc), r3(z), r3(dt), *ssd_args, n_seq=db, n_chunks=1, rows=1,
                            conv0=state_conv[l], ssm0=state_ssm[l].reshape(db, d_inner, D_STATE))
        return y.reshape(db, d_inner), ssm, conv

    ys, (k_s, v_s, ki_s, ssm_s, conv_s) = _trunk(x_sample.reshape(db, d), ada_s, db, p, db,
                                                 attend_s, ssd_s, final_g)

    st = lambda a, *shape: a.reshape((1,) + shape)
    return (yp.reshape(batch, seq, d), ys.reshape(db, 1, d),
            st(k_p, batch, seq, N_KV_HEADS, HEAD_DIM), st(v_p, batch, seq, N_KV_HEADS, HEAD_DIM),
            st(ki_p, batch, seq, IDX_DIM),
            st(ssm_p, batch, n_ssd_heads, SSD_HEAD_DIM, D_STATE), st(conv_p, batch, conv_w.shape[1] - 1, conv_dim),
            st(k_s, db, 1, N_KV_HEADS, HEAD_DIM), st(v_s, db, 1, N_KV_HEADS, HEAD_DIM), st(ki_s, db, 1, IDX_DIM),
            st(ssm_s, db, n_ssd_heads, SSD_HEAD_DIM, D_STATE), st(conv_s, db, conv_w.shape[1] - 1, conv_dim))
```

```python
import functools
import math

import numpy as np
import jax
import jax.numpy as jnp
from jax import lax
from jax.experimental import pallas as pl
from jax.experimental.pallas import tpu as pltpu

N_HEADS = 16
HEAD_DIM = 64
N_KV_HEADS = 4
N_IDX_HEADS = 8
IDX_DIM = 64
TOPK_MAX = 256
N_BUCKETS = 32
MAX_DISTANCE = 128
SSD_HEAD_DIM = 64
SSD_GROUPS = 4
D_STATE = 128
SSD_CHUNK = 128
EPS = 1e-6
PAGE_SIZE = 128

LANES = 128
SUBLANES = 8
VMEM_LIMIT = 56 * 1024 * 1024

F32 = jnp.float32
BF16 = jnp.bfloat16
I32 = jnp.int32
NEG_INF = float("-inf")
INT_MIN = -(2 ** 31)
LOG2E = math.log2(math.e)

_NT = (((1,), (1,)), ((), ()))


def _dot(a, b):
    return jnp.dot(a, b, preferred_element_type=F32)


def _dot_nt(a, b):
    return lax.dot_general(a, b, _NT, preferred_element_type=F32)


def _split3(a):
    hi = a.astype(BF16)
    r = a - hi.astype(F32)
    mid = r.astype(BF16)
    lo = (r - mid.astype(F32)).astype(BF16)
    return hi, mid, lo


def _dot_exact_rhs(a, b_bf16):
    hi, mid, lo = _split3(a)
    return _dot(hi, b_bf16) + _dot(mid, b_bf16) + _dot(lo, b_bf16)


def _dot_exact_lhs(a_bf16, b):
    hi, mid, lo = _split3(b)
    return _dot(a_bf16, hi) + _dot(a_bf16, mid) + _dot(a_bf16, lo)


def _rmsnorm(x, g):
    return (x * lax.rsqrt(jnp.mean(x * x, axis=-1, keepdims=True) + EPS)) * g


def _sigmoid(x):
    return 0.5 * jnp.tanh(0.5 * x) + 0.5


def _silu(x):
    h = 0.5 * x
    return h * jnp.tanh(h) + h


def _params(*sem):
    return pltpu.CompilerParams(dimension_semantics=sem, vmem_limit_bytes=VMEM_LIMIT)


def _t5_bucket_np(dist):
    n = np.maximum(dist, 0)
    max_exact = N_BUCKETS // 2
    nf = np.maximum(n, 1).astype(np.float32)
    val = (np.log(nf / np.float32(max_exact)) / np.float32(math.log(MAX_DISTANCE / max_exact))
           * np.float32(N_BUCKETS - max_exact)).astype(np.float32)
    frac = np.abs(val - np.round(val))
    knife = (frac < 1e-3) & (n > max_exact) & (val < N_BUCKETS - max_exact - 0.5)
    assert not knife.any()
    large = np.minimum(max_exact + val.astype(np.int32), N_BUCKETS - 1)
    return np.where(n < max_exact, n, large).astype(np.int32)


def _ada_kernel(c_ref, w_ref, b_ref, o_ref):
    h = _silu(c_ref[...]).astype(BF16)
    o_ref[...] = _dot(h, w_ref[...].astype(BF16)) + b_ref[...]


def _ada(c, w, b):
    rows, d = c.shape
    n = w.shape[1]
    tn = 1024
    return pl.pallas_call(
        _ada_kernel,
        grid=(n // tn,),
        in_specs=[pl.BlockSpec((rows, d), lambda j: (0, 0)),
                  pl.BlockSpec((d, tn), lambda j: (0, j)),
                  pl.BlockSpec((1, tn), lambda j: (0, j))],
        out_specs=pl.BlockSpec((rows, tn), lambda j: (0, j)),
        out_shape=jax.ShapeDtypeStruct((rows, n), F32),
        compiler_params=_params("arbitrary"),
    )(c, w, b.reshape(1, n))


class _Mod:
    def __init__(self, arr, rows_per_seq):
        self.arr = arr
        self.rows_per_seq = rows_per_seq

    def spec(self, tm):
        r = self.arr.shape[1]
        d = self.arr.shape[2]
        if r == 1:
            per = self.rows_per_seq // tm
            return pl.BlockSpec((1, 1, d), lambda i, *_: (i // per, 0, 0))
        assert r == tm
        return pl.BlockSpec((1, r, d), lambda i, *_: (i, 0, 0))


def _ffn_kernel(x_ref, sh_ref, sc_ref, gt_ref, g_ref, wg_ref, wu_ref, wo_ref, fg_ref, o_ref, h_scr, acc_scr,
                *, final_norm):
    j = pl.program_id(1)

    @pl.when(j == 0)
    def _():
        h = _rmsnorm(x_ref[...], g_ref[...]) * (1.0 + sc_ref[0]) + sh_ref[0]
        h_scr[...] = h.astype(BF16)
        acc_scr[...] = jnp.zeros_like(acc_scr)

    h = h_scr[...]
    gate = _dot(h, wg_ref[...])
    up = _dot(h, wu_ref[...])
    act = (_silu(gate) * up).astype(BF16)
    acc_scr[...] += _dot(act, wo_ref[...])

    @pl.when(j == pl.num_programs(1) - 1)
    def _():
        out = x_ref[...] + 0.5 * gt_ref[0] * acc_scr[...]
        if final_norm:
            out = _rmsnorm(out, fg_ref[...])
        o_ref[...] = out


def _ffn(x, shift, scale, gate, norm_g, w_in, w_out, final_g, *, tm, final_norm):
    m, d = x.shape
    f = w_out.shape[0]
    nj = 2
    tf = f // nj
    assert tf % LANES == 0 and m % tm == 0
    row = lambda i, j: (i, 0)
    const = lambda i, j: (0, 0)
    return pl.pallas_call(
        functools.partial(_ffn_kernel, final_norm=final_norm),
        grid=(m // tm, nj),
        in_specs=[pl.BlockSpec((tm, d), row),
                  shift.spec(tm), scale.spec(tm), gate.spec(tm),
                  pl.BlockSpec((1, d), const),
                  pl.BlockSpec((d, tf), lambda i, j: (0, j)),
                  pl.BlockSpec((d, tf), lambda i, j: (0, j + nj)),
                  pl.BlockSpec((tf, d), lambda i, j: (j, 0)),
                  pl.BlockSpec((1, d), const)],
        out_specs=pl.BlockSpec((tm, d), row),
        out_shape=jax.ShapeDtypeStruct((m, d), F32),
        scratch_shapes=[pltpu.VMEM((tm, d), BF16), pltpu.VMEM((tm, d), F32)],
        compiler_params=_params("parallel", "arbitrary"),
    )(x, shift.arr, scale.arr, gate.arr, norm_g, w_in, w_in, w_out, final_g)


def _proj_kernel(x_ref, sh_ref, sc_ref, g_ref, w_ref, *o_refs, offsets, scales):
    h = (_rmsnorm(x_ref[...], g_ref[...]) * (1.0 + sc_ref[0]) + sh_ref[0]).astype(BF16)
    for o_ref, off, scale in zip(o_refs, offsets, scales):
        width = o_ref.shape[1]
        out = _dot(h, w_ref[:, off:off + width])
        if scale != 1.0:
            out = out * scale
        o_ref[...] = out.astype(o_ref.dtype)


def _proj(x, shift, scale, norm_g, w, segments, *, tm):
    m, d = x.shape
    offsets, off = [], 0
    for width, _, _ in segments:
        offsets.append(off)
        off += -(-width // LANES) * LANES
    assert off == w.shape[1]
    row = lambda i: (i, 0)
    const = lambda i: (0, 0)
    return pl.pallas_call(
        functools.partial(_proj_kernel, offsets=tuple(offsets), scales=tuple(s for _, _, s in segments)),
        grid=(m // tm,),
        in_specs=[pl.BlockSpec((tm, d), row), shift.spec(tm), scale.spec(tm),
                  pl.BlockSpec((1, d), const), pl.BlockSpec(w.shape, const)],
        out_specs=[pl.BlockSpec((tm, width), row) for width, _, _ in segments],
        out_shape=[jax.ShapeDtypeStruct((m, width), dt) for width, dt, _ in segments],
        compiler_params=_params("parallel"),
    )(x, shift.arr, scale.arr, norm_g, w)


def _sort_key(score):
    score = jnp.where(score == 0.0, 0.0, score)
    bits = pltpu.bitcast(score, I32)
    return jnp.where(bits >= 0, bits, bits ^ jnp.int32(0x7FFFFFFF))


def _radix_threshold(count_ge, n_top, shape):
    def bit_body(t, carry):
        thr_u, cnt_thr = carry
        cand_u = thr_u | jnp.left_shift(jnp.int32(1), 31 - t)
        cnt = count_ge(cand_u ^ jnp.int32(INT_MIN))
        take = cnt >= n_top
        return jnp.where(take, cand_u, thr_u), jnp.where(take, cnt, cnt_thr)

    thr_u, cnt_thr = lax.fori_loop(0, 32, bit_body, (jnp.zeros(shape, I32), jnp.zeros(shape, I32)))
    return thr_u ^ jnp.int32(INT_MIN), cnt_thr


def _bias_table_kernel(rb_ref, bkt_ref, o_ref):
    far = N_BUCKETS - 1
    for slot in range(bkt_ref.shape[0]):
        bkt = bkt_ref[slot]
        for h in range(N_HEADS):
            acc = jnp.zeros(bkt.shape, F32)
            for b in range(N_BUCKETS - 1):
                acc = jnp.where(bkt == b, (rb_ref[b, h] - rb_ref[far, h]) * LOG2E, acc)
            o_ref[slot, h] = acc


def _bias_table(rel_bias, tq, seq):
    s = np.arange(tq)[:, None]
    t = np.arange(tq)[None, :]
    bkt = np.stack([_t5_bucket_np(t + tq - s), _t5_bucket_np(t - s)])
    assert (_t5_bucket_np(np.arange(tq + 1, seq + 1)) == N_BUCKETS - 1).all()
    return pl.pallas_call(
        _bias_table_kernel,
        in_specs=[pl.BlockSpec(memory_space=pltpu.SMEM), pl.BlockSpec(memory_space=pltpu.VMEM)],
        out_specs=pl.BlockSpec(memory_space=pltpu.VMEM),
        out_shape=jax.ShapeDtypeStruct((2, N_HEADS, tq, tq), F32),
    )(rel_bias, jnp.asarray(bkt))


_V_ROWS = HEAD_DIM + 16


def _dsa_prompt_kernel(q_ref, qi_ref, wi_ref, k_ref, v_ref, ki_ref, bias_ref, o_ref,
                       kb_scr, vt_scr, kib_scr, key_scr, mb_scr, m_scr, al_scr, lg_scr, acc_scr, *, n_top, tq):
    i = pl.program_id(1)
    nq = N_HEADS // N_KV_HEADS
    nblk = key_scr.shape[0]
    s_io = lax.broadcasted_iota(I32, (tq, tq), 0)
    t_io = lax.broadcasted_iota(I32, (tq, tq), 1)
    causal = s_io <= t_io

    @pl.when(i == 0)
    def _():
        kb_scr[...] = k_ref[...].astype(BF16)
        kib_scr[...] = ki_ref[...].astype(BF16)
        for c in range(nblk):
            for pair in range(N_KV_HEADS // 2):
                vt = v_ref[c * tq:(c + 1) * tq, pair * 2 * HEAD_DIM:(pair + 1) * 2 * HEAD_DIM].T.astype(BF16)
                vt_scr[c, 2 * pair, 0:HEAD_DIM, :] = vt[0:HEAD_DIM]
                vt_scr[c, 2 * pair + 1, 0:HEAD_DIM, :] = vt[HEAD_DIM:2 * HEAD_DIM]
            for n in range(N_KV_HEADS):
                vt_scr[c, n, HEAD_DIM:_V_ROWS, :] = jnp.ones((_V_ROWS - HEAD_DIM, tq), BF16)

    def chunk(j):
        return pl.ds(pl.multiple_of(j * tq, tq), tq)

    def fold(a):
        return jnp.sum(a.reshape(tq // SUBLANES, SUBLANES, tq), axis=0)

    qi = qi_ref[...]
    qi_st = jnp.concatenate([qi[:, h * IDX_DIM:(h + 1) * IDX_DIM] for h in range(N_IDX_HEADS)], axis=0)
    wi_t = wi_ref[...].T
    wi_row = jnp.concatenate([wi_t[h:h + 1, :] for h in range(N_IDX_HEADS)], axis=1)
    idx_scale = (N_IDX_HEADS * IDX_DIM) ** -0.5

    def visible(j):
        return (j < i) | ((j == i) & causal)

    n_pairs = (i + 2) // 2

    def score_body(jp, carry):
        for j in (2 * jp, 2 * jp + 1):
            d = jnp.maximum(_dot_nt(kib_scr[chunk(j), :], qi_st), 0.0) * wi_row
            s = d[:, 0:tq]
            for h in range(1, N_IDX_HEADS):
                s = s + d[:, h * tq:(h + 1) * tq]
            key_scr[j] = _sort_key(jnp.where(visible(j), s * idx_scale, NEG_INF))
        return carry

    lax.fori_loop(0, n_pairs, score_body, 0)

    def count(pred):
        def body(jp, c):
            return c + fold(pred(key_scr[2 * jp]).astype(I32)) + fold(pred(key_scr[2 * jp + 1]).astype(I32))
        c = lax.fori_loop(0, n_pairs, body, jnp.zeros((SUBLANES, tq), I32))
        return jnp.sum(c, axis=0, keepdims=True)

    thr, cnt_thr = _radix_threshold(lambda cand: count(lambda key: key >= cand), n_top, (1, tq))

    def sel_plain():
        def body(j, carry):
            mb_scr[j] = jnp.where((key_scr[j] >= thr) & visible(j), 0.0, NEG_INF)
            return carry
        lax.fori_loop(0, i + 1, body, 0)

    def sel_ties():
        need = (n_top - count(lambda key: key > thr)).astype(F32)
        tril = (s_io >= t_io).astype(BF16)

        def body(j, run_eq):
            key = key_scr[j]
            eq = key == thr
            pre = _dot(tril, eq.astype(BF16))
            sel = ((key > thr) | (eq & (run_eq + pre <= need))) & visible(j)
            mb_scr[j] = jnp.where(sel, 0.0, NEG_INF)
            return run_eq + pre[tq - 1:tq, :]
        lax.fori_loop(0, i + 1, body, jnp.zeros((1, tq), F32))

    lax.cond(jnp.max(cnt_thr) > n_top, sel_ties, sel_plain)

    q = q_ref[...]
    q_st = [jnp.concatenate([q[:, (n * nq + g) * HEAD_DIM:(n * nq + g + 1) * HEAD_DIM] for g in range(nq)], axis=0)
            for n in range(N_KV_HEADS)]
    m_scr[...] = jnp.full(m_scr.shape, -1e30, F32)
    acc_scr[...] = jnp.zeros_like(acc_scr)

    heads = range(N_KV_HEADS)

    def stage_a(j, slot):
        mb4 = jnp.concatenate([mb_scr[j]] * nq, axis=1)
        lgs = [_dot_nt(kb_scr[chunk(j), n * HEAD_DIM:(n + 1) * HEAD_DIM], q_st[n]) + mb4 for n in heads]
        if slot is not None:
            lgs = [lgs[n] + jnp.concatenate([bias_ref[slot, n * nq + g] for g in range(nq)], axis=1) for n in heads]
        for n in heads:
            m_old = m_scr[n]
            m_new = jnp.maximum(m_old, jnp.max(lgs[n], axis=0, keepdims=True))
            al_scr[n] = jnp.exp2(m_old - m_new)
            m_scr[n] = m_new
            lg_scr[j % 2, n] = lgs[n]

    def stage_b(j):
        ps = [jnp.exp2(lg_scr[j % 2, n] - m_scr[n]).astype(BF16) for n in heads]
        pvs = [_dot(vt_scr[j, n], ps[n]) for n in heads]
        for n in heads:
            acc_scr[n] = al_scr[n] * acc_scr[n] + pvs[n]

    @pl.when(i >= 2)
    def _():
        stage_a(0, None)

        def far_body(j, carry):
            stage_b(j - 1)
            stage_a(j, None)
            return carry

        lax.fori_loop(1, i - 1, far_body, 0)
        stage_b(i - 2)
        stage_a(i - 1, 0)

    @pl.when(i == 1)
    def _():
        stage_a(0, 0)

    @pl.when(i >= 1)
    def _():
        stage_b(i - 1)
        stage_a(i, 1)

    @pl.when(i == 0)
    def _():
        stage_a(0, 1)

    stage_b(i)

    for n in range(N_KV_HEADS):
        acc = acc_scr[n]
        out_t = acc[0:HEAD_DIM] / acc[HEAD_DIM:HEAD_DIM + 1]
        for pair in range(nq // 2):
            g0 = 2 * pair
            two = jnp.concatenate([out_t[:, g0 * tq:(g0 + 1) * tq], out_t[:, (g0 + 1) * tq:(g0 + 2) * tq]], axis=0)
            h0 = n * nq + g0
            o_ref[:, h0 * HEAD_DIM:(h0 + 2) * HEAD_DIM] = two.T.astype(o_ref.dtype)


def _dsa_prompt(q, qi, wi, k, v, ki, bias_tab, *, batch, seq, tq):
    nblk = seq // tq
    n_top = min(TOPK_MAX, seq // 4)
    nq = N_HEADS // N_KV_HEADS
    assert tq == LANES and nq % 2 == 0 and N_KV_HEADS % 2 == 0
    blk = lambda b, i: (b * nblk + i, 0)
    whole = lambda b, i: (b, 0)
    kvw = N_KV_HEADS * HEAD_DIM
    return pl.pallas_call(
        functools.partial(_dsa_prompt_kernel, n_top=n_top, tq=tq),
        grid=(batch, nblk),
        in_specs=[pl.BlockSpec((tq, N_HEADS * HEAD_DIM), blk),
                  pl.BlockSpec((tq, N_IDX_HEADS * IDX_DIM), blk),
                  pl.BlockSpec((tq, LANES), blk),
                  pl.BlockSpec((seq, kvw), whole),
                  pl.BlockSpec((seq, kvw), whole),
                  pl.BlockSpec((seq, IDX_DIM), whole),
                  pl.BlockSpec(bias_tab.shape, lambda b, i: (0, 0, 0, 0))],
        out_specs=pl.BlockSpec((tq, N_HEADS * HEAD_DIM), blk),
        out_shape=jax.ShapeDtypeStruct((batch * seq, N_HEADS * HEAD_DIM), BF16),
        scratch_shapes=[pltpu.VMEM((seq, kvw), BF16),
                        pltpu.VMEM((nblk, N_KV_HEADS, _V_ROWS, tq), BF16),
                        pltpu.VMEM((seq, IDX_DIM), BF16),
                        pltpu.VMEM((nblk, tq, tq), I32), pltpu.VMEM((nblk, tq, tq), F32),
                        pltpu.VMEM((N_KV_HEADS, 1, nq * tq), F32), pltpu.VMEM((N_KV_HEADS, 1, nq * tq), F32),
                        pltpu.VMEM((2, N_KV_HEADS, tq, nq * tq), F32),
                        pltpu.VMEM((N_KV_HEADS, _V_ROWS, nq * tq), F32)],
        compiler_params=_params("arbitrary", "arbitrary"),
    )(q, qi, wi, k, v, ki, bias_tab)


def _dsa_s_score_kernel(pt_ref, qi_ref, wi_ref, *refs, pg):
    page_refs, o_ref = refs[:pg], refs[pg]
    qi = qi_ref[0].astype(BF16)
    wi = wi_ref[0]
    idx_scale = (N_IDX_HEADS * IDX_DIM) ** -0.5
    for r in range(pg):
        d = _dot_nt(qi, page_refs[r][0])
        o_ref[r, 0] = jnp.sum(jnp.maximum(d, 0.0) * wi, axis=0, keepdims=True) * idx_scale


def _dsa_s_scores(page_table, qi, wi_bc, cache_kidx, *, pg):
    db, n_pages = page_table.shape
    page_spec = lambda r: pl.BlockSpec(
        (1, PAGE_SIZE, IDX_DIM), lambda b, s, pt: (pt[b * n_pages + s * pg + r], 0, 0))
    grid_spec = pltpu.PrefetchScalarGridSpec(
        num_scalar_prefetch=1,
        grid=(db, n_pages // pg),
        in_specs=[pl.BlockSpec((1, N_IDX_HEADS, IDX_DIM), lambda b, s, pt: (b, 0, 0)),
                  pl.BlockSpec((1, N_IDX_HEADS, LANES), lambda b, s, pt: (b, 0, 0))]
                 + [page_spec(r) for r in range(pg)],
        out_specs=pl.BlockSpec((pg, 1, 1, PAGE_SIZE), lambda b, s, pt: (s, b, 0, 0)),
    )
    return pl.pallas_call(
        functools.partial(_dsa_s_score_kernel, pg=pg),
        grid_spec=grid_spec,
        out_shape=jax.ShapeDtypeStruct((n_pages, db, 1, PAGE_SIZE), F32),
        compiler_params=_params("arbitrary", "arbitrary"),
    )(page_table.reshape(-1), qi, wi_bc, *([cache_kidx] * pg))


def _dsa_s_select_kernel(sc_ref, qi_ref, kis_ref, wi_ref, hsum_ref, rep_ref, o_ref, key_scr, *, n_top):
    n_pages, db, _ = sc_ref.shape
    idx_scale = (N_IDX_HEADS * IDX_DIM) ** -0.5

    prod = (qi_ref[...].astype(BF16).astype(F32)
            * jnp.concatenate([kis_ref[...].astype(BF16).astype(F32)] * N_IDX_HEADS, axis=1))
    d_self = _dot_exact_rhs(prod, hsum_ref[...])
    s_self = jnp.sum((jnp.maximum(d_self, 0.0) * wi_ref[...]).T, axis=0, keepdims=True) * idx_scale
    key_self = _sort_key(s_self)

    def to_keys(r, carry):
        key_scr[r] = _sort_key(sc_ref[r].T)
        return carry
    lax.fori_loop(0, n_pages, to_keys, 0)

    def fold(a):
        return jnp.sum(a.reshape(PAGE_SIZE // SUBLANES, SUBLANES, db), axis=0)

    def count(pred):
        def body(rp, c):
            return c + fold(pred(key_scr[2 * rp]).astype(I32)) + fold(pred(key_scr[2 * rp + 1]).astype(I32))
        c = lax.fori_loop(0, n_pages // 2, body, jnp.zeros((SUBLANES, db), I32))
        return jnp.sum(c, axis=0, keepdims=True) + pred(key_self).astype(I32)

    thr, cnt_thr = _radix_threshold(lambda cand: count(lambda key: key >= cand), n_top, (1, db))

    def emit(r, sel):
        rows = _dot(rep_ref[...], sel.astype(BF16))
        o_ref[r] = jnp.where(rows > 0.5, 0.0, NEG_INF).T

    def sel_plain():
        def body(r, carry):
            emit(r, key_scr[r] >= thr)
            return carry
        lax.fori_loop(0, n_pages, body, 0)
        emit(n_pages, jnp.broadcast_to(key_self >= thr, (PAGE_SIZE, db)))

    def sel_ties():
        need = (n_top - count(lambda key: key > thr)).astype(F32)
        r_io = lax.broadcasted_iota(I32, (PAGE_SIZE, PAGE_SIZE), 0)
        c_io = lax.broadcasted_iota(I32, (PAGE_SIZE, PAGE_SIZE), 1)
        tril = (r_io >= c_io).astype(BF16)

        def body(r, run_eq):
            key = key_scr[r]
            eq = key == thr
            pre = _dot(tril, eq.astype(BF16))
            emit(r, (key > thr) | (eq & (run_eq + pre <= need)))
            return run_eq + pre[PAGE_SIZE - 1:PAGE_SIZE, :]
        run_eq = lax.fori_loop(0, n_pages, body, jnp.zeros((1, db), F32))
        sel_self = (key_self > thr) | ((key_self == thr) & (run_eq + 1.0 <= need))
        emit(n_pages, jnp.broadcast_to(sel_self, (PAGE_SIZE, db)))

    lax.cond(jnp.max(cnt_thr) > n_top, sel_ties, sel_plain)


def _dsa_s_select(scores, qi, ki_s, wi, *, n_top):
    n_pages, db, _ = scores.shape
    assert db == LANES and n_pages % 2 == 0
    hsum = np.zeros((N_IDX_HEADS * IDX_DIM, LANES), np.float32)
    hsum[np.arange(N_IDX_HEADS * IDX_DIM), np.arange(N_IDX_HEADS * IDX_DIM) // IDX_DIM] = 1.0
    rep = np.zeros((PAGE_SIZE * N_KV_HEADS, PAGE_SIZE), np.float32)
    rep[np.arange(PAGE_SIZE * N_KV_HEADS), np.arange(PAGE_SIZE * N_KV_HEADS) // N_KV_HEADS] = 1.0
    return pl.pallas_call(
        functools.partial(_dsa_s_select_kernel, n_top=n_top),
        out_shape=jax.ShapeDtypeStruct((n_pages + 1, db, PAGE_SIZE * N_KV_HEADS), F32),
        scratch_shapes=[pltpu.VMEM((n_pages, PAGE_SIZE, db), I32)],
        compiler_params=pltpu.CompilerParams(vmem_limit_bytes=VMEM_LIMIT),
    )(scores, qi, ki_s, wi, jnp.asarray(hsum, BF16), jnp.asarray(rep, BF16))


def _bias_table_s_kernel(rbt_ref, bkt_ref, own_ref, o_ref):
    rbt = rbt_ref[...] * LOG2E
    n_rows = bkt_ref.shape[0]
    own = own_ref[...] > 0
    for r in range(n_rows):
        bkt = bkt_ref[r:r + 1, :]
        out = jnp.zeros((N_HEADS, bkt.shape[1]), F32)
        for b in range(N_BUCKETS):
            out = jnp.where(bkt == b, rbt[:, b:b + 1], out)
        o_ref[r] = out if r == n_rows - 1 else jnp.where(own, out, NEG_INF)


def _bias_table_s(rel_bias, n_pages):
    past = n_pages * PAGE_SIZE
    width = PAGE_SIZE * N_KV_HEADS
    pos = np.arange(n_pages + 1)[:, None] * PAGE_SIZE + np.arange(width)[None, :] // N_KV_HEADS
    pos[n_pages] = past
    own = (np.arange(width)[None, :] % N_KV_HEADS) == (np.arange(N_HEADS)[:, None] // (N_HEADS // N_KV_HEADS))
    return pl.pallas_call(
        _bias_table_s_kernel,
        out_shape=jax.ShapeDtypeStruct((n_pages + 1, N_HEADS, width), F32),
    )(rel_bias.T, jnp.asarray(_t5_bucket_np(past - pos)), jnp.asarray(own, I32))


def _dsa_s_attend_kernel(pt_ref, mb_ref, bias_ref, q_ref, ks_ref, vs_ref, *refs, pg, n_pages):
    k_refs, v_refs = refs[:pg], refs[pg:2 * pg]
    o_ref, m_scr, l_scr, acc_scr = refs[2 * pg:]
    s = pl.program_id(1)
    nq = N_HEADS // N_KV_HEADS
    kvw = N_KV_HEADS * HEAD_DIM

    @pl.when(s == 0)
    def _():
        m_scr[...] = jnp.full(m_scr.shape, -1e30, F32)
        l_scr[...] = jnp.zeros_like(l_scr)
        acc_scr[...] = jnp.zeros_like(acc_scr)

    q16 = q_ref[0]

    def page_rows(ref):
        return ref[0].reshape(PAGE_SIZE * N_KV_HEADS, HEAD_DIM).astype(BF16)

    logits = []
    for r in range(pg):
        page = s * pg + r
        logits.append(_dot_nt(q16, page_rows(k_refs[r])) + bias_ref[page] + mb_ref[page, 0])
    m_old = m_scr[...]
    m_new = m_old
    for lg in logits:
        m_new = jnp.maximum(m_new, jnp.max(lg, axis=1, keepdims=True))
    alpha = jnp.exp2(m_old - m_new)
    l_new = alpha * l_scr[...]
    acc = alpha * acc_scr[...]
    for r, lg in enumerate(logits):
        p = jnp.exp2(lg - m_new)
        l_new = l_new + jnp.sum(p, axis=1, keepdims=True)
        acc = acc + _dot(p.astype(BF16), page_rows(v_refs[r]))
    m_scr[...] = m_new
    l_scr[...] = l_new
    acc_scr[...] = acc

    @pl.when(s == pl.num_programs(1) - 1)
    def _():
        kv_of_head = lax.broadcasted_iota(I32, (N_HEADS, HEAD_DIM), 0) // nq

        def own_cols(ref):
            row = ref[0].astype(BF16).astype(F32)
            out = jnp.zeros((N_HEADS, HEAD_DIM), F32)
            for n in range(N_KV_HEADS):
                out = jnp.where(kv_of_head == n, row[:, n * HEAD_DIM:(n + 1) * HEAD_DIM], out)
            return out

        lg = jnp.sum(q16.astype(F32) * own_cols(ks_ref), axis=1, keepdims=True)
        lg = lg + bias_ref[n_pages][:, 0:1] + mb_ref[n_pages, 0][:, 0:1]
        m_fin = jnp.maximum(m_new, lg)
        a = jnp.exp2(m_new - m_fin)
        p = jnp.exp2(lg - m_fin)
        l_fin = a * l_new + p
        out = (a * acc + p.astype(BF16).astype(F32) * own_cols(vs_ref)) / l_fin
        o_ref[0] = out.astype(o_ref.dtype)


def _dsa_s_attend(page_table, mask, bias_tab, q, k_s, v_s, cache_k, cache_v, *, pg):
    db, n_pages = page_table.shape
    kvw = N_KV_HEADS * HEAD_DIM
    width = PAGE_SIZE * N_KV_HEADS
    per_seq = lambda *shape: pl.BlockSpec((1,) + shape, lambda b, s, pt: (b,) + (0,) * len(shape))
    page_spec = lambda r: pl.BlockSpec(
        (1, PAGE_SIZE, N_KV_HEADS, HEAD_DIM), lambda b, s, pt: (pt[b * n_pages + s * pg + r], 0, 0, 0))
    grid_spec = pltpu.PrefetchScalarGridSpec(
        num_scalar_prefetch=1,
        grid=(db, n_pages // pg),
        in_specs=[pl.BlockSpec((n_pages + 1, 1, 1, width), lambda b, s, pt: (0, b, 0, 0)),
                  pl.BlockSpec(bias_tab.shape, lambda b, s, pt: (0, 0, 0)),
                  per_seq(N_HEADS, HEAD_DIM), per_seq(1, kvw), per_seq(1, kvw)]
                 + [page_spec(r) for r in range(pg)] * 2,
        out_specs=per_seq(N_HEADS, HEAD_DIM),
        scratch_shapes=[pltpu.VMEM((N_HEADS, 1), F32), pltpu.VMEM((N_HEADS, 1), F32),
                        pltpu.VMEM((N_HEADS, HEAD_DIM), F32)],
    )
    return pl.pallas_call(
        functools.partial(_dsa_s_attend_kernel, pg=pg, n_pages=n_pages),
        grid_spec=grid_spec,
        out_shape=jax.ShapeDtypeStruct((db, N_HEADS, HEAD_DIM), BF16),
        compiler_params=_params("arbitrary", "arbitrary"),
    )(page_table.reshape(-1), mask.reshape(n_pages + 1, db, 1, width), bias_tab, q, k_s, v_s,
      *([cache_k] * pg), *([cache_v] * pg))


def _ssd_kernel(xbc_ref, z_ref, dt_ref, cw_ref, cb_ref, alog_ref, dtb_ref, dsk_ref, ng_ref, e_ref, *refs,
                rows, has_init, d_inner):
    if has_init:
        conv0_ref, ssm0_ref = refs[:2]
        refs = refs[2:]
    y_ref, ssm_ref, conv_ref, ext_scr, ht_scr, y_scr = refs
    c = pl.program_id(1)
    cl = SSD_CHUNK
    gw = d_inner // SSD_GROUPS
    hpg = gw // SSD_HEAD_DIM
    n_heads = d_inner // SSD_HEAD_DIM
    conv_dim = d_inner + 2 * SSD_GROUPS * D_STATE
    conv_w = cw_ref.shape[0]
    top = SUBLANES
    row_io = lax.broadcasted_iota(I32, (cl, 1), 0)

    def padded(ref):
        a = ref[0]
        if rows == cl:
            return a
        return jnp.where(row_io < rows, jnp.broadcast_to(a, (cl, a.shape[1])), 0.0)

    @pl.when(c == 0)
    def _():
        ext_scr[0:top, :] = jnp.zeros((top, conv_dim), F32)
        if has_init:
            ext_scr[top - conv_w + 1:top, :] = conv0_ref[0]
            for g in range(SSD_GROUPS):
                ht_scr[g] = ssm0_ref[0, g * gw:(g + 1) * gw, :].T
        else:
            ht_scr[...] = jnp.zeros_like(ht_scr)

    ext_scr[top:top + cl, :] = padded(xbc_ref)
    conv_ref[0] = ext_scr[top + rows - conv_w + 1:top + rows, :]

    cblk = 512
    for cb in range(conv_dim // cblk):
        sl = slice(cb * cblk, (cb + 1) * cblk)
        acc = jnp.broadcast_to(cb_ref[:, sl], (cl, cblk))
        for w in range(conv_w):
            acc = acc + ext_scr[top - conv_w + 1 + w:top - conv_w + 1 + w + cl, sl] * cw_ref[w:w + 1, sl]
        y_scr[:, sl] = _silu(acc)
    ext_scr[top - conv_w + 1:top, :] = ext_scr[top + cl - conv_w + 1:top + cl, :]

    dt = jax.nn.softplus(padded(dt_ref) + dtb_ref[...])
    if rows < cl:
        dt = jnp.where(row_io < rows, dt, 0.0)
    a_neg2 = -jnp.exp(alog_ref[...]) * LOG2E
    r_io = lax.broadcasted_iota(I32, (cl, cl), 0)
    c_io = lax.broadcasted_iota(I32, (cl, cl), 1)
    tril = r_io >= c_io
    acs = _dot_exact_lhs(tril.astype(BF16), dt * a_neg2)
    acs_t = acs.T
    acs_last = acs[cl - 1:cl, :]
    stacked = jnp.concatenate([dt, jnp.exp2(acs), jnp.exp2(acs_last - acs)], axis=0)
    hi = stacked.astype(BF16)
    lo = (stacked - hi.astype(F32)).astype(BF16)
    expanded = _dot(hi, e_ref[...]) + _dot(lo, e_ref[...])
    dt_x, ea_x, te_x = expanded[0:cl], expanded[cl:2 * cl], expanded[2 * cl:3 * cl]

    z = padded(z_ref)
    for g in range(SSD_GROUPS):
        gs = slice(g * gw, (g + 1) * gw)
        x_g = y_scr[:, gs]
        b_g = y_scr[:, d_inner + g * D_STATE:d_inner + (g + 1) * D_STATE]
        c_g = y_scr[:, d_inner + (SSD_GROUPS + g) * D_STATE:d_inner + (SSD_GROUPS + g + 1) * D_STATE]
        c16 = c_g.astype(BF16)
        cbm = _dot_nt(c16, b_g.astype(BF16))
        xdt = x_g * dt_x[:, gs]
        xdt16 = xdt.astype(BF16)
        ht = ht_scr[g]
        y_g = _dot(c16, ht.astype(BF16)) * ea_x[:, gs] + dsk_ref[:, gs] * x_g
        ht_scr[g] = ht * ea_x[cl - 1:cl, gs] + _dot(b_g.T.astype(BF16), (xdt * te_x[:, gs]).astype(BF16))
        diag = []
        for r in range(hpg):
            h = g * hpg + r
            seg = acs[:, h:h + 1] - acs_t[h:h + 1, :]
            m = (cbm * jnp.exp2(jnp.where(tril, seg, NEG_INF))).astype(BF16)
            diag.append(_dot(m, xdt16[:, r * SSD_HEAD_DIM:(r + 1) * SSD_HEAD_DIM]))
        y_g = (y_g + jnp.concatenate(diag, axis=1)) * _silu(z[:, gs])
        y_g = y_g * lax.rsqrt(jnp.mean(y_g * y_g, axis=1, keepdims=True) + EPS) * ng_ref[:, gs]
        y_ref[0, :, gs] = y_g[0:rows].astype(y_ref.dtype)

    @pl.when(c == pl.num_programs(1) - 1)
    def _():
        for g in range(SSD_GROUPS):
            ssm_ref[0, g * gw:(g + 1) * gw, :] = ht_scr[g].T


def _ssd(xbc, z, dt, conv_w, conv_b, a_log, dt_bias, d_skip, norm_g, *, n_seq, n_chunks, rows,
         conv0=None, ssm0=None):
    conv_dim = xbc.shape[-1]
    d_inner = z.shape[-1]
    n_heads = d_inner // SSD_HEAD_DIM
    cw = conv_w.shape[0]
    has_init = conv0 is not None
    pad = lambda a: jnp.pad(a, (0, LANES - a.shape[0])).reshape(1, LANES)
    expand = np.zeros((LANES, d_inner), np.float32)
    expand[np.arange(d_inner) // SSD_HEAD_DIM, np.arange(d_inner)] = 1.0
    step = lambda b, c: (b * n_chunks + c, 0, 0)
    seq = lambda b, c: (b, 0, 0)
    const = lambda b, c: (0, 0)
    in_specs = [pl.BlockSpec((1, rows, conv_dim), step), pl.BlockSpec((1, rows, d_inner), step),
                pl.BlockSpec((1, rows, LANES), step),
                pl.BlockSpec((cw, conv_dim), const), pl.BlockSpec((1, conv_dim), const),
                pl.BlockSpec((1, LANES), const), pl.BlockSpec((1, LANES), const),
                pl.BlockSpec((1, d_inner), const), pl.BlockSpec((1, d_inner), const),
                pl.BlockSpec((LANES, d_inner), const)]
    args = [xbc, z, dt, conv_w, conv_b.reshape(1, conv_dim), pad(a_log), pad(dt_bias),
            jnp.repeat(d_skip, SSD_HEAD_DIM).reshape(1, d_inner), norm_g.reshape(1, d_inner),
            jnp.asarray(expand, BF16)]
    if has_init:
        in_specs += [pl.BlockSpec((1, cw - 1, conv_dim), seq), pl.BlockSpec((1, d_inner, D_STATE), seq)]
        args += [conv0, ssm0]
    return pl.pallas_call(
        functools.partial(_ssd_kernel, rows=rows, has_init=has_init, d_inner=d_inner),
        grid=(n_seq, n_chunks),
        in_specs=in_specs,
        out_specs=[pl.BlockSpec((1, rows, d_inner), step), pl.BlockSpec((1, d_inner, D_STATE), seq),
                   pl.BlockSpec((1, cw - 1, conv_dim), seq)],
        out_shape=[jax.ShapeDtypeStruct((n_seq * n_chunks, rows, d_inner), BF16),
                   jax.ShapeDtypeStruct((n_seq, d_inner, D_STATE), F32),
                   jax.ShapeDtypeStruct((n_seq, cw - 1, conv_dim), F32)],
        scratch_shapes=[pltpu.VMEM((SUBLANES + SSD_CHUNK, conv_dim), F32),
                        pltpu.VMEM((SSD_GROUPS, D_STATE, d_inner // SSD_GROUPS), F32),
                        pltpu.VMEM((SSD_CHUNK, conv_dim), F32)],
        compiler_params=_params("arbitrary", "arbitrary"),
    )(*args)


def _merge_kernel(x_ref, sh_ref, sc_ref, gt_ref, g_ref, att_ref, ssd_ref, wg_ref, wa_ref, ws_ref, wo_ref, o_ref):
    x = x_ref[...]
    d = x.shape[1]
    h = (_rmsnorm(x, g_ref[...]) * (1.0 + sc_ref[0]) + sh_ref[0]).astype(BF16)
    gates = _dot(h, wg_ref[...])
    merged = (_sigmoid(gates[:, 0:d]) * _dot(att_ref[...], wa_ref[...])
              + _sigmoid(gates[:, d:2 * d]) * _dot(ssd_ref[...], ws_ref[...]))
    o_ref[...] = x + gt_ref[0] * _dot(merged.astype(BF16), wo_ref[...])


def _merge(x, shift, scale, gate, norm_g, att, ssd_y, w_g, w_a, w_s, w_o, *, tm):
    m, d = x.shape
    row = lambda i: (i, 0)
    const = lambda i: (0, 0)
    return pl.pallas_call(
        _merge_kernel,
        grid=(m // tm,),
        in_specs=[pl.BlockSpec((tm, d), row), shift.spec(tm), scale.spec(tm), gate.spec(tm),
                  pl.BlockSpec((1, d), const),
                  pl.BlockSpec((tm, att.shape[1]), row), pl.BlockSpec((tm, ssd_y.shape[1]), row),
                  pl.BlockSpec(w_g.shape, const), pl.BlockSpec(w_a.shape, const),
                  pl.BlockSpec(w_s.shape, const), pl.BlockSpec(w_o.shape, const)],
        out_specs=pl.BlockSpec((tm, d), row),
        out_shape=jax.ShapeDtypeStruct((m, d), F32),
        compiler_params=_params("parallel"),
    )(x, shift.arr, scale.arr, gate.arr, norm_g, att, ssd_y, w_g, w_a, w_s, w_o)


def _pad_cols(w, width):
    return jnp.pad(w, ((0, 0), (0, width - w.shape[1])))


def _trunk(x, mods, rows_per_seq, p, tm, attend, ssd_fn, final_g):
    sh1, sc1, g1, sh2, sc2, g2, sh3, sc3, g3 = [_Mod(a, rows_per_seq) for a in mods]
    x = _ffn(x, sh1, sc1, g1, p["norm_ffn1"], p["w_ffn1_in"], p["w_ffn1_out"], final_g, tm=tm, final_norm=False)
    q, k, v, qi, ki, wi = _proj(x, sh2, sc2, p["norm_mix"], p["w_att"], p["seg_att"], tm=tm)
    z, xbc, dt = _proj(x, sh2, sc2, p["norm_mix"], p["w_ssd"], p["seg_ssd"], tm=tm)
    att = attend(q, k, v, qi, ki, wi)
    ssd_y, ssm_new, conv_new = ssd_fn(z, xbc, dt)
    x = _merge(x, sh2, sc2, g2, p["norm_mix"], att, ssd_y, p["w_gate"], p["w_attn_out"], p["w_ssd_out"], p["w_out"],
               tm=min(tm, 256))
    y = _ffn(x, sh3, sc3, g3, p["norm_ffn2"], p["w_ffn2_in"], p["w_ffn2_out"], final_g, tm=tm, final_norm=True)
    return y, (k, v, ki, ssm_new, conv_new)


def kernel(x_prompt, x_sample, c_prompt, c_sample, cache_k, cache_v, cache_kidx, state_ssm, state_conv, page_table,
           w_ada, b_ada, norm_ffn1, w_ffn1_in, w_ffn1_out, norm_mix, w_in, rel_bias, conv_w, conv_b, a_log, dt_bias,
           d_skip, norm_ssd, w_attn_out, w_ssd_out, w_out, norm_ffn2, w_ffn2_in, w_ffn2_out, norm_final):
    depth = w_ada.shape[0]
    assert depth == 1
    batch, seq, d = x_prompt.shape
    db, dec_seq, _ = x_sample.shape
    assert dec_seq == 1
    n_pool = cache_k.shape[1]
    n_pages = page_table.shape[1]
    d_inner = norm_ssd.shape[1]
    conv_dim = conv_w.shape[2]
    n_ssd_heads = d_inner // SSD_HEAD_DIM
    att_q = N_HEADS * HEAD_DIM
    att_kv = N_KV_HEADS * HEAD_DIM
    idx_q = N_IDX_HEADS * IDX_DIM
    l = 0

    widths = (att_q, att_kv, att_kv, idx_q, IDX_DIM, N_IDX_HEADS, d_inner, conv_dim, n_ssd_heads, d, d)
    bounds = np.concatenate([[0], np.cumsum(widths)])
    assert bounds[-1] == w_in.shape[2]
    cols = [w_in[l][:, bounds[i]:bounds[i + 1]].astype(BF16) for i in range(len(widths))]
    w_q, w_k, w_v, w_qi, w_ki, w_wi, w_z, w_xbc, w_dt, w_ga, w_gs = cols
    row1 = lambda a: a.reshape(1, -1)
    p = {
        "norm_ffn1": row1(norm_ffn1[l]), "w_ffn1_in": w_ffn1_in[l].astype(BF16), "w_ffn1_out": w_ffn1_out[l].astype(BF16),
        "norm_mix": row1(norm_mix[l]),
        "w_att": jnp.concatenate([w_q, w_k, w_v, w_qi, _pad_cols(w_ki, LANES), _pad_cols(w_wi, LANES)], axis=1),
        "seg_att": [(att_q, BF16, HEAD_DIM ** -0.5 * LOG2E), (att_kv, F32, 1.0), (att_kv, F32, 1.0),
                    (idx_q, BF16, 1.0), (IDX_DIM, F32, 1.0), (LANES, F32, 1.0)],
        "w_ssd": jnp.concatenate([w_z, w_xbc, _pad_cols(w_dt, LANES)], axis=1),
        "seg_ssd": [(d_inner, F32, 1.0), (conv_dim, F32, 1.0), (LANES, F32, 1.0)],
        "w_gate": jnp.concatenate([w_ga, w_gs], axis=1),
        "w_attn_out": w_attn_out[l].astype(BF16), "w_ssd_out": w_ssd_out[l].astype(BF16), "w_out": w_out[l].astype(BF16),
        "norm_ffn2": row1(norm_ffn2[l]), "w_ffn2_in": w_ffn2_in[l].astype(BF16), "w_ffn2_out": w_ffn2_out[l].astype(BF16),
    }
    final_g = row1(norm_final)
    ssd_args = (conv_w[l], conv_b[l], a_log[l], dt_bias[l], d_skip[l], norm_ssd[l])

    ada = _ada(jnp.concatenate([c_prompt, c_sample], axis=0), w_ada[l], b_ada[l])
    ada_p = [a.reshape(batch, 1, d) for a in jnp.split(ada[:batch], 9, axis=1)]
    ada_s = [a.reshape(1, db, d) for a in jnp.split(ada[batch:], 9, axis=1)]

    tq = 128
    n_chunks = seq // SSD_CHUNK
    bias_tab = _bias_table(rel_bias, tq, seq)

    def attend_p(q, k, v, qi, ki, wi):
        return _dsa_prompt(q, qi, wi, k, v, ki, bias_tab, batch=batch, seq=seq, tq=tq)

    def ssd_p(z, xbc, dt):
        r3 = lambda a: a.reshape(batch * n_chunks, SSD_CHUNK, a.shape[-1])
        y, ssm, conv = _ssd(r3(xbc), r3(z), r3(dt), *ssd_args, n_seq=batch, n_chunks=n_chunks, rows=SSD_CHUNK)
        return y.reshape(batch * seq, d_inner), ssm, conv

    yp, (k_p, v_p, ki_p, ssm_p, conv_p) = _trunk(x_prompt.reshape(batch * seq, d), ada_p, seq, p, 512,
                                                 attend_p, ssd_p, final_g)

    pg = min(16, n_pages)
    n_top_s = min(TOPK_MAX, (n_pages * PAGE_SIZE + 1) // 4)
    bias_tab_s = _bias_table_s(rel_bias, n_pages)

    def attend_s(q, k, v, qi, ki, wi):
        wi_bc = jnp.broadcast_to(wi[:, :N_IDX_HEADS, None], (db, N_IDX_HEADS, LANES))
        scores = _dsa_s_scores(page_table, qi.reshape(db, N_IDX_HEADS, IDX_DIM), wi_bc,
                               cache_kidx[l].astype(BF16), pg=min(2 * pg, n_pages))
        mask = _dsa_s_select(scores.reshape(n_pages, db, PAGE_SIZE), qi, ki, wi, n_top=n_top_s)
        att = _dsa_s_attend(page_table, mask, bias_tab_s, q.reshape(db, N_HEADS, HEAD_DIM),
                            k.reshape(db, 1, att_kv), v.reshape(db, 1, att_kv),
                            cache_k[l], cache_v[l], pg=pg)
        return att.reshape(db, att_q)

    def ssd_s(z, xbc, dt):
        r3 = lambda a: a.reshape(db, 1, a.shape[-1])
        y, ssm, conv = _ssd(r3(xbc), r3(z), r3(dt), *ssd_args, n_seq=db, n_chunks=1, rows=1,
                            conv0=state_conv[l], ssm0=state_ssm[l].reshape(db, d_inner, D_STATE))
        return y.reshape(db, d_inner), ssm, conv

    ys, (k_s, v_s, ki_s, ssm_s, conv_s) = _trunk(x_sample.reshape(db, d), ada_s, db, p, db,
                                                 attend_s, ssd_s, final_g)

    st = lambda a, *shape: a.reshape((1,) + shape)
    return (yp.reshape(batch, seq, d), ys.reshape(db, 1, d),
            st(k_p, batch, seq, N_KV_HEADS, HEAD_DIM), st(v_p, batch, seq, N_KV_HEADS, HEAD_DIM),
            st(ki_p, batch, seq, IDX_DIM),
            st(ssm_p, batch, n_ssd_heads, SSD_HEAD_DIM, D_STATE), st(conv_p, batch, conv_w.shape[1] - 1, conv_dim),
            st(k_s, db, 1, N_KV_HEADS, HEAD_DIM), st(v_s, db, 1, N_KV_HEADS, HEAD_DIM), st(ki_s, db, 1, IDX_DIM),
            st(ssm_s, db, n_ssd_heads, SSD_HEAD_DIM, D_STATE), st(conv_s, db, conv_w.shape[1] - 1, conv_dim))
```

```python
import functools
import math

import numpy as np
import jax
import jax.numpy as jnp
from jax import lax
from jax.experimental import pallas as pl
from jax.experimental.pallas import tpu as pltpu

N_HEADS = 16
HEAD_DIM = 64
N_KV_HEADS = 4
N_IDX_HEADS = 8
IDX_DIM = 64
TOPK_MAX = 256
N_BUCKETS = 32
MAX_DISTANCE = 128
SSD_HEAD_DIM = 64
SSD_GROUPS = 4
D_STATE = 128
SSD_CHUNK = 128
EPS = 1e-6
PAGE_SIZE = 128

LANES = 128
SUBLANES = 8
MXU_TILE = 256
VMEM_LIMIT = 56 * 1024 * 1024

F32 = jnp.float32
BF16 = jnp.bfloat16
I32 = jnp.int32
NEG_INF = float("-inf")
INT_MIN = -(2 ** 31)
LOG2E = math.log2(math.e)

_NT = (((1,), (1,)), ((), ()))


def _dot(a, b):
    return jnp.dot(a, b, preferred_element_type=F32)


def _dot_nt(a, b):
    return lax.dot_general(a, b, _NT, preferred_element_type=F32)


def _split3(a):
    hi = a.astype(BF16)
    r = a - hi.astype(F32)
    mid = r.astype(BF16)
    lo = (r - mid.astype(F32)).astype(BF16)
    return hi, mid, lo


def _dot_exact_rhs(a, b_bf16):
    hi, mid, lo = _split3(a)
    return _dot(hi, b_bf16) + _dot(mid, b_bf16) + _dot(lo, b_bf16)


def _dot_exact_lhs(a_bf16, b):
    hi, mid, lo = _split3(b)
    return _dot(a_bf16, hi) + _dot(a_bf16, mid) + _dot(a_bf16, lo)


def _rmsnorm(x, g):
    return (x * lax.rsqrt(jnp.mean(x * x, axis=-1, keepdims=True) + EPS)) * g


def _sigmoid(x):
    return 0.5 * jnp.tanh(0.5 * x) + 0.5


def _silu(x):
    h = 0.5 * x
    return h * jnp.tanh(h) + h


def _params(*sem):
    return pltpu.CompilerParams(dimension_semantics=sem, vmem_limit_bytes=VMEM_LIMIT)


def _t5_bucket_np(dist):
    n = np.maximum(dist, 0)
    max_exact = N_BUCKETS // 2
    nf = np.maximum(n, 1).astype(np.float32)
    val = (np.log(nf / np.float32(max_exact)) / np.float32(math.log(MAX_DISTANCE / max_exact))
           * np.float32(N_BUCKETS - max_exact)).astype(np.float32)
    frac = np.abs(val - np.round(val))
    knife = (frac < 1e-3) & (n > max_exact) & (val < N_BUCKETS - max_exact - 0.5)
    assert not knife.any()
    large = np.minimum(max_exact + val.astype(np.int32), N_BUCKETS - 1)
    return np.where(n < max_exact, n, large).astype(np.int32)


def _ada_kernel(c_ref, w_ref, b_ref, o_ref):
    h = _silu(c_ref[...]).astype(BF16)
    o_ref[...] = _dot(h, w_ref[...].astype(BF16)) + b_ref[...]


def _ada(c, w, b):
    rows, d = c.shape
    n = w.shape[1]
    tn = 1024
    return pl.pallas_call(
        _ada_kernel,
        grid=(n // tn,),
        in_specs=[pl.BlockSpec((rows, d), lambda j: (0, 0)),
                  pl.BlockSpec((d, tn), lambda j: (0, j)),
                  pl.BlockSpec((1, tn), lambda j: (0, j))],
        out_specs=pl.BlockSpec((rows, tn), lambda j: (0, j)),
        out_shape=jax.ShapeDtypeStruct((rows, n), F32),
        compiler_params=_params("arbitrary"),
    )(c, w, b.reshape(1, n))


class _Mod:
    def __init__(self, arr, rows_per_seq):
        self.arr = arr
        self.rows_per_seq = rows_per_seq

    def spec(self, tm):
        r = self.arr.shape[1]
        d = self.arr.shape[2]
        if r == 1:
            per = self.rows_per_seq // tm
            return pl.BlockSpec((1, 1, d), lambda i, *_: (i // per, 0, 0))
        assert r == tm
        return pl.BlockSpec((1, r, d), lambda i, *_: (i, 0, 0))


def _ffn_kernel(x_ref, sh_ref, sc_ref, gt_ref, g_ref, wi_ref, wo_ref, fg_ref, o_ref, *, final_norm, splits):
    x = x_ref[...]
    f = wo_ref.shape[0]
    h = (_rmsnorm(x, g_ref[...]) * (1.0 + sc_ref[0]) + sh_ref[0]).astype(BF16)
    acc = None
    for lo, hi in splits:
        gate = _dot(h, wi_ref[:, lo:hi])
        up = _dot(h, wi_ref[:, f + lo:f + hi])
        part = _dot((_silu(gate) * up).astype(BF16), wo_ref[lo:hi, :])
        acc = part if acc is None else acc + part
    out = x + 0.5 * gt_ref[0] * acc
    if final_norm:
        out = _rmsnorm(out, fg_ref[...])
    o_ref[...] = out


def _ffn(x, shift, scale, gate, norm_g, w_in, w_out, final_g, *, tm, final_norm):
    m, d = x.shape
    f = w_out.shape[0]
    assert f % MXU_TILE == 0 and m % tm == 0
    mid = (f // MXU_TILE // 2) * MXU_TILE
    splits = ((0, mid), (mid, f))
    row = lambda i: (i, 0)
    const = lambda i: (0, 0)
    resident = lambda shape: pl.BlockSpec(shape, const, pipeline_mode=pl.Buffered(1))
    return pl.pallas_call(
        functools.partial(_ffn_kernel, final_norm=final_norm, splits=splits),
        grid=(m // tm,),
        in_specs=[pl.BlockSpec((tm, d), row),
                  shift.spec(tm), scale.spec(tm), gate.spec(tm),
                  pl.BlockSpec((1, d), const),
                  resident(w_in.shape), resident(w_out.shape),
                  pl.BlockSpec((1, d), const)],
        out_specs=pl.BlockSpec((tm, d), row),
        out_shape=jax.ShapeDtypeStruct((m, d), F32),
        compiler_params=_params("parallel"),
    )(x, shift.arr, scale.arr, gate.arr, norm_g, w_in, w_out, final_g)


def _proj_kernel(x_ref, sh_ref, sc_ref, g_ref, w_ref, *o_refs, offsets, scales):
    h = (_rmsnorm(x_ref[...], g_ref[...]) * (1.0 + sc_ref[0]) + sh_ref[0]).astype(BF16)
    for o_ref, off, scale in zip(o_refs, offsets, scales):
        width = o_ref.shape[1]
        out = _dot(h, w_ref[:, off:off + width])
        if scale != 1.0:
            out = out * scale
        o_ref[...] = out.astype(o_ref.dtype)


def _proj(x, shift, scale, norm_g, w, segments, *, tm):
    m, d = x.shape
    offsets, off = [], 0
    for width, _, _ in segments:
        offsets.append(off)
        off += -(-width // LANES) * LANES
    assert off == w.shape[1]
    row = lambda i: (i, 0)
    const = lambda i: (0, 0)
    return pl.pallas_call(
        functools.partial(_proj_kernel, offsets=tuple(offsets), scales=tuple(s for _, _, s in segments)),
        grid=(m // tm,),
        in_specs=[pl.BlockSpec((tm, d), row), shift.spec(tm), scale.spec(tm),
                  pl.BlockSpec((1, d), const), pl.BlockSpec(w.shape, const, pipeline_mode=pl.Buffered(1))],
        out_specs=[pl.BlockSpec((tm, width), row) for width, _, _ in segments],
        out_shape=[jax.ShapeDtypeStruct((m, width), dt) for width, dt, _ in segments],
        compiler_params=_params("parallel"),
    )(x, shift.arr, scale.arr, norm_g, w)


def _sort_key(score):
    score = jnp.where(score == 0.0, 0.0, score)
    bits = pltpu.bitcast(score, I32)
    return jnp.where(bits >= 0, bits, bits ^ jnp.int32(0x7FFFFFFF))


def _radix_threshold(count_ge, n_top, shape):
    def bit_body(t, carry):
        thr_u, cnt_thr = carry
        cand_u = thr_u | jnp.left_shift(jnp.int32(1), 31 - t)
        cnt = count_ge(cand_u ^ jnp.int32(INT_MIN))
        take = cnt >= n_top
        return jnp.where(take, cand_u, thr_u), jnp.where(take, cnt, cnt_thr)

    thr_u, cnt_thr = lax.fori_loop(0, 32, bit_body, (jnp.zeros(shape, I32), jnp.zeros(shape, I32)))
    return thr_u ^ jnp.int32(INT_MIN), cnt_thr


def _bias_table_kernel(rb_ref, bkt_ref, o_ref):
    far = N_BUCKETS - 1
    for slot in range(bkt_ref.shape[0]):
        bkt = bkt_ref[slot]
        for h in range(N_HEADS):
            acc = jnp.zeros(bkt.shape, F32)
            for b in range(N_BUCKETS - 1):
                acc = jnp.where(bkt == b, (rb_ref[b, h] - rb_ref[far, h]) * LOG2E, acc)
            o_ref[slot, h] = acc


def _bias_table(rel_bias, tq, seq):
    s = np.arange(tq)[:, None]
    t = np.arange(tq)[None, :]
    bkt = np.stack([_t5_bucket_np(t + tq - s), _t5_bucket_np(t - s)])
    assert (_t5_bucket_np(np.arange(tq + 1, seq + 1)) == N_BUCKETS - 1).all()
    return pl.pallas_call(
        _bias_table_kernel,
        in_specs=[pl.BlockSpec(memory_space=pltpu.SMEM), pl.BlockSpec(memory_space=pltpu.VMEM)],
        out_specs=pl.BlockSpec(memory_space=pltpu.VMEM),
        out_shape=jax.ShapeDtypeStruct((2, N_HEADS, tq, tq), F32),
    )(rel_bias, jnp.asarray(bkt))


_V_ROWS = HEAD_DIM + 16


def _dsa_prompt_kernel(q_ref, qi_ref, wi_ref, k_ref, v_ref, ki_ref, bias_ref, o_ref,
                       kb_scr, vt_scr, kib_scr, key_scr, mb_scr, m_scr, al_scr, lg_scr, acc_scr, *, n_top, tq):
    i = pl.program_id(1)
    nq = N_HEADS // N_KV_HEADS
    nblk = key_scr.shape[0]
    s_io = lax.broadcasted_iota(I32, (tq, tq), 0)
    t_io = lax.broadcasted_iota(I32, (tq, tq), 1)
    causal = s_io <= t_io

    @pl.when(i == 0)
    def _():
        kb_scr[...] = k_ref[...].astype(BF16)
        kib_scr[...] = ki_ref[...].astype(BF16)
        for c in range(nblk):
            for pair in range(N_KV_HEADS // 2):
                vt = v_ref[c * tq:(c + 1) * tq, pair * 2 * HEAD_DIM:(pair + 1) * 2 * HEAD_DIM].T.astype(BF16)
                vt_scr[c, 2 * pair, 0:HEAD_DIM, :] = vt[0:HEAD_DIM]
                vt_scr[c, 2 * pair + 1, 0:HEAD_DIM, :] = vt[HEAD_DIM:2 * HEAD_DIM]
            for n in range(N_KV_HEADS):
                vt_scr[c, n, HEAD_DIM:_V_ROWS, :] = jnp.ones((_V_ROWS - HEAD_DIM, tq), BF16)

    def chunk(j):
        return pl.ds(pl.multiple_of(j * tq, tq), tq)

    def fold(a):
        return jnp.sum(a.reshape(tq // SUBLANES, SUBLANES, tq), axis=0)

    qi = qi_ref[...]
    qi_st = jnp.concatenate([qi[:, h * IDX_DIM:(h + 1) * IDX_DIM] for h in range(N_IDX_HEADS)], axis=0)
    wi_t = wi_ref[...].T
    wi_row = jnp.concatenate([wi_t[h:h + 1, :] for h in range(N_IDX_HEADS)], axis=1)
    idx_scale = (N_IDX_HEADS * IDX_DIM) ** -0.5

    def visible(j):
        return (j < i) | ((j == i) & causal)

    n_pairs = (i + 2) // 2

    def score_body(jp, carry):
        for j in (2 * jp, 2 * jp + 1):
            d = jnp.maximum(_dot_nt(kib_scr[chunk(j), :], qi_st), 0.0) * wi_row
            s = d[:, 0:tq]
            for h in range(1, N_IDX_HEADS):
                s = s + d[:, h * tq:(h + 1) * tq]
            key_scr[j] = _sort_key(jnp.where(visible(j), s * idx_scale, NEG_INF))
        return carry

    lax.fori_loop(0, n_pairs, score_body, 0)

    def count(pred):
        def body(jp, c):
            return c + fold(pred(key_scr[2 * jp]).astype(I32)) + fold(pred(key_scr[2 * jp + 1]).astype(I32))
        c = lax.fori_loop(0, n_pairs, body, jnp.zeros((SUBLANES, tq), I32))
        return jnp.sum(c, axis=0, keepdims=True)

    thr, cnt_thr = _radix_threshold(lambda cand: count(lambda key: key >= cand), n_top, (1, tq))

    def sel_plain():
        def body(j, carry):
            mb_scr[j] = jnp.where((key_scr[j] >= thr) & visible(j), 0.0, NEG_INF)
            return carry
        lax.fori_loop(0, i + 1, body, 0)

    def sel_ties():
        need = (n_top - count(lambda key: key > thr)).astype(F32)
        tril = (s_io >= t_io).astype(BF16)

        def body(j, run_eq):
            key = key_scr[j]
            eq = key == thr
            pre = _dot(tril, eq.astype(BF16))
            sel = ((key > thr) | (eq & (run_eq + pre <= need))) & visible(j)
            mb_scr[j] = jnp.where(sel, 0.0, NEG_INF)
            return run_eq + pre[tq - 1:tq, :]
        lax.fori_loop(0, i + 1, body, jnp.zeros((1, tq), F32))

    lax.cond(jnp.max(cnt_thr) > n_top, sel_ties, sel_plain)

    q = q_ref[...]
    q_st = [jnp.concatenate([q[:, (n * nq + g) * HEAD_DIM:(n * nq + g + 1) * HEAD_DIM] for g in range(nq)], axis=0)
            for n in range(N_KV_HEADS)]
    m_scr[...] = jnp.full(m_scr.shape, -1e30, F32)
    acc_scr[...] = jnp.zeros_like(acc_scr)

    heads = range(N_KV_HEADS)

    def stage_a(j, slot):
        mb4 = jnp.concatenate([mb_scr[j]] * nq, axis=1)
        lgs = [_dot_nt(kb_scr[chunk(j), n * HEAD_DIM:(n + 1) * HEAD_DIM], q_st[n]) + mb4 for n in heads]
        if slot is not None:
            lgs = [lgs[n] + jnp.concatenate([bias_ref[slot, n * nq + g] for g in range(nq)], axis=1) for n in heads]
        for n in heads:
            m_old = m_scr[n]
            m_new = jnp.maximum(m_old, jnp.max(lgs[n], axis=0, keepdims=True))
            al_scr[n] = jnp.exp2(m_old - m_new)
            m_scr[n] = m_new
            lg_scr[j % 2, n] = lgs[n]

    def stage_b(j):
        ps = [jnp.exp2(lg_scr[j % 2, n] - m_scr[n]).astype(BF16) for n in heads]
        pvs = [_dot(vt_scr[j, n], ps[n]) for n in heads]
        for n in heads:
            acc_scr[n] = al_scr[n] * acc_scr[n] + pvs[n]

    @pl.when(i >= 2)
    def _():
        stage_a(0, None)

        def far_body(j, carry):
            stage_b(j - 1)
            stage_a(j, None)
            return carry

        lax.fori_loop(1, i - 1, far_body, 0)
        stage_b(i - 2)
        stage_a(i - 1, 0)

    @pl.when(i == 1)
    def _():
        stage_a(0, 0)

    @pl.when(i >= 1)
    def _():
        stage_b(i - 1)
        stage_a(i, 1)

    @pl.when(i == 0)
    def _():
        stage_a(0, 1)

    stage_b(i)

    for n in range(N_KV_HEADS):
        acc = acc_scr[n]
        out_t = acc[0:HEAD_DIM] / acc[HEAD_DIM:HEAD_DIM + 1]
        for pair in range(nq // 2):
            g0 = 2 * pair
            two = jnp.concatenate([out_t[:, g0 * tq:(g0 + 1) * tq], out_t[:, (g0 + 1) * tq:(g0 + 2) * tq]], axis=0)
            h0 = n * nq + g0
            o_ref[:, h0 * HEAD_DIM:(h0 + 2) * HEAD_DIM] = two.T.astype(o_ref.dtype)


def _dsa_prompt(q, qi, wi, k, v, ki, bias_tab, *, batch, seq, tq):
    nblk = seq // tq
    n_top = min(TOPK_MAX, seq // 4)
    nq = N_HEADS // N_KV_HEADS
    assert tq == LANES and nq % 2 == 0 and N_KV_HEADS % 2 == 0
    blk = lambda b, i: (b * nblk + i, 0)
    whole = lambda b, i: (b, 0)
    kvw = N_KV_HEADS * HEAD_DIM
    return pl.pallas_call(
        functools.partial(_dsa_prompt_kernel, n_top=n_top, tq=tq),
        grid=(batch, nblk),
        in_specs=[pl.BlockSpec((tq, N_HEADS * HEAD_DIM), blk),
                  pl.BlockSpec((tq, N_IDX_HEADS * IDX_DIM), blk),
                  pl.BlockSpec((tq, LANES), blk),
                  pl.BlockSpec((seq, kvw), whole),
                  pl.BlockSpec((seq, kvw), whole),
                  pl.BlockSpec((seq, IDX_DIM), whole),
                  pl.BlockSpec(bias_tab.shape, lambda b, i: (0, 0, 0, 0))],
        out_specs=pl.BlockSpec((tq, N_HEADS * HEAD_DIM), blk),
        out_shape=jax.ShapeDtypeStruct((batch * seq, N_HEADS * HEAD_DIM), BF16),
        scratch_shapes=[pltpu.VMEM((seq, kvw), BF16),
                        pltpu.VMEM((nblk, N_KV_HEADS, _V_ROWS, tq), BF16),
                        pltpu.VMEM((seq, IDX_DIM), BF16),
                        pltpu.VMEM((nblk, tq, tq), I32), pltpu.VMEM((nblk, tq, tq), F32),
                        pltpu.VMEM((N_KV_HEADS, 1, nq * tq), F32), pltpu.VMEM((N_KV_HEADS, 1, nq * tq), F32),
                        pltpu.VMEM((2, N_KV_HEADS, tq, nq * tq), F32),
                        pltpu.VMEM((N_KV_HEADS, _V_ROWS, nq * tq), F32)],
        compiler_params=_params("arbitrary", "arbitrary"),
    )(q, qi, wi, k, v, ki, bias_tab)


def _dsa_s_score_kernel(pt_ref, qi_ref, wi_ref, *refs, pg):
    page_refs, o_ref = refs[:pg], refs[pg]
    qi = qi_ref[0].astype(BF16)
    wi = wi_ref[0]
    idx_scale = (N_IDX_HEADS * IDX_DIM) ** -0.5
    for r in range(pg):
        d = _dot_nt(qi, page_refs[r][0])
        o_ref[r, 0] = jnp.sum(jnp.maximum(d, 0.0) * wi, axis=0, keepdims=True) * idx_scale


def _dsa_s_scores(page_table, qi, wi_bc, cache_kidx, *, pg):
    db, n_pages = page_table.shape
    page_spec = lambda r: pl.BlockSpec(
        (1, PAGE_SIZE, IDX_DIM), lambda b, s, pt: (pt[b * n_pages + s * pg + r], 0, 0))
    grid_spec = pltpu.PrefetchScalarGridSpec(
        num_scalar_prefetch=1,
        grid=(db, n_pages // pg),
        in_specs=[pl.BlockSpec((1, N_IDX_HEADS, IDX_DIM), lambda b, s, pt: (b, 0, 0)),
                  pl.BlockSpec((1, N_IDX_HEADS, LANES), lambda b, s, pt: (b, 0, 0))]
                 + [page_spec(r) for r in range(pg)],
        out_specs=pl.BlockSpec((pg, 1, 1, PAGE_SIZE), lambda b, s, pt: (s, b, 0, 0)),
    )
    return pl.pallas_call(
        functools.partial(_dsa_s_score_kernel, pg=pg),
        grid_spec=grid_spec,
        out_shape=jax.ShapeDtypeStruct((n_pages, db, 1, PAGE_SIZE), F32),
        compiler_params=_params("arbitrary", "arbitrary"),
    )(page_table.reshape(-1), qi, wi_bc, *([cache_kidx] * pg))


def _dsa_s_select_kernel(sc_ref, qi_ref, kis_ref, wi_ref, hsum_ref, o_ref, key_scr, *, n_top):
    n_pages, db, _ = sc_ref.shape
    idx_scale = (N_IDX_HEADS * IDX_DIM) ** -0.5

    prod = (qi_ref[...].astype(BF16).astype(F32)
            * jnp.concatenate([kis_ref[...].astype(BF16).astype(F32)] * N_IDX_HEADS, axis=1))
    d_self = _dot_exact_rhs(prod, hsum_ref[...])
    s_self = jnp.sum((jnp.maximum(d_self, 0.0) * wi_ref[...]).T, axis=0, keepdims=True) * idx_scale
    key_self = _sort_key(s_self)

    def to_keys(r, carry):
        key_scr[r] = _sort_key(sc_ref[r].T)
        return carry
    lax.fori_loop(0, n_pages, to_keys, 0)

    def fold(a):
        return jnp.sum(a.reshape(PAGE_SIZE // SUBLANES, SUBLANES, db), axis=0)

    def count(pred):
        def body(rp, c):
            return c + fold(pred(key_scr[2 * rp]).astype(I32)) + fold(pred(key_scr[2 * rp + 1]).astype(I32))
        c = lax.fori_loop(0, n_pages // 2, body, jnp.zeros((SUBLANES, db), I32))
        return jnp.sum(c, axis=0, keepdims=True) + pred(key_self).astype(I32)

    thr, cnt_thr = _radix_threshold(lambda cand: count(lambda key: key >= cand), n_top, (1, db))

    def emit(r, sel):
        o_ref[r] = jnp.where(sel, 0.0, NEG_INF).T

    def sel_plain():
        def body(r, carry):
            emit(r, key_scr[r] >= thr)
            return carry
        lax.fori_loop(0, n_pages, body, 0)
        emit(n_pages, jnp.broadcast_to(key_self >= thr, (PAGE_SIZE, db)))

    def sel_ties():
        need = (n_top - count(lambda key: key > thr)).astype(F32)
        r_io = lax.broadcasted_iota(I32, (PAGE_SIZE, PAGE_SIZE), 0)
        c_io = lax.broadcasted_iota(I32, (PAGE_SIZE, PAGE_SIZE), 1)
        tril = (r_io >= c_io).astype(BF16)

        def body(r, run_eq):
            key = key_scr[r]
            eq = key == thr
            pre = _dot(tril, eq.astype(BF16))
            emit(r, (key > thr) | (eq & (run_eq + pre <= need)))
            return run_eq + pre[PAGE_SIZE - 1:PAGE_SIZE, :]
        run_eq = lax.fori_loop(0, n_pages, body, jnp.zeros((1, db), F32))
        sel_self = (key_self > thr) | ((key_self == thr) & (run_eq + 1.0 <= need))
        emit(n_pages, jnp.broadcast_to(sel_self, (PAGE_SIZE, db)))

    lax.cond(jnp.max(cnt_thr) > n_top, sel_ties, sel_plain)


def _dsa_s_select(scores, qi, ki_s, wi, *, n_top):
    n_pages, db, _ = scores.shape
    assert db == LANES and n_pages % 2 == 0
    hsum = np.zeros((N_IDX_HEADS * IDX_DIM, LANES), np.float32)
    hsum[np.arange(N_IDX_HEADS * IDX_DIM), np.arange(N_IDX_HEADS * IDX_DIM) // IDX_DIM] = 1.0
    return pl.pallas_call(
        functools.partial(_dsa_s_select_kernel, n_top=n_top),
        out_shape=jax.ShapeDtypeStruct((n_pages + 1, db, PAGE_SIZE), F32),
        scratch_shapes=[pltpu.VMEM((n_pages, PAGE_SIZE, db), I32)],
        compiler_params=pltpu.CompilerParams(vmem_limit_bytes=VMEM_LIMIT),
    )(scores, qi, ki_s, wi, jnp.asarray(hsum, BF16))


def _bias_table_s_kernel(rbt_ref, bkt_ref, o_ref):
    rbt = rbt_ref[...] * LOG2E
    for r in range(bkt_ref.shape[0]):
        bkt = bkt_ref[r:r + 1, :]
        out = jnp.zeros((N_HEADS, bkt.shape[1]), F32)
        for b in range(N_BUCKETS):
            out = jnp.where(bkt == b, rbt[:, b:b + 1], out)
        o_ref[r] = out


def _bias_table_s(rel_bias, n_pages):
    past = n_pages * PAGE_SIZE
    pos = np.arange((n_pages + 1) * PAGE_SIZE).reshape(n_pages + 1, PAGE_SIZE)
    pos[n_pages] = past
    return pl.pallas_call(
        _bias_table_s_kernel,
        out_shape=jax.ShapeDtypeStruct((n_pages + 1, N_HEADS, PAGE_SIZE), F32),
    )(rel_bias.T, jnp.asarray(_t5_bucket_np(past - pos)))


def _dsa_s_attend_kernel(pt_ref, mb_ref, bias_ref, q_ref, ks_ref, vs_ref, *refs, pg, n_pages):
    k_refs, v_refs = refs[:pg], refs[pg:2 * pg]
    o_ref, m_scr, l_scr, acc_scr = refs[2 * pg:]
    s = pl.program_id(1)
    nq = N_HEADS // N_KV_HEADS
    kvw = N_KV_HEADS * HEAD_DIM

    @pl.when(s == 0)
    def _():
        m_scr[...] = jnp.full(m_scr.shape, -1e30, F32)
        l_scr[...] = jnp.zeros_like(l_scr)
        acc_scr[...] = jnp.zeros_like(acc_scr)

    q = q_ref[0]
    h_io = lax.broadcasted_iota(I32, (N_HEADS, kvw), 0)
    c_io = lax.broadcasted_iota(I32, (N_HEADS, kvw), 1)
    band = (c_io // HEAD_DIM) == (h_io // nq)
    q_bd = jnp.where(band, jnp.concatenate([q] * N_KV_HEADS, axis=1), 0.0)
    logits = []
    for r in range(pg):
        page = s * pg + r
        logits.append(_dot_nt(q_bd, k_refs[r][0].astype(BF16)) + bias_ref[page] + mb_ref[page, 0])
    m_old = m_scr[...]
    m_new = m_old
    for lg in logits:
        m_new = jnp.maximum(m_new, jnp.max(lg, axis=1, keepdims=True))
    alpha = jnp.exp2(m_old - m_new)
    l_new = alpha * l_scr[...]
    acc = alpha * acc_scr[...]
    for r, lg in enumerate(logits):
        p = jnp.exp2(lg - m_new)
        l_new = l_new + jnp.sum(p, axis=1, keepdims=True)
        acc = acc + _dot(p.astype(BF16), v_refs[r][0].astype(BF16))
    m_scr[...] = m_new
    l_scr[...] = l_new
    acc_scr[...] = acc

    @pl.when(s == pl.num_programs(1) - 1)
    def _():
        ks = ks_ref[0].astype(BF16).astype(F32)
        lg = jnp.sum(q_bd.astype(F32) * ks, axis=1, keepdims=True)
        lg = lg + bias_ref[n_pages][:, 0:1] + mb_ref[n_pages, 0][:, 0:1]
        m_fin = jnp.maximum(m_new, lg)
        a = jnp.exp2(m_new - m_fin)
        p = jnp.exp2(lg - m_fin)
        l_fin = a * l_new + p
        out = (a * acc + p.astype(BF16).astype(F32) * vs_ref[0].astype(BF16).astype(F32)) / l_fin
        out = jnp.where(band, out, 0.0)
        res = out[:, 0:HEAD_DIM]
        for n in range(1, N_KV_HEADS):
            res = res + out[:, n * HEAD_DIM:(n + 1) * HEAD_DIM]
        o_ref[0] = res.astype(o_ref.dtype)


def _dsa_s_attend(page_table, mask, bias_tab, q, k_s, v_s, cache_k, cache_v, *, pg):
    db, n_pages = page_table.shape
    kvw = N_KV_HEADS * HEAD_DIM
    width = PAGE_SIZE
    per_seq = lambda *shape: pl.BlockSpec((1,) + shape, lambda b, s, pt: (b,) + (0,) * len(shape))
    page_spec = lambda r: pl.BlockSpec(
        (1, PAGE_SIZE, kvw), lambda b, s, pt: (pt[b * n_pages + s * pg + r], 0, 0))
    grid_spec = pltpu.PrefetchScalarGridSpec(
        num_scalar_prefetch=1,
        grid=(db, n_pages // pg),
        in_specs=[pl.BlockSpec((n_pages + 1, 1, 1, width), lambda b, s, pt: (0, b, 0, 0)),
                  pl.BlockSpec(bias_tab.shape, lambda b, s, pt: (0, 0, 0)),
                  per_seq(N_HEADS, HEAD_DIM), per_seq(1, kvw), per_seq(1, kvw)]
                 + [page_spec(r) for r in range(pg)] * 2,
        out_specs=per_seq(N_HEADS, HEAD_DIM),
        scratch_shapes=[pltpu.VMEM((N_HEADS, 1), F32), pltpu.VMEM((N_HEADS, 1), F32),
                        pltpu.VMEM((N_HEADS, kvw), F32)],
    )
    return pl.pallas_call(
        functools.partial(_dsa_s_attend_kernel, pg=pg, n_pages=n_pages),
        grid_spec=grid_spec,
        out_shape=jax.ShapeDtypeStruct((db, N_HEADS, HEAD_DIM), BF16),
        compiler_params=_params("arbitrary", "arbitrary"),
    )(page_table.reshape(-1), mask.reshape(n_pages + 1, db, 1, width), bias_tab, q, k_s, v_s,
      *([cache_k] * pg), *([cache_v] * pg))


def _ssd_kernel(xbc_ref, z_ref, dt_ref, cw_ref, cb_ref, alog_ref, dtb_ref, dsk_ref, ng_ref, e_ref, *refs,
                rows, has_init, d_inner):
    if has_init:
        conv0_ref, ssm0_ref = refs[:2]
        refs = refs[2:]
    y_ref, ssm_ref, conv_ref, ext_scr, ht_scr, y_scr = refs
    c = pl.program_id(1)
    cl = SSD_CHUNK
    gw = d_inner // SSD_GROUPS
    hpg = gw // SSD_HEAD_DIM
    n_heads = d_inner // SSD_HEAD_DIM
    conv_dim = d_inner + 2 * SSD_GROUPS * D_STATE
    conv_w = cw_ref.shape[0]
    top = SUBLANES
    row_io = lax.broadcasted_iota(I32, (cl, 1), 0)

    def padded(ref):
        a = ref[0]
        if rows == cl:
            return a
        return jnp.where(row_io < rows, jnp.broadcast_to(a, (cl, a.shape[1])), 0.0)

    @pl.when(c == 0)
    def _():
        ext_scr[0:top, :] = jnp.zeros((top, conv_dim), F32)
        if has_init:
            ext_scr[top - conv_w + 1:top, :] = conv0_ref[0]
            for g in range(SSD_GROUPS):
                ht_scr[g] = ssm0_ref[0, g * gw:(g + 1) * gw, :].T
        else:
            ht_scr[...] = jnp.zeros_like(ht_scr)

    ext_scr[top:top + cl, :] = padded(xbc_ref)
    conv_ref[0] = ext_scr[top + rows - conv_w + 1:top + rows, :]

    cblk = 512
    for cb in range(conv_dim // cblk):
        sl = slice(cb * cblk, (cb + 1) * cblk)
        acc = jnp.broadcast_to(cb_ref[:, sl], (cl, cblk))
        for w in range(conv_w):
            acc = acc + ext_scr[top - conv_w + 1 + w:top - conv_w + 1 + w + cl, sl] * cw_ref[w:w + 1, sl]
        y_scr[:, sl] = _silu(acc)
    ext_scr[top - conv_w + 1:top, :] = ext_scr[top + cl - conv_w + 1:top + cl, :]

    dt = jax.nn.softplus(padded(dt_ref) + dtb_ref[...])
    if rows < cl:
        dt = jnp.where(row_io < rows, dt, 0.0)
    a_neg2 = -jnp.exp(alog_ref[...]) * LOG2E
    r_io = lax.broadcasted_iota(I32, (cl, cl), 0)
    c_io = lax.broadcasted_iota(I32, (cl, cl), 1)
    tril = r_io >= c_io
    acs = _dot_exact_lhs(tril.astype(BF16), dt * a_neg2)
    acs_t = acs.T
    acs_last = acs[cl - 1:cl, :]
    stacked = jnp.concatenate([dt, jnp.exp2(acs), jnp.exp2(acs_last - acs)], axis=0)
    hi = stacked.astype(BF16)
    lo = (stacked - hi.astype(F32)).astype(BF16)
    expanded = _dot(hi, e_ref[...]) + _dot(lo, e_ref[...])
    dt_x, ea_x, te_x = expanded[0:cl], expanded[cl:2 * cl], expanded[2 * cl:3 * cl]

    z = padded(z_ref)
    for g in range(SSD_GROUPS):
        gs = slice(g * gw, (g + 1) * gw)
        x_g = y_scr[:, gs]
        b_g = y_scr[:, d_inner + g * D_STATE:d_inner + (g + 1) * D_STATE]
        c_g = y_scr[:, d_inner + (SSD_GROUPS + g) * D_STATE:d_inner + (SSD_GROUPS + g + 1) * D_STATE]
        c16 = c_g.astype(BF16)
        cbm = _dot_nt(c16, b_g.astype(BF16))
        xdt = x_g * dt_x[:, gs]
        xdt16 = xdt.astype(BF16)
        ht = ht_scr[g]
        y_g = _dot(c16, ht.astype(BF16)) * ea_x[:, gs] + dsk_ref[:, gs] * x_g
        ht_scr[g] = ht * ea_x[cl - 1:cl, gs] + _dot(b_g.T.astype(BF16), (xdt * te_x[:, gs]).astype(BF16))
        diag = []
        for r in range(hpg):
            h = g * hpg + r
            seg = acs[:, h:h + 1] - acs_t[h:h + 1, :]
            m = (cbm * jnp.exp2(jnp.where(tril, seg, NEG_INF))).astype(BF16)
            diag.append(_dot(m, xdt16[:, r * SSD_HEAD_DIM:(r + 1) * SSD_HEAD_DIM]))
        y_g = (y_g + jnp.concatenate(diag, axis=1)) * _silu(z[:, gs])
        y_g = y_g * lax.rsqrt(jnp.mean(y_g * y_g, axis=1, keepdims=True) + EPS) * ng_ref[:, gs]
        y_ref[0, :, gs] = y_g[0:rows].astype(y_ref.dtype)

    @pl.when(c == pl.num_programs(1) - 1)
    def _():
        for g in range(SSD_GROUPS):
            ssm_ref[0, g * gw:(g + 1) * gw, :] = ht_scr[g].T


def _ssd(xbc, z, dt, conv_w, conv_b, a_log, dt_bias, d_skip, norm_g, *, n_seq, n_chunks, rows,
         conv0=None, ssm0=None):
    conv_dim = xbc.shape[-1]
    d_inner = z.shape[-1]
    n_heads = d_inner // SSD_HEAD_DIM
    cw = conv_w.shape[0]
    has_init = conv0 is not None
    pad = lambda a: jnp.pad(a, (0, LANES - a.shape[0])).reshape(1, LANES)
    expand = np.zeros((LANES, d_inner), np.float32)
    expand[np.arange(d_inner) // SSD_HEAD_DIM, np.arange(d_inner)] = 1.0
    step = lambda b, c: (b * n_chunks + c, 0, 0)
    seq = lambda b, c: (b, 0, 0)
    const = lambda b, c: (0, 0)
    in_specs = [pl.BlockSpec((1, rows, conv_dim), step), pl.BlockSpec((1, rows, d_inner), step),
                pl.BlockSpec((1, rows, LANES), step),
                pl.BlockSpec((cw, conv_dim), const), pl.BlockSpec((1, conv_dim), const),
                pl.BlockSpec((1, LANES), const), pl.BlockSpec((1, LANES), const),
                pl.BlockSpec((1, d_inner), const), pl.BlockSpec((1, d_inner), const),
                pl.BlockSpec((LANES, d_inner), const)]
    args = [xbc, z, dt, conv_w, conv_b.reshape(1, conv_dim), pad(a_log), pad(dt_bias),
            jnp.repeat(d_skip, SSD_HEAD_DIM).reshape(1, d_inner), norm_g.reshape(1, d_inner),
            jnp.asarray(expand, BF16)]
    if has_init:
        in_specs += [pl.BlockSpec((1, cw - 1, conv_dim), seq), pl.BlockSpec((1, d_inner, D_STATE), seq)]
        args += [conv0, ssm0]
    return pl.pallas_call(
        functools.partial(_ssd_kernel, rows=rows, has_init=has_init, d_inner=d_inner),
        grid=(n_seq, n_chunks),
        in_specs=in_specs,
        out_specs=[pl.BlockSpec((1, rows, d_inner), step), pl.BlockSpec((1, d_inner, D_STATE), seq),
                   pl.BlockSpec((1, cw - 1, conv_dim), seq)],
        out_shape=[jax.ShapeDtypeStruct((n_seq * n_chunks, rows, d_inner), BF16),
                   jax.ShapeDtypeStruct((n_seq, d_inner, D_STATE), F32),
                   jax.ShapeDtypeStruct((n_seq, cw - 1, conv_dim), F32)],
        scratch_shapes=[pltpu.VMEM((SUBLANES + SSD_CHUNK, conv_dim), F32),
                        pltpu.VMEM((SSD_GROUPS, D_STATE, d_inner // SSD_GROUPS), F32),
                        pltpu.VMEM((SSD_CHUNK, conv_dim), F32)],
        compiler_params=_params("arbitrary", "arbitrary"),
    )(*args)


def _merge_kernel(x_ref, sh_ref, sc_ref, gt_ref, g_ref, att_ref, ssd_ref, wg_ref, wa_ref, ws_ref, wo_ref, o_ref):
    x = x_ref[...]
    d = x.shape[1]
    h = (_rmsnorm(x, g_ref[...]) * (1.0 + sc_ref[0]) + sh_ref[0]).astype(BF16)
    gates = _dot(h, wg_ref[...])
    merged = (_sigmoid(gates[:, 0:d]) * _dot(att_ref[...], wa_ref[...])
              + _sigmoid(gates[:, d:2 * d]) * _dot(ssd_ref[...], ws_ref[...]))
    o_ref[...] = x + gt_ref[0] * _dot(merged.astype(BF16), wo_ref[...])


def _merge(x, shift, scale, gate, norm_g, att, ssd_y, w_g, w_a, w_s, w_o, *, tm):
    m, d = x.shape
    row = lambda i: (i, 0)
    const = lambda i: (0, 0)
    resident = lambda shape: pl.BlockSpec(shape, const, pipeline_mode=pl.Buffered(1))
    return pl.pallas_call(
        _merge_kernel,
        grid=(m // tm,),
        in_specs=[pl.BlockSpec((tm, d), row), shift.spec(tm), scale.spec(tm), gate.spec(tm),
                  pl.BlockSpec((1, d), const),
                  pl.BlockSpec((tm, att.shape[1]), row), pl.BlockSpec((tm, ssd_y.shape[1]), row),
                  resident(w_g.shape), resident(w_a.shape), resident(w_s.shape), resident(w_o.shape)],
        out_specs=pl.BlockSpec((tm, d), row),
        out_shape=jax.ShapeDtypeStruct((m, d), F32),
        compiler_params=_params("parallel"),
    )(x, shift.arr, scale.arr, gate.arr, norm_g, att, ssd_y, w_g, w_a, w_s, w_o)


def _pad_cols(w, width):
    return jnp.pad(w, ((0, 0), (0, width - w.shape[1])))


def _trunk(x, mods, rows_per_seq, p, tm, attend, ssd_fn, final_g):
    sh1, sc1, g1, sh2, sc2, g2, sh3, sc3, g3 = [_Mod(a, rows_per_seq) for a in mods]
    x = _ffn(x, sh1, sc1, g1, p["norm_ffn1"], p["w_ffn1_in"], p["w_ffn1_out"], final_g, tm=tm, final_norm=False)
    q, k, v, qi, ki, wi = _proj(x, sh2, sc2, p["norm_mix"], p["w_att"], p["seg_att"], tm=tm)
    z, xbc, dt = _proj(x, sh2, sc2, p["norm_mix"], p["w_ssd"], p["seg_ssd"], tm=tm)
    att = attend(q, k, v, qi, ki, wi)
    ssd_y, ssm_new, conv_new = ssd_fn(z, xbc, dt)
    x = _merge(x, sh2, sc2, g2, p["norm_mix"], att, ssd_y, p["w_gate"], p["w_attn_out"], p["w_ssd_out"], p["w_out"],
               tm=tm)
    y = _ffn(x, sh3, sc3, g3, p["norm_ffn2"], p["w_ffn2_in"], p["w_ffn2_out"], final_g, tm=tm, final_norm=True)
    return y, (k, v, ki, ssm_new, conv_new)


def kernel(x_prompt, x_sample, c_prompt, c_sample, cache_k, cache_v, cache_kidx, state_ssm, state_conv, page_table,
           w_ada, b_ada, norm_ffn1, w_ffn1_in, w_ffn1_out, norm_mix, w_in, rel_bias, conv_w, conv_b, a_log, dt_bias,
           d_skip, norm_ssd, w_attn_out, w_ssd_out, w_out, norm_ffn2, w_ffn2_in, w_ffn2_out, norm_final):
    depth = w_ada.shape[0]
    assert depth == 1
    batch, seq, d = x_prompt.shape
    db, dec_seq, _ = x_sample.shape
    assert dec_seq == 1
    n_pool = cache_k.shape[1]
    n_pages = page_table.shape[1]
    d_inner = norm_ssd.shape[1]
    conv_dim = conv_w.shape[2]
    n_ssd_heads = d_inner // SSD_HEAD_DIM
    att_q = N_HEADS * HEAD_DIM
    att_kv = N_KV_HEADS * HEAD_DIM
    idx_q = N_IDX_HEADS * IDX_DIM
    l = 0

    widths = (att_q, att_kv, att_kv, idx_q, IDX_DIM, N_IDX_HEADS, d_inner, conv_dim, n_ssd_heads, d, d)
    bounds = np.concatenate([[0], np.cumsum(widths)])
    assert bounds[-1] == w_in.shape[2]
    cols = [w_in[l][:, bounds[i]:bounds[i + 1]].astype(BF16) for i in range(len(widths))]
    w_q, w_k, w_v, w_qi, w_ki, w_wi, w_z, w_xbc, w_dt, w_ga, w_gs = cols
    row1 = lambda a: a.reshape(1, -1)
    p = {
        "norm_ffn1": row1(norm_ffn1[l]), "w_ffn1_in": w_ffn1_in[l].astype(BF16), "w_ffn1_out": w_ffn1_out[l].astype(BF16),
        "norm_mix": row1(norm_mix[l]),
        "w_att": jnp.concatenate([w_q, w_k, w_v, w_qi, _pad_cols(w_ki, LANES), _pad_cols(w_wi, LANES)], axis=1),
        "seg_att": [(att_q, BF16, HEAD_DIM ** -0.5 * LOG2E), (att_kv, F32, 1.0), (att_kv, F32, 1.0),
                    (idx_q, BF16, 1.0), (IDX_DIM, F32, 1.0), (LANES, F32, 1.0)],
        "w_ssd": jnp.concatenate([w_z, w_xbc, _pad_cols(w_dt, LANES)], axis=1),
        "seg_ssd": [(d_inner, F32, 1.0), (conv_dim, F32, 1.0), (LANES, F32, 1.0)],
        "w_gate": jnp.concatenate([w_ga, w_gs], axis=1),
        "w_attn_out": w_attn_out[l].astype(BF16), "w_ssd_out": w_ssd_out[l].astype(BF16), "w_out": w_out[l].astype(BF16),
        "norm_ffn2": row1(norm_ffn2[l]), "w_ffn2_in": w_ffn2_in[l].astype(BF16), "w_ffn2_out": w_ffn2_out[l].astype(BF16),
    }
    final_g = row1(norm_final)
    ssd_args = (conv_w[l], conv_b[l], a_log[l], dt_bias[l], d_skip[l], norm_ssd[l])

    ada = _ada(jnp.concatenate([c_prompt, c_sample], axis=0), w_ada[l], b_ada[l])
    ada_p = [a.reshape(batch, 1, d) for a in jnp.split(ada[:batch], 9, axis=1)]
    ada_s = [a.reshape(1, db, d) for a in jnp.split(ada[batch:], 9, axis=1)]

    tq = 128
    n_chunks = seq // SSD_CHUNK
    bias_tab = _bias_table(rel_bias, tq, seq)

    def attend_p(q, k, v, qi, ki, wi):
        return _dsa_prompt(q, qi, wi, k, v, ki, bias_tab, batch=batch, seq=seq, tq=tq)

    def ssd_p(z, xbc, dt):
        r3 = lambda a: a.reshape(batch * n_chunks, SSD_CHUNK, a.shape[-1])
        y, ssm, conv = _ssd(r3(xbc), r3(z), r3(dt), *ssd_args, n_seq=batch, n_chunks=n_chunks, rows=SSD_CHUNK)
        return y.reshape(batch * seq, d_inner), ssm, conv

    yp, (k_p, v_p, ki_p, ssm_p, conv_p) = _trunk(x_prompt.reshape(batch * seq, d), ada_p, seq, p, 512,
                                                 attend_p, ssd_p, final_g)

    pg = min(16, n_pages)
    n_top_s = min(TOPK_MAX, (n_pages * PAGE_SIZE + 1) // 4)
    bias_tab_s = _bias_table_s(rel_bias, n_pages)

    def attend_s(q, k, v, qi, ki, wi):
        wi_bc = jnp.broadcast_to(wi[:, :N_IDX_HEADS, None], (db, N_IDX_HEADS, LANES))
        scores = _dsa_s_scores(page_table, qi.reshape(db, N_IDX_HEADS, IDX_DIM), wi_bc,
                               cache_kidx[l].astype(BF16), pg=min(2 * pg, n_pages))
        mask = _dsa_s_select(scores.reshape(n_pages, db, PAGE_SIZE), qi, ki, wi, n_top=n_top_s)
        att = _dsa_s_attend(page_table, mask, bias_tab_s, q.reshape(db, N_HEADS, HEAD_DIM),
                            k.reshape(db, 1, att_kv), v.reshape(db, 1, att_kv),
                            cache_k[l].reshape(n_pool, PAGE_SIZE, att_kv),
                            cache_v[l].reshape(n_pool, PAGE_SIZE, att_kv), pg=pg)
        return att.reshape(db, att_q)

    def ssd_s(z, xbc, dt):
        r3 = lambda a: a.reshape(db, 1, a.shape[-1])
        y, ssm, conv = _ssd(r3(xbc), r3(z), r3(dt), *ssd_args, n_seq=db, n_chunks=1, rows=1,
                            conv0=state_conv[l], ssm0=state_ssm[l].reshape(db, d_inner, D_STATE))
        return y.reshape(db, d_inner), ssm, conv

    ys, (k_s, v_s, ki_s, ssm_s, conv_s) = _trunk(x_sample.reshape(db, d), ada_s, db, p, db,
                                                 attend_s, ssd_s, final_g)

    st = lambda a, *shape: a.reshape((1,) + shape)
    return (yp.reshape(batch, seq, d), ys.reshape(db, 1, d),
            st(k_p, batch, seq, N_KV_HEADS, HEAD_DIM), st(v_p, batch, seq, N_KV_HEADS, HEAD_DIM),
            st(ki_p, batch, seq, IDX_DIM),
            st(ssm_p, batch, n_ssd_heads, SSD_HEAD_DIM, D_STATE), st(conv_p, batch, conv_w.shape[1] - 1, conv_dim),
            st(k_s, db, 1, N_KV_HEADS, HEAD_DIM), st(v_s, db, 1, N_KV_HEADS, HEAD_DIM), st(ki_s, db, 1, IDX_DIM),
            st(ssm_s, db, n_ssd_heads, SSD_HEAD_DIM, D_STATE), st(conv_s, db, conv_w.shape[1] - 1, conv_dim))
```

```python
import functools
import math

import numpy as np
import jax
import jax.numpy as jnp
from jax import lax
from jax.experimental import pallas as pl
from jax.experimental.pallas import tpu as pltpu

N_HEADS = 16
HEAD_DIM = 64
N_KV_HEADS = 4
N_IDX_HEADS = 8
IDX_DIM = 64
TOPK_MAX = 256
N_BUCKETS = 32
MAX_DISTANCE = 128
SSD_HEAD_DIM = 64
SSD_GROUPS = 4
D_STATE = 128
SSD_CHUNK = 128
EPS = 1e-6
PAGE_SIZE = 128

LANES = 128
SUBLANES = 8
MXU_TILE = 256
VMEM_LIMIT = 56 * 1024 * 1024

F32 = jnp.float32
BF16 = jnp.bfloat16
I32 = jnp.int32
I16 = jnp.int16
NEG_INF = float("-inf")
INT_MIN = -(2 ** 31)
LOG2E = math.log2(math.e)

_NT = (((1,), (1,)), ((), ()))


def _dot(a, b):
    return jnp.dot(a, b, preferred_element_type=F32)


def _dot_nt(a, b):
    return lax.dot_general(a, b, _NT, preferred_element_type=F32)


def _split3(a):
    hi = a.astype(BF16)
    r = a - hi.astype(F32)
    mid = r.astype(BF16)
    lo = (r - mid.astype(F32)).astype(BF16)
    return hi, mid, lo


def _dot_exact_rhs(a, b_bf16):
    hi, mid, lo = _split3(a)
    return _dot(hi, b_bf16) + _dot(mid, b_bf16) + _dot(lo, b_bf16)


def _dot_exact_lhs(a_bf16, b):
    hi, mid, lo = _split3(b)
    return _dot(a_bf16, hi) + _dot(a_bf16, mid) + _dot(a_bf16, lo)


def _rmsnorm(x, g):
    return (x * lax.rsqrt(jnp.mean(x * x, axis=-1, keepdims=True) + EPS)) * g


def _sigmoid(x):
    return 0.5 * jnp.tanh(0.5 * x) + 0.5


def _silu(x):
    h = 0.5 * x
    return h * jnp.tanh(h) + h


def _params(*sem):
    return pltpu.CompilerParams(dimension_semantics=sem, vmem_limit_bytes=VMEM_LIMIT)


def _t5_bucket_np(dist):
    n = np.maximum(dist, 0)
    max_exact = N_BUCKETS // 2
    nf = np.maximum(n, 1).astype(np.float32)
    val = (np.log(nf / np.float32(max_exact)) / np.float32(math.log(MAX_DISTANCE / max_exact))
           * np.float32(N_BUCKETS - max_exact)).astype(np.float32)
    frac = np.abs(val - np.round(val))
    knife = (frac < 1e-3) & (n > max_exact) & (val < N_BUCKETS - max_exact - 0.5)
    assert not knife.any()
    large = np.minimum(max_exact + val.astype(np.int32), N_BUCKETS - 1)
    return np.where(n < max_exact, n, large).astype(np.int32)


def _ada_kernel(c_ref, w_ref, b_ref, o_ref):
    h = _silu(c_ref[...]).astype(BF16)
    o_ref[...] = _dot(h, w_ref[...].astype(BF16)) + b_ref[...]


def _ada(c, w, b):
    rows, d = c.shape
    n = w.shape[1]
    tn = 1024
    return pl.pallas_call(
        _ada_kernel,
        grid=(n // tn,),
        in_specs=[pl.BlockSpec((rows, d), lambda j: (0, 0)),
                  pl.BlockSpec((d, tn), lambda j: (0, j)),
                  pl.BlockSpec((1, tn), lambda j: (0, j))],
        out_specs=pl.BlockSpec((rows, tn), lambda j: (0, j)),
        out_shape=jax.ShapeDtypeStruct((rows, n), F32),
        compiler_params=_params("arbitrary"),
    )(c, w, b.reshape(1, n))


class _Mod:
    def __init__(self, arr, rows_per_seq):
        self.arr = arr
        self.rows_per_seq = rows_per_seq

    def spec(self, tm):
        r = self.arr.shape[1]
        d = self.arr.shape[2]
        if r == 1:
            per = self.rows_per_seq // tm
            return pl.BlockSpec((1, 1, d), lambda i, *_: (i // per, 0, 0))
        assert r == tm
        return pl.BlockSpec((1, r, d), lambda i, *_: (i, 0, 0))


def _ffn_kernel(x_ref, sh_ref, sc_ref, gt_ref, g_ref, wi_ref, wo_ref, fg_ref, o_ref, *, final_norm, splits):
    x = x_ref[...]
    f = wo_ref.shape[0]
    h = (_rmsnorm(x, g_ref[...]) * (1.0 + sc_ref[0]) + sh_ref[0]).astype(BF16)
    acc = None
    for lo, hi in splits:
        gate = _dot(h, wi_ref[:, lo:hi])
        up = _dot(h, wi_ref[:, f + lo:f + hi])
        part = _dot((_silu(gate) * up).astype(BF16), wo_ref[lo:hi, :])
        acc = part if acc is None else acc + part
    out = x + 0.5 * gt_ref[0] * acc
    if final_norm:
        out = _rmsnorm(out, fg_ref[...])
    o_ref[...] = out


def _ffn(x, shift, scale, gate, norm_g, w_in, w_out, final_g, *, tm, final_norm):
    m, d = x.shape
    f = w_out.shape[0]
    assert f % MXU_TILE == 0 and m % tm == 0
    mid = (f // MXU_TILE // 2) * MXU_TILE
    splits = ((0, mid), (mid, f))
    row = lambda i: (i, 0)
    const = lambda i: (0, 0)
    resident = lambda shape: pl.BlockSpec(shape, const, pipeline_mode=pl.Buffered(1))
    return pl.pallas_call(
        functools.partial(_ffn_kernel, final_norm=final_norm, splits=splits),
        grid=(m // tm,),
        in_specs=[pl.BlockSpec((tm, d), row),
                  shift.spec(tm), scale.spec(tm), gate.spec(tm),
                  pl.BlockSpec((1, d), const),
                  resident(w_in.shape), resident(w_out.shape),
                  pl.BlockSpec((1, d), const)],
        out_specs=pl.BlockSpec((tm, d), row),
        out_shape=jax.ShapeDtypeStruct((m, d), F32),
        compiler_params=_params("parallel"),
    )(x, shift.arr, scale.arr, gate.arr, norm_g, w_in, w_out, final_g)


def _proj_kernel(x_ref, sh_ref, sc_ref, g_ref, w_ref, *o_refs, offsets, scales):
    h = (_rmsnorm(x_ref[...], g_ref[...]) * (1.0 + sc_ref[0]) + sh_ref[0]).astype(BF16)
    for o_ref, off, scale in zip(o_refs, offsets, scales):
        width = o_ref.shape[1]
        out = _dot(h, w_ref[:, off:off + width])
        if scale != 1.0:
            out = out * scale
        o_ref[...] = out.astype(o_ref.dtype)


def _proj(x, shift, scale, norm_g, w, segments, *, tm):
    m, d = x.shape
    offsets, off = [], 0
    for width, _, _ in segments:
        offsets.append(off)
        off += -(-width // LANES) * LANES
    assert off == w.shape[1]
    row = lambda i: (i, 0)
    const = lambda i: (0, 0)
    return pl.pallas_call(
        functools.partial(_proj_kernel, offsets=tuple(offsets), scales=tuple(s for _, _, s in segments)),
        grid=(m // tm,),
        in_specs=[pl.BlockSpec((tm, d), row), shift.spec(tm), scale.spec(tm),
                  pl.BlockSpec((1, d), const), pl.BlockSpec(w.shape, const, pipeline_mode=pl.Buffered(1))],
        out_specs=[pl.BlockSpec((tm, width), row) for width, _, _ in segments],
        out_shape=[jax.ShapeDtypeStruct((m, width), dt) for width, dt, _ in segments],
        compiler_params=_params("parallel"),
    )(x, shift.arr, scale.arr, norm_g, w)


def _sort_key(score):
    score = jnp.where(score == 0.0, 0.0, score)
    bits = pltpu.bitcast(score, I32)
    return jnp.where(bits >= 0, bits, bits ^ jnp.int32(0x7FFFFFFF))


def _radix16(count_ge, need, shape):
    def bit_body(t, thr_u):
        cand_u = thr_u | jnp.left_shift(jnp.int32(1), 15 - t)
        return jnp.where(count_ge(cand_u - 32768) >= need, cand_u, thr_u)

    return lax.fori_loop(0, 16, bit_body, jnp.zeros(shape, I32)) - 32768


def _radix_threshold(count_ge, n_top, shape):
    def bit_body(t, carry):
        thr_u, cnt_thr = carry
        cand_u = thr_u | jnp.left_shift(jnp.int32(1), 31 - t)
        cnt = count_ge(cand_u ^ jnp.int32(INT_MIN))
        take = cnt >= n_top
        return jnp.where(take, cand_u, thr_u), jnp.where(take, cnt, cnt_thr)

    thr_u, cnt_thr = lax.fori_loop(0, 32, bit_body, (jnp.zeros(shape, I32), jnp.zeros(shape, I32)))
    return thr_u ^ jnp.int32(INT_MIN), cnt_thr


def _bias_table_kernel(rb_ref, bkt_ref, o_ref):
    far = N_BUCKETS - 1
    for slot in range(bkt_ref.shape[0]):
        bkt = bkt_ref[slot]
        for h in range(N_HEADS):
            acc = jnp.zeros(bkt.shape, F32)
            for b in range(N_BUCKETS - 1):
                acc = jnp.where(bkt == b, (rb_ref[b, h] - rb_ref[far, h]) * LOG2E, acc)
            o_ref[slot, h] = acc


def _bias_table(rel_bias, tq, seq):
    s = np.arange(tq)[:, None]
    t = np.arange(tq)[None, :]
    bkt = np.stack([_t5_bucket_np(t + tq - s), _t5_bucket_np(t - s)])
    assert (_t5_bucket_np(np.arange(tq + 1, seq + 1)) == N_BUCKETS - 1).all()
    return pl.pallas_call(
        _bias_table_kernel,
        in_specs=[pl.BlockSpec(memory_space=pltpu.SMEM), pl.BlockSpec(memory_space=pltpu.VMEM)],
        out_specs=pl.BlockSpec(memory_space=pltpu.VMEM),
        out_shape=jax.ShapeDtypeStruct((2, N_HEADS, tq, tq), F32),
    )(rel_bias, jnp.asarray(bkt))


_V_ROWS = HEAD_DIM + 16


def _dsa_prompt_kernel(q_ref, qi_ref, wi_ref, k_ref, v_ref, ki_ref, bias_ref, o_ref,
                       kb_scr, vt_scr, kib_scr, key_scr, half_scr, mb_scr, m_scr, al_scr, lg_scr, acc_scr,
                       *, n_top, tq):
    i = pl.program_id(1)
    nq = N_HEADS // N_KV_HEADS
    nblk = key_scr.shape[0]
    s_io = lax.broadcasted_iota(I32, (tq, tq), 0)
    t_io = lax.broadcasted_iota(I32, (tq, tq), 1)
    causal = s_io <= t_io

    @pl.when(i == 0)
    def _():
        kb_scr[...] = k_ref[...].astype(BF16)
        kib_scr[...] = ki_ref[...].astype(BF16)
        for c in range(nblk):
            for pair in range(N_KV_HEADS // 2):
                vt = v_ref[c * tq:(c + 1) * tq, pair * 2 * HEAD_DIM:(pair + 1) * 2 * HEAD_DIM].T.astype(BF16)
                vt_scr[c, 2 * pair, 0:HEAD_DIM, :] = vt[0:HEAD_DIM]
                vt_scr[c, 2 * pair + 1, 0:HEAD_DIM, :] = vt[HEAD_DIM:2 * HEAD_DIM]
            for n in range(N_KV_HEADS):
                vt_scr[c, n, HEAD_DIM:_V_ROWS, :] = jnp.ones((_V_ROWS - HEAD_DIM, tq), BF16)

    def chunk(j):
        return pl.ds(pl.multiple_of(j * tq, tq), tq)

    def fold(a):
        return jnp.sum(a.reshape(tq // SUBLANES, SUBLANES, tq), axis=0)

    qi = qi_ref[...]
    qi_st = jnp.concatenate([qi[:, h * IDX_DIM:(h + 1) * IDX_DIM] for h in range(N_IDX_HEADS)], axis=0)
    wi_t = wi_ref[...].T
    wi_row = jnp.concatenate([wi_t[h:h + 1, :] for h in range(N_IDX_HEADS)], axis=1)
    idx_scale = (N_IDX_HEADS * IDX_DIM) ** -0.5

    def visible(j):
        return (j < i) | ((j == i) & causal)

    n_pairs = (i + 2) // 2

    def score_body(jp, carry):
        for j in (2 * jp, 2 * jp + 1):
            d = jnp.maximum(_dot_nt(kib_scr[chunk(j), :], qi_st), 0.0) * wi_row
            s = d[:, 0:tq]
            for h in range(1, N_IDX_HEADS):
                s = s + d[:, h * tq:(h + 1) * tq]
            key = _sort_key(jnp.where(visible(j), s * idx_scale, NEG_INF))
            key_scr[j] = key
            half_scr[j] = jnp.right_shift(key, 16).astype(I16)
        return carry

    lax.fori_loop(0, n_pairs, score_body, 0)

    def count(pred):
        def body(jp, c):
            return c + fold(pred(key_scr[2 * jp]).astype(I32)) + fold(pred(key_scr[2 * jp + 1]).astype(I32))
        c = lax.fori_loop(0, n_pairs, body, jnp.zeros((SUBLANES, tq), I32))
        return jnp.sum(c, axis=0, keepdims=True)

    n_quads = (i + 4) // 4

    def park_body(j, carry):
        half_scr[j] = jnp.full((tq, tq), -32768, I16)
        return carry

    lax.fori_loop(2 * n_pairs, 4 * n_quads, park_body, 0)

    def count16(pred):
        def fold16(a):
            out = a[0:16]
            for r in range(16, tq, 16):
                out = out + a[r:r + 16]
            return out

        def body(jq, c):
            for u in range(4):
                c = c + fold16(pred(half_scr[4 * jq + u]).astype(I16))
            return c
        c = lax.fori_loop(0, n_quads, body, jnp.zeros((16, tq), I16))
        return jnp.sum(c.astype(I32), axis=0, keepdims=True)

    def count16_ge(cand):
        cand16 = cand.astype(I16)
        return count16(lambda half: half >= cand16)

    thr_hi = _radix16(count16_ge, n_top, (1, tq))
    thr_hi16 = thr_hi.astype(I16)
    need_lo = n_top - count16(lambda half: half > thr_hi16)

    def low_body(jp, carry):
        for j in (2 * jp, 2 * jp + 1):
            key = key_scr[j]
            low = jnp.bitwise_and(key, 0xFFFF) - 32768
            half_scr[j] = jnp.where(jnp.right_shift(key, 16) == thr_hi, low, -32768).astype(I16)
        return carry

    lax.fori_loop(0, n_pairs, low_body, 0)
    thr = thr_hi * 65536 + (_radix16(count16_ge, need_lo, (1, tq)) + 32768)
    cnt_thr = count(lambda key: key >= thr)

    def sel_plain():
        def body(j, carry):
            mb_scr[j] = jnp.where((key_scr[j] >= thr) & visible(j), 0.0, NEG_INF)
            return carry
        lax.fori_loop(0, i + 1, body, 0)

    def sel_ties():
        need = (n_top - count(lambda key: key > thr)).astype(F32)
        tril = (s_io >= t_io).astype(BF16)

        def body(j, run_eq):
            key = key_scr[j]
            eq = key == thr
            pre = _dot(tril, eq.astype(BF16))
            sel = ((key > thr) | (eq & (run_eq + pre <= need))) & visible(j)
            mb_scr[j] = jnp.where(sel, 0.0, NEG_INF)
            return run_eq + pre[tq - 1:tq, :]
        lax.fori_loop(0, i + 1, body, jnp.zeros((1, tq), F32))

    lax.cond(jnp.max(cnt_thr) > n_top, sel_ties, sel_plain)

    q = q_ref[...]
    q_st = [jnp.concatenate([q[:, (n * nq + g) * HEAD_DIM:(n * nq + g + 1) * HEAD_DIM] for g in range(nq)], axis=0)
            for n in range(N_KV_HEADS)]
    m_scr[...] = jnp.full(m_scr.shape, -1e30, F32)
    acc_scr[...] = jnp.zeros_like(acc_scr)

    heads = range(N_KV_HEADS)

    def stage_a(j, slot):
        mb4 = jnp.concatenate([mb_scr[j]] * nq, axis=1)
        lgs = [_dot_nt(kb_scr[chunk(j), n * HEAD_DIM:(n + 1) * HEAD_DIM], q_st[n]) + mb4 for n in heads]
        if slot is not None:
            lgs = [lgs[n] + jnp.concatenate([bias_ref[slot, n * nq + g] for g in range(nq)], axis=1) for n in heads]
        for n in heads:
            m_old = m_scr[n]
            m_new = jnp.maximum(m_old, jnp.max(lgs[n], axis=0, keepdims=True))
            al_scr[n] = jnp.exp2(m_old - m_new)
            m_scr[n] = m_new
            lg_scr[j % 2, n] = lgs[n]

    def stage_b(j):
        ps = [jnp.exp2(lg_scr[j % 2, n] - m_scr[n]).astype(BF16) for n in heads]
        pvs = [_dot(vt_scr[j, n], ps[n]) for n in heads]
        for n in heads:
            acc_scr[n] = al_scr[n] * acc_scr[n] + pvs[n]

    @pl.when(i >= 2)
    def _():
        stage_a(0, None)

        def far_body(j, carry):
            stage_b(j - 1)
            stage_a(j, None)
            return carry

        lax.fori_loop(1, i - 1, far_body, 0)
        stage_b(i - 2)
        stage_a(i - 1, 0)

    @pl.when(i == 1)
    def _():
        stage_a(0, 0)

    @pl.when(i >= 1)
    def _():
        stage_b(i - 1)
        stage_a(i, 1)

    @pl.when(i == 0)
    def _():
        stage_a(0, 1)

    stage_b(i)

    for n in range(N_KV_HEADS):
        acc = acc_scr[n]
        out_t = acc[0:HEAD_DIM] / acc[HEAD_DIM:HEAD_DIM + 1]
        for pair in range(nq // 2):
            g0 = 2 * pair
            two = jnp.concatenate([out_t[:, g0 * tq:(g0 + 1) * tq], out_t[:, (g0 + 1) * tq:(g0 + 2) * tq]], axis=0)
            h0 = n * nq + g0
            o_ref[:, h0 * HEAD_DIM:(h0 + 2) * HEAD_DIM] = two.T.astype(o_ref.dtype)


def _dsa_prompt(q, qi, wi, k, v, ki, bias_tab, *, batch, seq, tq):
    nblk = seq // tq
    n_top = min(TOPK_MAX, seq // 4)
    nq = N_HEADS // N_KV_HEADS
    assert tq == LANES and nq % 2 == 0 and N_KV_HEADS % 2 == 0 and nblk % 4 == 0
    blk = lambda b, i: (b * nblk + i, 0)
    whole = lambda b, i: (b, 0)
    kvw = N_KV_HEADS * HEAD_DIM
    return pl.pallas_call(
        functools.partial(_dsa_prompt_kernel, n_top=n_top, tq=tq),
        grid=(batch, nblk),
        in_specs=[pl.BlockSpec((tq, N_HEADS * HEAD_DIM), blk),
                  pl.BlockSpec((tq, N_IDX_HEADS * IDX_DIM), blk),
                  pl.BlockSpec((tq, LANES), blk),
                  pl.BlockSpec((seq, kvw), whole),
                  pl.BlockSpec((seq, kvw), whole),
                  pl.BlockSpec((seq, IDX_DIM), whole),
                  pl.BlockSpec(bias_tab.shape, lambda b, i: (0, 0, 0, 0))],
        out_specs=pl.BlockSpec((tq, N_HEADS * HEAD_DIM), blk),
        out_shape=jax.ShapeDtypeStruct((batch * seq, N_HEADS * HEAD_DIM), BF16),
        scratch_shapes=[pltpu.VMEM((seq, kvw), BF16),
                        pltpu.VMEM((nblk, N_KV_HEADS, _V_ROWS, tq), BF16),
                        pltpu.VMEM((seq, IDX_DIM), BF16),
                        pltpu.VMEM((nblk, tq, tq), I32), pltpu.VMEM((nblk, tq, tq), I16),
                        pltpu.VMEM((nblk, tq, tq), F32),
                        pltpu.VMEM((N_KV_HEADS, 1, nq * tq), F32), pltpu.VMEM((N_KV_HEADS, 1, nq * tq), F32),
                        pltpu.VMEM((2, N_KV_HEADS, tq, nq * tq), F32),
                        pltpu.VMEM((N_KV_HEADS, _V_ROWS, nq * tq), F32)],
        compiler_params=_params("arbitrary", "arbitrary"),
    )(q, qi, wi, k, v, ki, bias_tab)


def _dsa_s_score_kernel(pt_ref, qi_ref, wi_ref, *refs, pg):
    page_refs, o_ref = refs[:pg], refs[pg]
    qi = qi_ref[0].astype(BF16)
    wi = wi_ref[0]
    idx_scale = (N_IDX_HEADS * IDX_DIM) ** -0.5
    for r in range(pg):
        d = _dot_nt(qi, page_refs[r][0])
        o_ref[r, 0] = jnp.sum(jnp.maximum(d, 0.0) * wi, axis=0, keepdims=True) * idx_scale


def _dsa_s_scores(page_table, qi, wi_bc, cache_kidx, *, pg):
    db, n_pages = page_table.shape
    page_spec = lambda r: pl.BlockSpec(
        (1, PAGE_SIZE, IDX_DIM), lambda b, s, pt: (pt[b * n_pages + s * pg + r], 0, 0))
    grid_spec = pltpu.PrefetchScalarGridSpec(
        num_scalar_prefetch=1,
        grid=(db, n_pages // pg),
        in_specs=[pl.BlockSpec((1, N_IDX_HEADS, IDX_DIM), lambda b, s, pt: (b, 0, 0)),
                  pl.BlockSpec((1, N_IDX_HEADS, LANES), lambda b, s, pt: (b, 0, 0))]
                 + [page_spec(r) for r in range(pg)],
        out_specs=pl.BlockSpec((pg, 1, 1, PAGE_SIZE), lambda b, s, pt: (s, b, 0, 0)),
    )
    return pl.pallas_call(
        functools.partial(_dsa_s_score_kernel, pg=pg),
        grid_spec=grid_spec,
        out_shape=jax.ShapeDtypeStruct((n_pages, db, 1, PAGE_SIZE), F32),
        compiler_params=_params("arbitrary", "arbitrary"),
    )(page_table.reshape(-1), qi, wi_bc, *([cache_kidx] * pg))


def _dsa_s_select_kernel(sc_ref, qi_ref, kis_ref, wi_ref, hsum_ref, o_ref, key_scr, *, n_top):
    n_pages, db, _ = sc_ref.shape
    idx_scale = (N_IDX_HEADS * IDX_DIM) ** -0.5

    prod = (qi_ref[...].astype(BF16).astype(F32)
            * jnp.concatenate([kis_ref[...].astype(BF16).astype(F32)] * N_IDX_HEADS, axis=1))
    d_self = _dot_exact_rhs(prod, hsum_ref[...])
    s_self = jnp.sum((jnp.maximum(d_self, 0.0) * wi_ref[...]).T, axis=0, keepdims=True) * idx_scale
    key_self = _sort_key(s_self)

    def to_keys(r, carry):
        key_scr[r] = _sort_key(sc_ref[r].T)
        return carry
    lax.fori_loop(0, n_pages, to_keys, 0)

    def fold(a):
        return jnp.sum(a.reshape(PAGE_SIZE // SUBLANES, SUBLANES, db), axis=0)

    def count(pred):
        def body(rp, c):
            return c + fold(pred(key_scr[2 * rp]).astype(I32)) + fold(pred(key_scr[2 * rp + 1]).astype(I32))
        c = lax.fori_loop(0, n_pages // 2, body, jnp.zeros((SUBLANES, db), I32))
        return jnp.sum(c, axis=0, keepdims=True) + pred(key_self).astype(I32)

    thr, cnt_thr = _radix_threshold(lambda cand: count(lambda key: key >= cand), n_top, (1, db))

    def emit(r, sel):
        o_ref[r] = jnp.where(sel, 0.0, NEG_INF).T

    def sel_plain():
        def body(r, carry):
            emit(r, key_scr[r] >= thr)
            return carry
        lax.fori_loop(0, n_pages, body, 0)
        emit(n_pages, jnp.broadcast_to(key_self >= thr, (PAGE_SIZE, db)))

    def sel_ties():
        need = (n_top - count(lambda key: key > thr)).astype(F32)
        r_io = lax.broadcasted_iota(I32, (PAGE_SIZE, PAGE_SIZE), 0)
        c_io = lax.broadcasted_iota(I32, (PAGE_SIZE, PAGE_SIZE), 1)
        tril = (r_io >= c_io).astype(BF16)

        def body(r, run_eq):
            key = key_scr[r]
            eq = key == thr
            pre = _dot(tril, eq.astype(BF16))
            emit(r, (key > thr) | (eq & (run_eq + pre <= need)))
            return run_eq + pre[PAGE_SIZE - 1:PAGE_SIZE, :]
        run_eq = lax.fori_loop(0, n_pages, body, jnp.zeros((1, db), F32))
        sel_self = (key_self > thr) | ((key_self == thr) & (run_eq + 1.0 <= need))
        emit(n_pages, jnp.broadcast_to(sel_self, (PAGE_SIZE, db)))

    lax.cond(jnp.max(cnt_thr) > n_top, sel_ties, sel_plain)


def _dsa_s_select(scores, qi, ki_s, wi, *, n_top):
    n_pages, db, _ = scores.shape
    assert db == LANES and n_pages % 2 == 0
    hsum = np.zeros((N_IDX_HEADS * IDX_DIM, LANES), np.float32)
    hsum[np.arange(N_IDX_HEADS * IDX_DIM), np.arange(N_IDX_HEADS * IDX_DIM) // IDX_DIM] = 1.0
    return pl.pallas_call(
        functools.partial(_dsa_s_select_kernel, n_top=n_top),
        out_shape=jax.ShapeDtypeStruct((n_pages + 1, db, PAGE_SIZE), F32),
        scratch_shapes=[pltpu.VMEM((n_pages, PAGE_SIZE, db), I32)],
        compiler_params=pltpu.CompilerParams(vmem_limit_bytes=VMEM_LIMIT),
    )(scores, qi, ki_s, wi, jnp.asarray(hsum, BF16))


def _bias_table_s_kernel(rbt_ref, bkt_ref, o_ref):
    rbt = rbt_ref[...] * LOG2E
    for r in range(bkt_ref.shape[0]):
        bkt = bkt_ref[r:r + 1, :]
        out = jnp.zeros((N_HEADS, bkt.shape[1]), F32)
        for b in range(N_BUCKETS):
            out = jnp.where(bkt == b, rbt[:, b:b + 1], out)
        o_ref[r] = out


def _bias_table_s(rel_bias, n_pages):
    past = n_pages * PAGE_SIZE
    pos = np.arange((n_pages + 1) * PAGE_SIZE).reshape(n_pages + 1, PAGE_SIZE)
    pos[n_pages] = past
    return pl.pallas_call(
        _bias_table_s_kernel,
        out_shape=jax.ShapeDtypeStruct((n_pages + 1, N_HEADS, PAGE_SIZE), F32),
    )(rel_bias.T, jnp.asarray(_t5_bucket_np(past - pos)))


def _dsa_s_attend_kernel(pt_ref, mb_ref, bias_ref, q_ref, ks_ref, vs_ref, *refs, pg, n_pages):
    k_refs, v_refs = refs[:pg], refs[pg:2 * pg]
    o_ref, m_scr, l_scr, acc_scr = refs[2 * pg:]
    s = pl.program_id(1)
    nq = N_HEADS // N_KV_HEADS
    kvw = N_KV_HEADS * HEAD_DIM

    @pl.when(s == 0)
    def _():
        m_scr[...] = jnp.full(m_scr.shape, -1e30, F32)
        l_scr[...] = jnp.zeros_like(l_scr)
        acc_scr[...] = jnp.zeros_like(acc_scr)

    q = q_ref[0]
    h_io = lax.broadcasted_iota(I32, (N_HEADS, kvw), 0)
    c_io = lax.broadcasted_iota(I32, (N_HEADS, kvw), 1)
    band = (c_io // HEAD_DIM) == (h_io // nq)
    q_bd = jnp.where(band, jnp.concatenate([q] * N_KV_HEADS, axis=1), 0.0)
    logits = []
    for r in range(pg):
        page = s * pg + r
        logits.append(_dot_nt(q_bd, k_refs[r][0].astype(BF16)) + bias_ref[page] + mb_ref[page, 0])
    m_old = m_scr[...]
    m_new = m_old
    for lg in logits:
        m_new = jnp.maximum(m_new, jnp.max(lg, axis=1, keepdims=True))
    alpha = jnp.exp2(m_old - m_new)
    l_new = alpha * l_scr[...]
    acc = alpha * acc_scr[...]
    for r, lg in enumerate(logits):
        p = jnp.exp2(lg - m_new)
        l_new = l_new + jnp.sum(p, axis=1, keepdims=True)
        acc = acc + _dot(p.astype(BF16), v_refs[r][0].astype(BF16))
    m_scr[...] = m_new
    l_scr[...] = l_new
    acc_scr[...] = acc

    @pl.when(s == pl.num_programs(1) - 1)
    def _():
        ks = ks_ref[0].astype(BF16).astype(F32)
        lg = jnp.sum(q_bd.astype(F32) * ks, axis=1, keepdims=True)
        lg = lg + bias_ref[n_pages][:, 0:1] + mb_ref[n_pages, 0][:, 0:1]
        m_fin = jnp.maximum(m_new, lg)
        a = jnp.exp2(m_new - m_fin)
        p = jnp.exp2(lg - m_fin)
        l_fin = a * l_new + p
        out = (a * acc + p.astype(BF16).astype(F32) * vs_ref[0].astype(BF16).astype(F32)) / l_fin
        out = jnp.where(band, out, 0.0)
        res = out[:, 0:HEAD_DIM]
        for n in range(1, N_KV_HEADS):
            res = res + out[:, n * HEAD_DIM:(n + 1) * HEAD_DIM]
        o_ref[0] = res.astype(o_ref.dtype)


def _dsa_s_attend(page_table, mask, bias_tab, q, k_s, v_s, cache_k, cache_v, *, pg):
    db, n_pages = page_table.shape
    kvw = N_KV_HEADS * HEAD_DIM
    width = PAGE_SIZE
    per_seq = lambda *shape: pl.BlockSpec((1,) + shape, lambda b, s, pt: (b,) + (0,) * len(shape))
    page_spec = lambda r: pl.BlockSpec(
        (1, PAGE_SIZE, kvw), lambda b, s, pt: (pt[b * n_pages + s * pg + r], 0, 0))
    grid_spec = pltpu.PrefetchScalarGridSpec(
        num_scalar_prefetch=1,
        grid=(db, n_pages // pg),
        in_specs=[pl.BlockSpec((n_pages + 1, 1, 1, width), lambda b, s, pt: (0, b, 0, 0)),
                  pl.BlockSpec(bias_tab.shape, lambda b, s, pt: (0, 0, 0)),
                  per_seq(N_HEADS, HEAD_DIM), per_seq(1, kvw), per_seq(1, kvw)]
                 + [page_spec(r) for r in range(pg)] * 2,
        out_specs=per_seq(N_HEADS, HEAD_DIM),
        scratch_shapes=[pltpu.VMEM((N_HEADS, 1), F32), pltpu.VMEM((N_HEADS, 1), F32),
                        pltpu.VMEM((N_HEADS, kvw), F32)],
    )
    return pl.pallas_call(
        functools.partial(_dsa_s_attend_kernel, pg=pg, n_pages=n_pages),
        grid_spec=grid_spec,
        out_shape=jax.ShapeDtypeStruct((db, N_HEADS, HEAD_DIM), BF16),
        compiler_params=_params("arbitrary", "arbitrary"),
    )(page_table.reshape(-1), mask.reshape(n_pages + 1, db, 1, width), bias_tab, q, k_s, v_s,
      *([cache_k] * pg), *([cache_v] * pg))


def _ssd_kernel(xbc_ref, z_ref, dt_ref, cw_ref, cb_ref, alog_ref, dtb_ref, dsk_ref, ng_ref, e_ref, *refs,
                rows, has_init, d_inner):
    if has_init:
        conv0_ref, ssm0_ref = refs[:2]
        refs = refs[2:]
    y_ref, ssm_ref, conv_ref, ext_scr, ht_scr, y_scr = refs
    c = pl.program_id(1)
    cl = SSD_CHUNK
    gw = d_inner // SSD_GROUPS
    hpg = gw // SSD_HEAD_DIM
    n_heads = d_inner // SSD_HEAD_DIM
    conv_dim = d_inner + 2 * SSD_GROUPS * D_STATE
    conv_w = cw_ref.shape[0]
    top = SUBLANES
    row_io = lax.broadcasted_iota(I32, (cl, 1), 0)

    def padded(ref):
        a = ref[0]
        if rows == cl:
            return a
        return jnp.where(row_io < rows, jnp.broadcast_to(a, (cl, a.shape[1])), 0.0)

    @pl.when(c == 0)
    def _():
        ext_scr[0:top, :] = jnp.zeros((top, conv_dim), F32)
        if has_init:
            ext_scr[top - conv_w + 1:top, :] = conv0_ref[0]
            for g in range(SSD_GROUPS):
                ht_scr[g] = ssm0_ref[0, g * gw:(g + 1) * gw, :].T
        else:
            ht_scr[...] = jnp.zeros_like(ht_scr)

    ext_scr[top:top + cl, :] = padded(xbc_ref)
    conv_ref[0] = ext_scr[top + rows - conv_w + 1:top + rows, :]

    cblk = 512
    for cb in range(conv_dim // cblk):
        sl = slice(cb * cblk, (cb + 1) * cblk)
        acc = jnp.broadcast_to(cb_ref[:, sl], (cl, cblk))
        for w in range(conv_w):
            acc = acc + ext_scr[top - conv_w + 1 + w:top - conv_w + 1 + w + cl, sl] * cw_ref[w:w + 1, sl]
        y_scr[:, sl] = _silu(acc)
    ext_scr[top - conv_w + 1:top, :] = ext_scr[top + cl - conv_w + 1:top + cl, :]

    dt = jax.nn.softplus(padded(dt_ref) + dtb_ref[...])
    if rows < cl:
        dt = jnp.where(row_io < rows, dt, 0.0)
    a_neg2 = -jnp.exp(alog_ref[...]) * LOG2E
    r_io = lax.broadcasted_iota(I32, (cl, cl), 0)
    c_io = lax.broadcasted_iota(I32, (cl, cl), 1)
    tril = r_io >= c_io
    acs = _dot_exact_lhs(tril.astype(BF16), dt * a_neg2)
    acs_t = acs.T
    acs_last = acs[cl - 1:cl, :]
    stacked = jnp.concatenate([dt, jnp.exp2(acs), jnp.exp2(acs_last - acs)], axis=0)
    hi = stacked.astype(BF16)
    lo = (stacked - hi.astype(F32)).astype(BF16)
    expanded = _dot(hi, e_ref[...]) + _dot(lo, e_ref[...])
    dt_x, ea_x, te_x = expanded[0:cl], expanded[cl:2 * cl], expanded[2 * cl:3 * cl]

    z = padded(z_ref)
    for g in range(SSD_GROUPS):
        gs = slice(g * gw, (g + 1) * gw)
        x_g = y_scr[:, gs]
        b_g = y_scr[:, d_inner + g * D_STATE:d_inner + (g + 1) * D_STATE]
        c_g = y_scr[:, d_inner + (SSD_GROUPS + g) * D_STATE:d_inner + (SSD_GROUPS + g + 1) * D_STATE]
        c16 = c_g.astype(BF16)
        cbm = _dot_nt(c16, b_g.astype(BF16))
        xdt = x_g * dt_x[:, gs]
        xdt16 = xdt.astype(BF16)
        ht = ht_scr[g]
        y_g = _dot(c16, ht.astype(BF16)) * ea_x[:, gs] + dsk_ref[:, gs] * x_g
        ht_scr[g] = ht * ea_x[cl - 1:cl, gs] + _dot(b_g.T.astype(BF16), (xdt * te_x[:, gs]).astype(BF16))
        diag = []
        for r in range(hpg):
            h = g * hpg + r
            seg = acs[:, h:h + 1] - acs_t[h:h + 1, :]
            m = (cbm * jnp.exp2(jnp.where(tril, seg, NEG_INF))).astype(BF16)
            diag.append(_dot(m, xdt16[:, r * SSD_HEAD_DIM:(r + 1) * SSD_HEAD_DIM]))
        y_g = (y_g + jnp.concatenate(diag, axis=1)) * _silu(z[:, gs])
        y_g = y_g * lax.rsqrt(jnp.mean(y_g * y_g, axis=1, keepdims=True) + EPS) * ng_ref[:, gs]
        y_ref[0, :, gs] = y_g[0:rows].astype(y_ref.dtype)

    @pl.when(c == pl.num_programs(1) - 1)
    def _():
        for g in range(SSD_GROUPS):
            ssm_ref[0, g * gw:(g + 1) * gw, :] = ht_scr[g].T


def _ssd(xbc, z, dt, conv_w, conv_b, a_log, dt_bias, d_skip, norm_g, *, n_seq, n_chunks, rows,
         conv0=None, ssm0=None):
    conv_dim = xbc.shape[-1]
    d_inner = z.shape[-1]
    n_heads = d_inner // SSD_HEAD_DIM
    cw = conv_w.shape[0]
    has_init = conv0 is not None
    pad = lambda a: jnp.pad(a, (0, LANES - a.shape[0])).reshape(1, LANES)
    expand = np.zeros((LANES, d_inner), np.float32)
    expand[np.arange(d_inner) // SSD_HEAD_DIM, np.arange(d_inner)] = 1.0
    step = lambda b, c: (b * n_chunks + c, 0, 0)
    seq = lambda b, c: (b, 0, 0)
    const = lambda b, c: (0, 0)
    in_specs = [pl.BlockSpec((1, rows, conv_dim), step), pl.BlockSpec((1, rows, d_inner), step),
                pl.BlockSpec((1, rows, LANES), step),
                pl.BlockSpec((cw, conv_dim), const), pl.BlockSpec((1, conv_dim), const),
                pl.BlockSpec((1, LANES), const), pl.BlockSpec((1, LANES), const),
                pl.BlockSpec((1, d_inner), const), pl.BlockSpec((1, d_inner), const),
                pl.BlockSpec((LANES, d_inner), const)]
    args = [xbc, z, dt, conv_w, conv_b.reshape(1, conv_dim), pad(a_log), pad(dt_bias),
            jnp.repeat(d_skip, SSD_HEAD_DIM).reshape(1, d_inner), norm_g.reshape(1, d_inner),
            jnp.asarray(expand, BF16)]
    if has_init:
        in_specs += [pl.BlockSpec((1, cw - 1, conv_dim), seq), pl.BlockSpec((1, d_inner, D_STATE), seq)]
        args += [conv0, ssm0]
    return pl.pallas_call(
        functools.partial(_ssd_kernel, rows=rows, has_init=has_init, d_inner=d_inner),
        grid=(n_seq, n_chunks),
        in_specs=in_specs,
        out_specs=[pl.BlockSpec((1, rows, d_inner), step), pl.BlockSpec((1, d_inner, D_STATE), seq),
                   pl.BlockSpec((1, cw - 1, conv_dim), seq)],
        out_shape=[jax.ShapeDtypeStruct((n_seq * n_chunks, rows, d_inner), BF16),
                   jax.ShapeDtypeStruct((n_seq, d_inner, D_STATE), F32),
                   jax.ShapeDtypeStruct((n_seq, cw - 1, conv_dim), F32)],
        scratch_shapes=[pltpu.VMEM((SUBLANES + SSD_CHUNK, conv_dim), F32),
                        pltpu.VMEM((SSD_GROUPS, D_STATE, d_inner // SSD_GROUPS), F32),
                        pltpu.VMEM((SSD_CHUNK, conv_dim), F32)],
        compiler_params=_params("arbitrary", "arbitrary"),
    )(*args)


def _merge_kernel(x_ref, sh_ref, sc_ref, gt_ref, g_ref, att_ref, ssd_ref, wg_ref, wa_ref, ws_ref, wo_ref, o_ref):
    x = x_ref[...]
    d = x.shape[1]
    h = (_rmsnorm(x, g_ref[...]) * (1.0 + sc_ref[0]) + sh_ref[0]).astype(BF16)
    gates = _dot(h, wg_ref[...])
    merged = (_sigmoid(gates[:, 0:d]) * _dot(att_ref[...], wa_ref[...])
              + _sigmoid(gates[:, d:2 * d]) * _dot(ssd_ref[...], ws_ref[...]))
    o_ref[...] = x + gt_ref[0] * _dot(merged.astype(BF16), wo_ref[...])


def _merge(x, shift, scale, gate, norm_g, att, ssd_y, w_g, w_a, w_s, w_o, *, tm):
    m, d = x.shape
    row = lambda i: (i, 0)
    const = lambda i: (0, 0)
    resident = lambda shape: pl.BlockSpec(shape, const, pipeline_mode=pl.Buffered(1))
    return pl.pallas_call(
        _merge_kernel,
        grid=(m // tm,),
        in_specs=[pl.BlockSpec((tm, d), row), shift.spec(tm), scale.spec(tm), gate.spec(tm),
                  pl.BlockSpec((1, d), const),
                  pl.BlockSpec((tm, att.shape[1]), row), pl.BlockSpec((tm, ssd_y.shape[1]), row),
                  resident(w_g.shape), resident(w_a.shape), resident(w_s.shape), resident(w_o.shape)],
        out_specs=pl.BlockSpec((tm, d), row),
        out_shape=jax.ShapeDtypeStruct((m, d), F32),
        compiler_params=_params("parallel"),
    )(x, shift.arr, scale.arr, gate.arr, norm_g, att, ssd_y, w_g, w_a, w_s, w_o)


def _pad_cols(w, width):
    return jnp.pad(w, ((0, 0), (0, width - w.shape[1])))


def _trunk(x, mods, rows_per_seq, p, tm, attend, ssd_fn, final_g):
    sh1, sc1, g1, sh2, sc2, g2, sh3, sc3, g3 = [_Mod(a, rows_per_seq) for a in mods]
    x = _ffn(x, sh1, sc1, g1, p["norm_ffn1"], p["w_ffn1_in"], p["w_ffn1_out"], final_g, tm=tm, final_norm=False)
    q, k, v, qi, ki, wi = _proj(x, sh2, sc2, p["norm_mix"], p["w_att"], p["seg_att"], tm=tm)
    z, xbc, dt = _proj(x, sh2, sc2, p["norm_mix"], p["w_ssd"], p["seg_ssd"], tm=tm)
    att = attend(q, k, v, qi, ki, wi)
    ssd_y, ssm_new, conv_new = ssd_fn(z, xbc, dt)
    x = _merge(x, sh2, sc2, g2, p["norm_mix"], att, ssd_y, p["w_gate"], p["w_attn_out"], p["w_ssd_out"], p["w_out"],
               tm=tm)
    y = _ffn(x, sh3, sc3, g3, p["norm_ffn2"], p["w_ffn2_in"], p["w_ffn2_out"], final_g, tm=tm, final_norm=True)
    return y, (k, v, ki, ssm_new, conv_new)


def kernel(x_prompt, x_sample, c_prompt, c_sample, cache_k, cache_v, cache_kidx, state_ssm, state_conv, page_table,
           w_ada, b_ada, norm_ffn1, w_ffn1_in, w_ffn1_out, norm_mix, w_in, rel_bias, conv_w, conv_b, a_log, dt_bias,
           d_skip, norm_ssd, w_attn_out, w_ssd_out, w_out, norm_ffn2, w_ffn2_in, w_ffn2_out, norm_final):
    depth = w_ada.shape[0]
    assert depth == 1
    batch, seq, d = x_prompt.shape
    db, dec_seq, _ = x_sample.shape
    assert dec_seq == 1
    n_pool = cache_k.shape[1]
    n_pages = page_table.shape[1]
    d_inner = norm_ssd.shape[1]
    conv_dim = conv_w.shape[2]
    n_ssd_heads = d_inner // SSD_HEAD_DIM
    att_q = N_HEADS * HEAD_DIM
    att_kv = N_KV_HEADS * HEAD_DIM
    idx_q = N_IDX_HEADS * IDX_DIM
    l = 0

    widths = (att_q, att_kv, att_kv, idx_q, IDX_DIM, N_IDX_HEADS, d_inner, conv_dim, n_ssd_heads, d, d)
    bounds = np.concatenate([[0], np.cumsum(widths)])
    assert bounds[-1] == w_in.shape[2]
    cols = [w_in[l][:, bounds[i]:bounds[i + 1]].astype(BF16) for i in range(len(widths))]
    w_q, w_k, w_v, w_qi, w_ki, w_wi, w_z, w_xbc, w_dt, w_ga, w_gs = cols
    row1 = lambda a: a.reshape(1, -1)
    p = {
        "norm_ffn1": row1(norm_ffn1[l]), "w_ffn1_in": w_ffn1_in[l].astype(BF16), "w_ffn1_out": w_ffn1_out[l].astype(BF16),
        "norm_mix": row1(norm_mix[l]),
        "w_att": jnp.concatenate([w_q, w_k, w_v, w_qi, _pad_cols(w_ki, LANES), _pad_cols(w_wi, LANES)], axis=1),
        "seg_att": [(att_q, BF16, HEAD_DIM ** -0.5 * LOG2E), (att_kv, F32, 1.0), (att_kv, F32, 1.0),
                    (idx_q, BF16, 1.0), (IDX_DIM, F32, 1.0), (LANES, F32, 1.0)],
        "w_ssd": jnp.concatenate([w_z, w_xbc, _pad_cols(w_dt, LANES)], axis=1),
        "seg_ssd": [(d_inner, F32, 1.0), (conv_dim, F32, 1.0), (LANES, F32, 1.0)],
        "w_gate": jnp.concatenate([w_ga, w_gs], axis=1),
        "w_attn_out": w_attn_out[l].astype(BF16), "w_ssd_out": w_ssd_out[l].astype(BF16), "w_out": w_out[l].astype(BF16),
        "norm_ffn2": row1(norm_ffn2[l]), "w_ffn2_in": w_ffn2_in[l].astype(BF16), "w_ffn2_out": w_ffn2_out[l].astype(BF16),
    }
    final_g = row1(norm_final)
    ssd_args = (conv_w[l], conv_b[l], a_log[l], dt_bias[l], d_skip[l], norm_ssd[l])

    ada = _ada(jnp.concatenate([c_prompt, c_sample], axis=0), w_ada[l], b_ada[l])
    ada_p = [a.reshape(batch, 1, d) for a in jnp.split(ada[:batch], 9, axis=1)]
    ada_s = [a.reshape(1, db, d) for a in jnp.split(ada[batch:], 9, axis=1)]

    tq = 128
    n_chunks = seq // SSD_CHUNK
    bias_tab = _bias_table(rel_bias, tq, seq)

    def attend_p(q, k, v, qi, ki, wi):
        return _dsa_prompt(q, qi, wi, k, v, ki, bias_tab, batch=batch, seq=seq, tq=tq)

    def ssd_p(z, xbc, dt):
        r3 = lambda a: a.reshape(batch * n_chunks, SSD_CHUNK, a.shape[-1])
        y, ssm, conv = _ssd(r3(xbc), r3(z), r3(dt), *ssd_args, n_seq=batch, n_chunks=n_chunks, rows=SSD_CHUNK)
        return y.reshape(batch * seq, d_inner), ssm, conv

    yp, (k_p, v_p, ki_p, ssm_p, conv_p) = _trunk(x_prompt.reshape(batch * seq, d), ada_p, seq, p, 512,
                                                 attend_p, ssd_p, final_g)

    pg = min(16, n_pages)
    n_top_s = min(TOPK_MAX, (n_pages * PAGE_SIZE + 1) // 4)
    bias_tab_s = _bias_table_s(rel_bias, n_pages)

    def attend_s(q, k, v, qi, ki, wi):
        wi_bc = jnp.broadcast_to(wi[:, :N_IDX_HEADS, None], (db, N_IDX_HEADS, LANES))
        scores = _dsa_s_scores(page_table, qi.reshape(db, N_IDX_HEADS, IDX_DIM), wi_bc,
                               cache_kidx[l].astype(BF16), pg=min(2 * pg, n_pages))
        mask = _dsa_s_select(scores.reshape(n_pages, db, PAGE_SIZE), qi, ki, wi, n_top=n_top_s)
        att = _dsa_s_attend(page_table, mask, bias_tab_s, q.reshape(db, N_HEADS, HEAD_DIM),
                            k.reshape(db, 1, att_kv), v.reshape(db, 1, att_kv),
                            cache_k[l].reshape(n_pool, PAGE_SIZE, att_kv),
                            cache_v[l].reshape(n_pool, PAGE_SIZE, att_kv), pg=pg)
        return att.reshape(db, att_q)

    def ssd_s(z, xbc, dt):
        r3 = lambda a: a.reshape(db, 1, a.shape[-1])
        y, ssm, conv = _ssd(r3(xbc), r3(z), r3(dt), *ssd_args, n_seq=db, n_chunks=1, rows=1,
                            conv0=state_conv[l], ssm0=state_ssm[l].reshape(db, d_inner, D_STATE))
        return y.reshape(db, d_inner), ssm, conv

    ys, (k_s, v_s, ki_s, ssm_s, conv_s) = _trunk(x_sample.reshape(db, d), ada_s, db, p, db,
                                                 attend_s, ssd_s, final_g)

    st = lambda a, *shape: a.reshape((1,) + shape)
    return (yp.reshape(batch, seq, d), ys.reshape(db, 1, d),
            st(k_p, batch, seq, N_KV_HEADS, HEAD_DIM), st(v_p, batch, seq, N_KV_HEADS, HEAD_DIM),
            st(ki_p, batch, seq, IDX_DIM),
            st(ssm_p, batch, n_ssd_heads, SSD_HEAD_DIM, D_STATE), st(conv_p, batch, conv_w.shape[1] - 1, conv_dim),
            st(k_s, db, 1, N_KV_HEADS, HEAD_DIM), st(v_s, db, 1, N_KV_HEADS, HEAD_DIM), st(ki_s, db, 1, IDX_DIM),
            st(ssm_s, db, n_ssd_heads, SSD_HEAD_DIM, D_STATE), st(conv_s, db, conv_w.shape[1] - 1, conv_dim))
```

```python
import functools
import math

import numpy as np
import jax
import jax.numpy as jnp
from jax import lax
from jax.experimental import pallas as pl
from jax.experimental.pallas import tpu as pltpu

N_HEADS = 16
HEAD_DIM = 64
N_KV_HEADS = 4
N_IDX_HEADS = 8
IDX_DIM = 64
TOPK_MAX = 256
N_BUCKETS = 32
MAX_DISTANCE = 128
SSD_HEAD_DIM = 64
SSD_GROUPS = 4
D_STATE = 128
SSD_CHUNK = 128
EPS = 1e-6
PAGE_SIZE = 128

LANES = 128
SUBLANES = 8
MXU_TILE = 256
VMEM_LIMIT = 56 * 1024 * 1024

F32 = jnp.float32
BF16 = jnp.bfloat16
I32 = jnp.int32
NEG_INF = float("-inf")
INT_MIN = -(2 ** 31)
LOG2E = math.log2(math.e)

_NT = (((1,), (1,)), ((), ()))


def _dot(a, b):
    return jnp.dot(a, b, preferred_element_type=F32)


def _dot_nt(a, b):
    return lax.dot_general(a, b, _NT, preferred_element_type=F32)


def _split3(a):
    hi = a.astype(BF16)
    r = a - hi.astype(F32)
    mid = r.astype(BF16)
    lo = (r - mid.astype(F32)).astype(BF16)
    return hi, mid, lo


def _dot_exact_rhs(a, b_bf16):
    hi, mid, lo = _split3(a)
    return _dot(hi, b_bf16) + _dot(mid, b_bf16) + _dot(lo, b_bf16)


def _dot_exact_lhs(a_bf16, b):
    hi, mid, lo = _split3(b)
    return _dot(a_bf16, hi) + _dot(a_bf16, mid) + _dot(a_bf16, lo)


def _rmsnorm(x, g):
    return (x * lax.rsqrt(jnp.mean(x * x, axis=-1, keepdims=True) + EPS)) * g


def _sigmoid(x):
    return 0.5 * jnp.tanh(0.5 * x) + 0.5


def _silu(x):
    h = 0.5 * x
    return h * jnp.tanh(h) + h


def _params(*sem):
    return pltpu.CompilerParams(dimension_semantics=sem, vmem_limit_bytes=VMEM_LIMIT)


def _t5_bucket_np(dist):
    n = np.maximum(dist, 0)
    max_exact = N_BUCKETS // 2
    nf = np.maximum(n, 1).astype(np.float32)
    val = (np.log(nf / np.float32(max_exact)) / np.float32(math.log(MAX_DISTANCE / max_exact))
           * np.float32(N_BUCKETS - max_exact)).astype(np.float32)
    frac = np.abs(val - np.round(val))
    knife = (frac < 1e-3) & (n > max_exact) & (val < N_BUCKETS - max_exact - 0.5)
    assert not knife.any()
    large = np.minimum(max_exact + val.astype(np.int32), N_BUCKETS - 1)
    return np.where(n < max_exact, n, large).astype(np.int32)


def _ada_kernel(c_ref, w_ref, b_ref, o_ref):
    h = _silu(c_ref[...]).astype(BF16)
    o_ref[...] = _dot(h, w_ref[...].astype(BF16)) + b_ref[...]


def _ada(c, w, b):
    rows, d = c.shape
    n = w.shape[1]
    tn = 1024
    return pl.pallas_call(
        _ada_kernel,
        grid=(n // tn,),
        in_specs=[pl.BlockSpec((rows, d), lambda j: (0, 0)),
                  pl.BlockSpec((d, tn), lambda j: (0, j)),
                  pl.BlockSpec((1, tn), lambda j: (0, j))],
        out_specs=pl.BlockSpec((rows, tn), lambda j: (0, j)),
        out_shape=jax.ShapeDtypeStruct((rows, n), F32),
        compiler_params=_params("arbitrary"),
    )(c, w, b.reshape(1, n))


class _Mod:
    def __init__(self, arr, rows_per_seq):
        self.arr = arr
        self.rows_per_seq = rows_per_seq

    def spec(self, tm):
        r = self.arr.shape[1]
        d = self.arr.shape[2]
        if r == 1:
            per = self.rows_per_seq // tm
            return pl.BlockSpec((1, 1, d), lambda i, *_: (i // per, 0, 0))
        assert r == tm
        return pl.BlockSpec((1, r, d), lambda i, *_: (i, 0, 0))


def _ffn_kernel(x_ref, sh_ref, sc_ref, gt_ref, g_ref, wi_ref, wo_ref, fg_ref, o_ref, *, final_norm, splits):
    x = x_ref[...]
    f = wo_ref.shape[0]
    h = (_rmsnorm(x, g_ref[...]) * (1.0 + sc_ref[0]) + sh_ref[0]).astype(BF16)
    acc = None
    for lo, hi in splits:
        gate = _dot(h, wi_ref[:, lo:hi])
        up = _dot(h, wi_ref[:, f + lo:f + hi])
        part = _dot((_silu(gate) * up).astype(BF16), wo_ref[lo:hi, :])
        acc = part if acc is None else acc + part
    out = x + 0.5 * gt_ref[0] * acc
    if final_norm:
        out = _rmsnorm(out, fg_ref[...])
    o_ref[...] = out


def _ffn(x, shift, scale, gate, norm_g, w_in, w_out, final_g, *, tm, final_norm):
    m, d = x.shape
    f = w_out.shape[0]
    assert f % MXU_TILE == 0 and m % tm == 0
    mid = (f // MXU_TILE // 2) * MXU_TILE
    splits = ((0, mid), (mid, f))
    row = lambda i: (i, 0)
    const = lambda i: (0, 0)
    resident = lambda shape: pl.BlockSpec(shape, const, pipeline_mode=pl.Buffered(1))
    return pl.pallas_call(
        functools.partial(_ffn_kernel, final_norm=final_norm, splits=splits),
        grid=(m // tm,),
        in_specs=[pl.BlockSpec((tm, d), row),
                  shift.spec(tm), scale.spec(tm), gate.spec(tm),
                  pl.BlockSpec((1, d), const),
                  resident(w_in.shape), resident(w_out.shape),
                  pl.BlockSpec((1, d), const)],
        out_specs=pl.BlockSpec((tm, d), row),
        out_shape=jax.ShapeDtypeStruct((m, d), F32),
        compiler_params=_params("parallel"),
    )(x, shift.arr, scale.arr, gate.arr, norm_g, w_in, w_out, final_g)


def _proj_kernel(x_ref, sh_ref, sc_ref, g_ref, w_ref, *o_refs, offsets, scales):
    h = (_rmsnorm(x_ref[...], g_ref[...]) * (1.0 + sc_ref[0]) + sh_ref[0]).astype(BF16)
    for o_ref, off, scale in zip(o_refs, offsets, scales):
        width = o_ref.shape[1]
        out = _dot(h, w_ref[:, off:off + width])
        if scale != 1.0:
            out = out * scale
        o_ref[...] = out.astype(o_ref.dtype)


def _proj(x, shift, scale, norm_g, w, segments, *, tm):
    m, d = x.shape
    offsets, off = [], 0
    for width, _, _ in segments:
        offsets.append(off)
        off += -(-width // LANES) * LANES
    assert off == w.shape[1]
    row = lambda i: (i, 0)
    const = lambda i: (0, 0)
    return pl.pallas_call(
        functools.partial(_proj_kernel, offsets=tuple(offsets), scales=tuple(s for _, _, s in segments)),
        grid=(m // tm,),
        in_specs=[pl.BlockSpec((tm, d), row), shift.spec(tm), scale.spec(tm),
                  pl.BlockSpec((1, d), const), pl.BlockSpec(w.shape, const, pipeline_mode=pl.Buffered(1))],
        out_specs=[pl.BlockSpec((tm, width), row) for width, _, _ in segments],
        out_shape=[jax.ShapeDtypeStruct((m, width), dt) for width, dt, _ in segments],
        compiler_params=_params("parallel"),
    )(x, shift.arr, scale.arr, norm_g, w)


def _sort_key(score):
    score = jnp.where(score == 0.0, 0.0, score)
    bits = pltpu.bitcast(score, I32)
    return jnp.where(bits >= 0, bits, bits ^ jnp.int32(0x7FFFFFFF))


def _radix_threshold(count_ge, n_top, shape):
    def bit_body(t, carry):
        thr_u, cnt_thr = carry
        cand_u = thr_u | jnp.left_shift(jnp.int32(1), 31 - t)
        cnt = count_ge(cand_u ^ jnp.int32(INT_MIN))
        take = cnt >= n_top
        return jnp.where(take, cand_u, thr_u), jnp.where(take, cnt, cnt_thr)

    thr_u, cnt_thr = lax.fori_loop(0, 32, bit_body, (jnp.zeros(shape, I32), jnp.zeros(shape, I32)))
    return thr_u ^ jnp.int32(INT_MIN), cnt_thr


def _bias_table_kernel(rb_ref, bkt_ref, o_ref):
    far = N_BUCKETS - 1
    for slot in range(bkt_ref.shape[0]):
        bkt = bkt_ref[slot]
        for h in range(N_HEADS):
            acc = jnp.zeros(bkt.shape, F32)
            for b in range(N_BUCKETS - 1):
                acc = jnp.where(bkt == b, (rb_ref[b, h] - rb_ref[far, h]) * LOG2E, acc)
            o_ref[slot, h] = acc


def _bias_table(rel_bias, tq, seq):
    s = np.arange(tq)[:, None]
    t = np.arange(tq)[None, :]
    bkt = np.stack([_t5_bucket_np(t + tq - s), _t5_bucket_np(t - s)])
    assert (_t5_bucket_np(np.arange(tq + 1, seq + 1)) == N_BUCKETS - 1).all()
    return pl.pallas_call(
        _bias_table_kernel,
        in_specs=[pl.BlockSpec(memory_space=pltpu.SMEM), pl.BlockSpec(memory_space=pltpu.VMEM)],
        out_specs=pl.BlockSpec(memory_space=pltpu.VMEM),
        out_shape=jax.ShapeDtypeStruct((2, N_HEADS, tq, tq), F32),
    )(rel_bias, jnp.asarray(bkt))


_V_ROWS = HEAD_DIM + 16


def _dsa_prompt_kernel(q_ref, qi_ref, wi_ref, k_ref, v_ref, ki_ref, bias_ref, o_ref,
                       kb_scr, vt_scr, kib_scr, key_scr, mb_scr, m_scr, al_scr, lg_scr, acc_scr, *, n_top, tq):
    i = pl.program_id(1)
    nq = N_HEADS // N_KV_HEADS
    nblk = key_scr.shape[0]
    s_io = lax.broadcasted_iota(I32, (tq, tq), 0)
    t_io = lax.broadcasted_iota(I32, (tq, tq), 1)
    causal = s_io <= t_io

    @pl.when(i == 0)
    def _():
        kb_scr[...] = k_ref[...].astype(BF16)
        kib_scr[...] = ki_ref[...].astype(BF16)
        for c in range(nblk):
            for pair in range(N_KV_HEADS // 2):
                vt = v_ref[c * tq:(c + 1) * tq, pair * 2 * HEAD_DIM:(pair + 1) * 2 * HEAD_DIM].T.astype(BF16)
                vt_scr[c, 2 * pair, 0:HEAD_DIM, :] = vt[0:HEAD_DIM]
                vt_scr[c, 2 * pair + 1, 0:HEAD_DIM, :] = vt[HEAD_DIM:2 * HEAD_DIM]
            for n in range(N_KV_HEADS):
                vt_scr[c, n, HEAD_DIM:_V_ROWS, :] = jnp.ones((_V_ROWS - HEAD_DIM, tq), BF16)

    def chunk(j):
        return pl.ds(pl.multiple_of(j * tq, tq), tq)

    def fold(a):
        return jnp.sum(a.reshape(tq // SUBLANES, SUBLANES, tq), axis=0)

    qi = qi_ref[...]
    qi_st = jnp.concatenate([qi[:, h * IDX_DIM:(h + 1) * IDX_DIM] for h in range(N_IDX_HEADS)], axis=0)
    wi_t = wi_ref[...].T
    wi_row = jnp.concatenate([wi_t[h:h + 1, :] for h in range(N_IDX_HEADS)], axis=1)
    idx_scale = (N_IDX_HEADS * IDX_DIM) ** -0.5

    def visible(j):
        return (j < i) | ((j == i) & causal)

    n_pairs = (i + 2) // 2

    def score_body(jp, carry):
        for j in (2 * jp, 2 * jp + 1):
            d = jnp.maximum(_dot_nt(kib_scr[chunk(j), :], qi_st), 0.0) * wi_row
            s = d[:, 0:tq]
            for h in range(1, N_IDX_HEADS):
                s = s + d[:, h * tq:(h + 1) * tq]
            key_scr[j] = _sort_key(jnp.where(visible(j), s * idx_scale, NEG_INF))
        return carry

    lax.fori_loop(0, n_pairs, score_body, 0)

    n_quads = (i + 4) // 4

    def park_body(j, carry):
        key_scr[j] = jnp.full((tq, tq), INT_MIN, I32)
        return carry

    lax.fori_loop(2 * n_pairs, 4 * n_quads, park_body, 0)

    def count(pred):
        def body(jq, c):
            for u in range(4):
                c = c + fold(pred(key_scr[4 * jq + u]).astype(I32))
            return c
        c = lax.fori_loop(0, n_quads, body, jnp.zeros((SUBLANES, tq), I32))
        return jnp.sum(c, axis=0, keepdims=True)

    thr, cnt_thr = _radix_threshold(lambda cand: count(lambda key: key >= cand), n_top, (1, tq))

    def sel_plain():
        def body(j, carry):
            mb_scr[j] = jnp.where((key_scr[j] >= thr) & visible(j), 0.0, NEG_INF)
            return carry
        lax.fori_loop(0, i + 1, body, 0)

    def sel_ties():
        need = (n_top - count(lambda key: key > thr)).astype(F32)
        tril = (s_io >= t_io).astype(BF16)

        def body(j, run_eq):
            key = key_scr[j]
            eq = key == thr
            pre = _dot(tril, eq.astype(BF16))
            sel = ((key > thr) | (eq & (run_eq + pre <= need))) & visible(j)
            mb_scr[j] = jnp.where(sel, 0.0, NEG_INF)
            return run_eq + pre[tq - 1:tq, :]
        lax.fori_loop(0, i + 1, body, jnp.zeros((1, tq), F32))

    lax.cond(jnp.max(cnt_thr) > n_top, sel_ties, sel_plain)

    q = q_ref[...]
    q_st = [jnp.concatenate([q[:, (n * nq + g) * HEAD_DIM:(n * nq + g + 1) * HEAD_DIM] for g in range(nq)], axis=0)
            for n in range(N_KV_HEADS)]
    m_scr[...] = jnp.full(m_scr.shape, -1e30, F32)
    acc_scr[...] = jnp.zeros_like(acc_scr)

    heads = range(N_KV_HEADS)

    def stage_a(j, slot):
        mb4 = jnp.concatenate([mb_scr[j]] * nq, axis=1)
        lgs = [_dot_nt(kb_scr[chunk(j), n * HEAD_DIM:(n + 1) * HEAD_DIM], q_st[n]) + mb4 for n in heads]
        if slot is not None:
            lgs = [lgs[n] + jnp.concatenate([bias_ref[slot, n * nq + g] for g in range(nq)], axis=1) for n in heads]
        for n in heads:
            m_old = m_scr[n]
            m_new = jnp.maximum(m_old, jnp.max(lgs[n], axis=0, keepdims=True))
            al_scr[n] = jnp.exp2(m_old - m_new)
            m_scr[n] = m_new
            lg_scr[j % 2, n] = lgs[n]

    def stage_b(j):
        ps = [jnp.exp2(lg_scr[j % 2, n] - m_scr[n]).astype(BF16) for n in heads]
        pvs = [_dot(vt_scr[j, n], ps[n]) for n in heads]
        for n in heads:
            acc_scr[n] = al_scr[n] * acc_scr[n] + pvs[n]

    @pl.when(i >= 2)
    def _():
        stage_a(0, None)

        def far_body(j, carry):
            stage_b(j - 1)
            stage_a(j, None)
            return carry

        lax.fori_loop(1, i - 1, far_body, 0)
        stage_b(i - 2)
        stage_a(i - 1, 0)

    @pl.when(i == 1)
    def _():
        stage_a(0, 0)

    @pl.when(i >= 1)
    def _():
        stage_b(i - 1)
        stage_a(i, 1)

    @pl.when(i == 0)
    def _():
        stage_a(0, 1)

    stage_b(i)

    for n in range(N_KV_HEADS):
        acc = acc_scr[n]
        out_t = acc[0:HEAD_DIM] / acc[HEAD_DIM:HEAD_DIM + 1]
        for pair in range(nq // 2):
            g0 = 2 * pair
            two = jnp.concatenate([out_t[:, g0 * tq:(g0 + 1) * tq], out_t[:, (g0 + 1) * tq:(g0 + 2) * tq]], axis=0)
            h0 = n * nq + g0
            o_ref[:, h0 * HEAD_DIM:(h0 + 2) * HEAD_DIM] = two.T.astype(o_ref.dtype)


def _dsa_prompt(q, qi, wi, k, v, ki, bias_tab, *, batch, seq, tq):
    nblk = seq // tq
    n_top = min(TOPK_MAX, seq // 4)
    nq = N_HEADS // N_KV_HEADS
    assert tq == LANES and nq % 2 == 0 and N_KV_HEADS % 2 == 0 and nblk % 4 == 0
    blk = lambda b, i: (b * nblk + i, 0)
    whole = lambda b, i: (b, 0)
    kvw = N_KV_HEADS * HEAD_DIM
    return pl.pallas_call(
        functools.partial(_dsa_prompt_kernel, n_top=n_top, tq=tq),
        grid=(batch, nblk),
        in_specs=[pl.BlockSpec((tq, N_HEADS * HEAD_DIM), blk),
                  pl.BlockSpec((tq, N_IDX_HEADS * IDX_DIM), blk),
                  pl.BlockSpec((tq, LANES), blk),
                  pl.BlockSpec((seq, kvw), whole),
                  pl.BlockSpec((seq, kvw), whole),
                  pl.BlockSpec((seq, IDX_DIM), whole),
                  pl.BlockSpec(bias_tab.shape, lambda b, i: (0, 0, 0, 0))],
        out_specs=pl.BlockSpec((tq, N_HEADS * HEAD_DIM), blk),
        out_shape=jax.ShapeDtypeStruct((batch * seq, N_HEADS * HEAD_DIM), BF16),
        scratch_shapes=[pltpu.VMEM((seq, kvw), BF16),
                        pltpu.VMEM((nblk, N_KV_HEADS, _V_ROWS, tq), BF16),
                        pltpu.VMEM((seq, IDX_DIM), BF16),
                        pltpu.VMEM((nblk, tq, tq), I32), pltpu.VMEM((nblk, tq, tq), F32),
                        pltpu.VMEM((N_KV_HEADS, 1, nq * tq), F32), pltpu.VMEM((N_KV_HEADS, 1, nq * tq), F32),
                        pltpu.VMEM((2, N_KV_HEADS, tq, nq * tq), F32),
                        pltpu.VMEM((N_KV_HEADS, _V_ROWS, nq * tq), F32)],
        compiler_params=_params("arbitrary", "arbitrary"),
    )(q, qi, wi, k, v, ki, bias_tab)


def _dsa_s_score_kernel(pt_ref, qi_ref, wi_ref, *refs, pg):
    page_refs, o_ref = refs[:pg], refs[pg]
    qi = qi_ref[0].astype(BF16)
    wi = wi_ref[0]
    idx_scale = (N_IDX_HEADS * IDX_DIM) ** -0.5
    for r in range(pg):
        d = _dot_nt(qi, page_refs[r][0])
        o_ref[r, 0] = jnp.sum(jnp.maximum(d, 0.0) * wi, axis=0, keepdims=True) * idx_scale


def _dsa_s_scores(page_table, qi, wi_bc, cache_kidx, *, pg):
    db, n_pages = page_table.shape
    page_spec = lambda r: pl.BlockSpec(
        (1, PAGE_SIZE, IDX_DIM), lambda b, s, pt: (pt[b * n_pages + s * pg + r], 0, 0))
    grid_spec = pltpu.PrefetchScalarGridSpec(
        num_scalar_prefetch=1,
        grid=(db, n_pages // pg),
        in_specs=[pl.BlockSpec((1, N_IDX_HEADS, IDX_DIM), lambda b, s, pt: (b, 0, 0)),
                  pl.BlockSpec((1, N_IDX_HEADS, LANES), lambda b, s, pt: (b, 0, 0))]
                 + [page_spec(r) for r in range(pg)],
        out_specs=pl.BlockSpec((pg, 1, 1, PAGE_SIZE), lambda b, s, pt: (s, b, 0, 0)),
    )
    return pl.pallas_call(
        functools.partial(_dsa_s_score_kernel, pg=pg),
        grid_spec=grid_spec,
        out_shape=jax.ShapeDtypeStruct((n_pages, db, 1, PAGE_SIZE), F32),
        compiler_params=_params("arbitrary", "arbitrary"),
    )(page_table.reshape(-1), qi, wi_bc, *([cache_kidx] * pg))


def _dsa_s_select_kernel(sc_ref, qi_ref, kis_ref, wi_ref, hsum_ref, o_ref, key_scr, *, n_top):
    n_pages, db, _ = sc_ref.shape
    idx_scale = (N_IDX_HEADS * IDX_DIM) ** -0.5

    prod = (qi_ref[...].astype(BF16).astype(F32)
            * jnp.concatenate([kis_ref[...].astype(BF16).astype(F32)] * N_IDX_HEADS, axis=1))
    d_self = _dot_exact_rhs(prod, hsum_ref[...])
    s_self = jnp.sum((jnp.maximum(d_self, 0.0) * wi_ref[...]).T, axis=0, keepdims=True) * idx_scale
    key_self = _sort_key(s_self)

    def to_keys(r, carry):
        key_scr[r] = _sort_key(sc_ref[r].T)
        return carry
    lax.fori_loop(0, n_pages, to_keys, 0)

    def fold(a):
        return jnp.sum(a.reshape(PAGE_SIZE // SUBLANES, SUBLANES, db), axis=0)

    def count(pred):
        def body(rp, c):
            return c + fold(pred(key_scr[2 * rp]).astype(I32)) + fold(pred(key_scr[2 * rp + 1]).astype(I32))
        c = lax.fori_loop(0, n_pages // 2, body, jnp.zeros((SUBLANES, db), I32))
        return jnp.sum(c, axis=0, keepdims=True) + pred(key_self).astype(I32)

    thr, cnt_thr = _radix_threshold(lambda cand: count(lambda key: key >= cand), n_top, (1, db))

    def emit(r, sel):
        o_ref[r] = jnp.where(sel, 0.0, NEG_INF).T

    def sel_plain():
        def body(r, carry):
            emit(r, key_scr[r] >= thr)
            return carry
        lax.fori_loop(0, n_pages, body, 0)
        emit(n_pages, jnp.broadcast_to(key_self >= thr, (PAGE_SIZE, db)))

    def sel_ties():
        need = (n_top - count(lambda key: key > thr)).astype(F32)
        r_io = lax.broadcasted_iota(I32, (PAGE_SIZE, PAGE_SIZE), 0)
        c_io = lax.broadcasted_iota(I32, (PAGE_SIZE, PAGE_SIZE), 1)
        tril = (r_io >= c_io).astype(BF16)

        def body(r, run_eq):
            key = key_scr[r]
            eq = key == thr
            pre = _dot(tril, eq.astype(BF16))
            emit(r, (key > thr) | (eq & (run_eq + pre <= need)))
            return run_eq + pre[PAGE_SIZE - 1:PAGE_SIZE, :]
        run_eq = lax.fori_loop(0, n_pages, body, jnp.zeros((1, db), F32))
        sel_self = (key_self > thr) | ((key_self == thr) & (run_eq + 1.0 <= need))
        emit(n_pages, jnp.broadcast_to(sel_self, (PAGE_SIZE, db)))

    lax.cond(jnp.max(cnt_thr) > n_top, sel_ties, sel_plain)


def _dsa_s_select(scores, qi, ki_s, wi, *, n_top):
    n_pages, db, _ = scores.shape
    assert db == LANES and n_pages % 2 == 0
    hsum = np.zeros((N_IDX_HEADS * IDX_DIM, LANES), np.float32)
    hsum[np.arange(N_IDX_HEADS * IDX_DIM), np.arange(N_IDX_HEADS * IDX_DIM) // IDX_DIM] = 1.0
    return pl.pallas_call(
        functools.partial(_dsa_s_select_kernel, n_top=n_top),
        out_shape=jax.ShapeDtypeStruct((n_pages + 1, db, PAGE_SIZE), F32),
        scratch_shapes=[pltpu.VMEM((n_pages, PAGE_SIZE, db), I32)],
        compiler_params=pltpu.CompilerParams(vmem_limit_bytes=VMEM_LIMIT),
    )(scores, qi, ki_s, wi, jnp.asarray(hsum, BF16))


def _bias_table_s_kernel(rbt_ref, bkt_ref, o_ref):
    rbt = rbt_ref[...] * LOG2E
    for r in range(bkt_ref.shape[0]):
        bkt = bkt_ref[r:r + 1, :]
        out = jnp.zeros((N_HEADS, bkt.shape[1]), F32)
        for b in range(N_BUCKETS):
            out = jnp.where(bkt == b, rbt[:, b:b + 1], out)
        o_ref[r] = out


def _bias_table_s(rel_bias, n_pages):
    past = n_pages * PAGE_SIZE
    pos = np.arange((n_pages + 1) * PAGE_SIZE).reshape(n_pages + 1, PAGE_SIZE)
    pos[n_pages] = past
    return pl.pallas_call(
        _bias_table_s_kernel,
        out_shape=jax.ShapeDtypeStruct((n_pages + 1, N_HEADS, PAGE_SIZE), F32),
    )(rel_bias.T, jnp.asarray(_t5_bucket_np(past - pos)))


def _dsa_s_attend_kernel(pt_ref, mb_ref, bias_ref, q_ref, ks_ref, vs_ref, *refs, pg, n_pages):
    k_refs, v_refs = refs[:pg], refs[pg:2 * pg]
    o_ref, m_scr, l_scr, acc_scr = refs[2 * pg:]
    s = pl.program_id(1)
    nq = N_HEADS // N_KV_HEADS
    kvw = N_KV_HEADS * HEAD_DIM

    @pl.when(s == 0)
    def _():
        m_scr[...] = jnp.full(m_scr.shape, -1e30, F32)
        l_scr[...] = jnp.zeros_like(l_scr)
        acc_scr[...] = jnp.zeros_like(acc_scr)

    q = q_ref[0]
    h_io = lax.broadcasted_iota(I32, (N_HEADS, kvw), 0)
    c_io = lax.broadcasted_iota(I32, (N_HEADS, kvw), 1)
    band = (c_io // HEAD_DIM) == (h_io // nq)
    q_bd = jnp.where(band, jnp.concatenate([q] * N_KV_HEADS, axis=1), 0.0)
    logits = []
    for r in range(pg):
        page = s * pg + r
        logits.append(_dot_nt(q_bd, k_refs[r][0].astype(BF16)) + bias_ref[page] + mb_ref[page, 0])
    m_old = m_scr[...]
    m_new = m_old
    for lg in logits:
        m_new = jnp.maximum(m_new, jnp.max(lg, axis=1, keepdims=True))
    alpha = jnp.exp2(m_old - m_new)
    l_new = alpha * l_scr[...]
    acc = alpha * acc_scr[...]
    for r, lg in enumerate(logits):
        p = jnp.exp2(lg - m_new)
        l_new = l_new + jnp.sum(p, axis=1, keepdims=True)
        acc = acc + _dot(p.astype(BF16), v_refs[r][0].astype(BF16))
    m_scr[...] = m_new
    l_scr[...] = l_new
    acc_scr[...] = acc

    @pl.when(s == pl.num_programs(1) - 1)
    def _():
        ks = ks_ref[0].astype(BF16).astype(F32)
        lg = jnp.sum(q_bd.astype(F32) * ks, axis=1, keepdims=True)
        lg = lg + bias_ref[n_pages][:, 0:1] + mb_ref[n_pages, 0][:, 0:1]
        m_fin = jnp.maximum(m_new, lg)
        a = jnp.exp2(m_new - m_fin)
        p = jnp.exp2(lg - m_fin)
        l_fin = a * l_new + p
        out = (a * acc + p.astype(BF16).astype(F32) * vs_ref[0].astype(BF16).astype(F32)) / l_fin
        out = jnp.where(band, out, 0.0)
        res = out[:, 0:HEAD_DIM]
        for n in range(1, N_KV_HEADS):
            res = res + out[:, n * HEAD_DIM:(n + 1) * HEAD_DIM]
        o_ref[0] = res.astype(o_ref.dtype)


def _dsa_s_attend(page_table, mask, bias_tab, q, k_s, v_s, cache_k, cache_v, *, pg):
    db, n_pages = page_table.shape
    kvw = N_KV_HEADS * HEAD_DIM
    width = PAGE_SIZE
    per_seq = lambda *shape: pl.BlockSpec((1,) + shape, lambda b, s, pt: (b,) + (0,) * len(shape))
    page_spec = lambda r: pl.BlockSpec(
        (1, PAGE_SIZE, kvw), lambda b, s, pt: (pt[b * n_pages + s * pg + r], 0, 0))
    grid_spec = pltpu.PrefetchScalarGridSpec(
        num_scalar_prefetch=1,
        grid=(db, n_pages // pg),
        in_specs=[pl.BlockSpec((n_pages + 1, 1, 1, width), lambda b, s, pt: (0, b, 0, 0)),
                  pl.BlockSpec(bias_tab.shape, lambda b, s, pt: (0, 0, 0)),
                  per_seq(N_HEADS, HEAD_DIM), per_seq(1, kvw), per_seq(1, kvw)]
                 + [page_spec(r) for r in range(pg)] * 2,
        out_specs=per_seq(N_HEADS, HEAD_DIM),
        scratch_shapes=[pltpu.VMEM((N_HEADS, 1), F32), pltpu.VMEM((N_HEADS, 1), F32),
                        pltpu.VMEM((N_HEADS, kvw), F32)],
    )
    return pl.pallas_call(
        functools.partial(_dsa_s_attend_kernel, pg=pg, n_pages=n_pages),
        grid_spec=grid_spec,
        out_shape=jax.ShapeDtypeStruct((db, N_HEADS, HEAD_DIM), BF16),
        compiler_params=_params("arbitrary", "arbitrary"),
    )(page_table.reshape(-1), mask.reshape(n_pages + 1, db, 1, width), bias_tab, q, k_s, v_s,
      *([cache_k] * pg), *([cache_v] * pg))


def _ssd_kernel(xbc_ref, z_ref, dt_ref, cw_ref, cb_ref, alog_ref, dtb_ref, dsk_ref, ng_ref, e_ref, *refs,
                rows, has_init, d_inner):
    if has_init:
        conv0_ref, ssm0_ref = refs[:2]
        refs = refs[2:]
    y_ref, ssm_ref, conv_ref, ext_scr, ht_scr, y_scr = refs
    c = pl.program_id(1)
    cl = SSD_CHUNK
    gw = d_inner // SSD_GROUPS
    hpg = gw // SSD_HEAD_DIM
    n_heads = d_inner // SSD_HEAD_DIM
    conv_dim = d_inner + 2 * SSD_GROUPS * D_STATE
    conv_w = cw_ref.shape[0]
    top = SUBLANES
    row_io = lax.broadcasted_iota(I32, (cl, 1), 0)

    def padded(ref):
        a = ref[0]
        if rows == cl:
            return a
        return jnp.where(row_io < rows, jnp.broadcast_to(a, (cl, a.shape[1])), 0.0)

    @pl.when(c == 0)
    def _():
        ext_scr[0:top, :] = jnp.zeros((top, conv_dim), F32)
        if has_init:
            ext_scr[top - conv_w + 1:top, :] = conv0_ref[0]
            for g in range(SSD_GROUPS):
                ht_scr[g] = ssm0_ref[0, g * gw:(g + 1) * gw, :].T
        else:
            ht_scr[...] = jnp.zeros_like(ht_scr)

    ext_scr[top:top + cl, :] = padded(xbc_ref)
    conv_ref[0] = ext_scr[top + rows - conv_w + 1:top + rows, :]

    cblk = 512
    for cb in range(conv_dim // cblk):
        sl = slice(cb * cblk, (cb + 1) * cblk)
        acc = jnp.broadcast_to(cb_ref[:, sl], (cl, cblk))
        for w in range(conv_w):
            acc = acc + ext_scr[top - conv_w + 1 + w:top - conv_w + 1 + w + cl, sl] * cw_ref[w:w + 1, sl]
        y_scr[:, sl] = _silu(acc)
    ext_scr[top - conv_w + 1:top, :] = ext_scr[top + cl - conv_w + 1:top + cl, :]

    dt = jax.nn.softplus(padded(dt_ref) + dtb_ref[...])
    if rows < cl:
        dt = jnp.where(row_io < rows, dt, 0.0)
    a_neg2 = -jnp.exp(alog_ref[...]) * LOG2E
    r_io = lax.broadcasted_iota(I32, (cl, cl), 0)
    c_io = lax.broadcasted_iota(I32, (cl, cl), 1)
    tril = r_io >= c_io
    acs = _dot_exact_lhs(tril.astype(BF16), dt * a_neg2)
    acs_t = acs.T
    acs_last = acs[cl - 1:cl, :]
    stacked = jnp.concatenate([dt, jnp.exp2(acs), jnp.exp2(acs_last - acs)], axis=0)
    hi = stacked.astype(BF16)
    lo = (stacked - hi.astype(F32)).astype(BF16)
    expanded = _dot(hi, e_ref[...]) + _dot(lo, e_ref[...])
    dt_x, ea_x, te_x = expanded[0:cl], expanded[cl:2 * cl], expanded[2 * cl:3 * cl]

    z = padded(z_ref)
    for g in range(SSD_GROUPS):
        gs = slice(g * gw, (g + 1) * gw)
        x_g = y_scr[:, gs]
        b_g = y_scr[:, d_inner + g * D_STATE:d_inner + (g + 1) * D_STATE]
        c_g = y_scr[:, d_inner + (SSD_GROUPS + g) * D_STATE:d_inner + (SSD_GROUPS + g + 1) * D_STATE]
        c16 = c_g.astype(BF16)
        cbm = _dot_nt(c16, b_g.astype(BF16))
        xdt = x_g * dt_x[:, gs]
        xdt16 = xdt.astype(BF16)
        ht = ht_scr[g]
        y_g = _dot(c16, ht.astype(BF16)) * ea_x[:, gs] + dsk_ref[:, gs] * x_g
        ht_scr[g] = ht * ea_x[cl - 1:cl, gs] + _dot(b_g.T.astype(BF16), (xdt * te_x[:, gs]).astype(BF16))
        diag = []
        for r in range(hpg):
            h = g * hpg + r
            seg = acs[:, h:h + 1] - acs_t[h:h + 1, :]
            m = (cbm * jnp.exp2(jnp.where(tril, seg, NEG_INF))).astype(BF16)
            diag.append(_dot(m, xdt16[:, r * SSD_HEAD_DIM:(r + 1) * SSD_HEAD_DIM]))
        y_g = (y_g + jnp.concatenate(diag, axis=1)) * _silu(z[:, gs])
        y_g = y_g * lax.rsqrt(jnp.mean(y_g * y_g, axis=1, keepdims=True) + EPS) * ng_ref[:, gs]
        y_ref[0, :, gs] = y_g[0:rows].astype(y_ref.dtype)

    @pl.when(c == pl.num_programs(1) - 1)
    def _():
        for g in range(SSD_GROUPS):
            ssm_ref[0, g * gw:(g + 1) * gw, :] = ht_scr[g].T


def _ssd(xbc, z, dt, conv_w, conv_b, a_log, dt_bias, d_skip, norm_g, *, n_seq, n_chunks, rows,
         conv0=None, ssm0=None):
    conv_dim = xbc.shape[-1]
    d_inner = z.shape[-1]
    n_heads = d_inner // SSD_HEAD_DIM
    cw = conv_w.shape[0]
    has_init = conv0 is not None
    pad = lambda a: jnp.pad(a, (0, LANES - a.shape[0])).reshape(1, LANES)
    expand = np.zeros((LANES, d_inner), np.float32)
    expand[np.arange(d_inner) // SSD_HEAD_DIM, np.arange(d_inner)] = 1.0
    step = lambda b, c: (b * n_chunks + c, 0, 0)
    seq = lambda b, c: (b, 0, 0)
    const = lambda b, c: (0, 0)
    in_specs = [pl.BlockSpec((1, rows, conv_dim), step), pl.BlockSpec((1, rows, d_inner), step),
                pl.BlockSpec((1, rows, LANES), step),
                pl.BlockSpec((cw, conv_dim), const), pl.BlockSpec((1, conv_dim), const),
                pl.BlockSpec((1, LANES), const), pl.BlockSpec((1, LANES), const),
                pl.BlockSpec((1, d_inner), const), pl.BlockSpec((1, d_inner), const),
                pl.BlockSpec((LANES, d_inner), const)]
    args = [xbc, z, dt, conv_w, conv_b.reshape(1, conv_dim), pad(a_log), pad(dt_bias),
            jnp.repeat(d_skip, SSD_HEAD_DIM).reshape(1, d_inner), norm_g.reshape(1, d_inner),
            jnp.asarray(expand, BF16)]
    if has_init:
        in_specs += [pl.BlockSpec((1, cw - 1, conv_dim), seq), pl.BlockSpec((1, d_inner, D_STATE), seq)]
        args += [conv0, ssm0]
    return pl.pallas_call(
        functools.partial(_ssd_kernel, rows=rows, has_init=has_init, d_inner=d_inner),
        grid=(n_seq, n_chunks),
        in_specs=in_specs,
        out_specs=[pl.BlockSpec((1, rows, d_inner), step), pl.BlockSpec((1, d_inner, D_STATE), seq),
                   pl.BlockSpec((1, cw - 1, conv_dim), seq)],
        out_shape=[jax.ShapeDtypeStruct((n_seq * n_chunks, rows, d_inner), BF16),
                   jax.ShapeDtypeStruct((n_seq, d_inner, D_STATE), F32),
                   jax.ShapeDtypeStruct((n_seq, cw - 1, conv_dim), F32)],
        scratch_shapes=[pltpu.VMEM((SUBLANES + SSD_CHUNK, conv_dim), F32),
                        pltpu.VMEM((SSD_GROUPS, D_STATE, d_inner // SSD_GROUPS), F32),
                        pltpu.VMEM((SSD_CHUNK, conv_dim), F32)],
        compiler_params=_params("arbitrary", "arbitrary"),
    )(*args)


def _merge_kernel(x_ref, sh_ref, sc_ref, gt_ref, g_ref, att_ref, ssd_ref, wg_ref, wa_ref, ws_ref, wo_ref, o_ref):
    x = x_ref[...]
    d = x.shape[1]
    h = (_rmsnorm(x, g_ref[...]) * (1.0 + sc_ref[0]) + sh_ref[0]).astype(BF16)
    gates = _dot(h, wg_ref[...])
    merged = (_sigmoid(gates[:, 0:d]) * _dot(att_ref[...], wa_ref[...])
              + _sigmoid(gates[:, d:2 * d]) * _dot(ssd_ref[...], ws_ref[...]))
    o_ref[...] = x + gt_ref[0] * _dot(merged.astype(BF16), wo_ref[...])


def _merge(x, shift, scale, gate, norm_g, att, ssd_y, w_g, w_a, w_s, w_o, *, tm):
    m, d = x.shape
    row = lambda i: (i, 0)
    const = lambda i: (0, 0)
    resident = lambda shape: pl.BlockSpec(shape, const, pipeline_mode=pl.Buffered(1))
    return pl.pallas_call(
        _merge_kernel,
        grid=(m // tm,),
        in_specs=[pl.BlockSpec((tm, d), row), shift.spec(tm), scale.spec(tm), gate.spec(tm),
                  pl.BlockSpec((1, d), const),
                  pl.BlockSpec((tm, att.shape[1]), row), pl.BlockSpec((tm, ssd_y.shape[1]), row),
                  resident(w_g.shape), resident(w_a.shape), resident(w_s.shape), resident(w_o.shape)],
        out_specs=pl.BlockSpec((tm, d), row),
        out_shape=jax.ShapeDtypeStruct((m, d), F32),
        compiler_params=_params("parallel"),
    )(x, shift.arr, scale.arr, gate.arr, norm_g, att, ssd_y, w_g, w_a, w_s, w_o)


def _pad_cols(w, width):
    return jnp.pad(w, ((0, 0), (0, width - w.shape[1])))


def _trunk(x, mods, rows_per_seq, p, tm, attend, ssd_fn, final_g):
    sh1, sc1, g1, sh2, sc2, g2, sh3, sc3, g3 = [_Mod(a, rows_per_seq) for a in mods]
    x = _ffn(x, sh1, sc1, g1, p["norm_ffn1"], p["w_ffn1_in"], p["w_ffn1_out"], final_g, tm=tm, final_norm=False)
    q, k, v, qi, ki, wi = _proj(x, sh2, sc2, p["norm_mix"], p["w_att"], p["seg_att"], tm=tm)
    z, xbc, dt = _proj(x, sh2, sc2, p["norm_mix"], p["w_ssd"], p["seg_ssd"], tm=tm)
    att = attend(q, k, v, qi, ki, wi)
    ssd_y, ssm_new, conv_new = ssd_fn(z, xbc, dt)
    x = _merge(x, sh2, sc2, g2, p["norm_mix"], att, ssd_y, p["w_gate"], p["w_attn_out"], p["w_ssd_out"], p["w_out"],
               tm=tm)
    y = _ffn(x, sh3, sc3, g3, p["norm_ffn2"], p["w_ffn2_in"], p["w_ffn2_out"], final_g, tm=tm, final_norm=True)
    return y, (k, v, ki, ssm_new, conv_new)


def kernel(x_prompt, x_sample, c_prompt, c_sample, cache_k, cache_v, cache_kidx, state_ssm, state_conv, page_table,
           w_ada, b_ada, norm_ffn1, w_ffn1_in, w_ffn1_out, norm_mix, w_in, rel_bias, conv_w, conv_b, a_log, dt_bias,
           d_skip, norm_ssd, w_attn_out, w_ssd_out, w_out, norm_ffn2, w_ffn2_in, w_ffn2_out, norm_final):
    depth = w_ada.shape[0]
    assert depth == 1
    batch, seq, d = x_prompt.shape
    db, dec_seq, _ = x_sample.shape
    assert dec_seq == 1
    n_pool = cache_k.shape[1]
    n_pages = page_table.shape[1]
    d_inner = norm_ssd.shape[1]
    conv_dim = conv_w.shape[2]
    n_ssd_heads = d_inner // SSD_HEAD_DIM
    att_q = N_HEADS * HEAD_DIM
    att_kv = N_KV_HEADS * HEAD_DIM
    idx_q = N_IDX_HEADS * IDX_DIM
    l = 0

    widths = (att_q, att_kv, att_kv, idx_q, IDX_DIM, N_IDX_HEADS, d_inner, conv_dim, n_ssd_heads, d, d)
    bounds = np.concatenate([[0], np.cumsum(widths)])
    assert bounds[-1] == w_in.shape[2]
    cols = [w_in[l][:, bounds[i]:bounds[i + 1]].astype(BF16) for i in range(len(widths))]
    w_q, w_k, w_v, w_qi, w_ki, w_wi, w_z, w_xbc, w_dt, w_ga, w_gs = cols
    row1 = lambda a: a.reshape(1, -1)
    p = {
        "norm_ffn1": row1(norm_ffn1[l]), "w_ffn1_in": w_ffn1_in[l].astype(BF16), "w_ffn1_out": w_ffn1_out[l].astype(BF16),
        "norm_mix": row1(norm_mix[l]),
        "w_att": jnp.concatenate([w_q, w_k, w_v, w_qi, _pad_cols(w_ki, LANES), _pad_cols(w_wi, LANES)], axis=1),
        "seg_att": [(att_q, BF16, HEAD_DIM ** -0.5 * LOG2E), (att_kv, F32, 1.0), (att_kv, F32, 1.0),
                    (idx_q, BF16, 1.0), (IDX_DIM, F32, 1.0), (LANES, F32, 1.0)],
        "w_ssd": jnp.concatenate([w_z, w_xbc, _pad_cols(w_dt, LANES)], axis=1),
        "seg_ssd": [(d_inner, F32, 1.0), (conv_dim, F32, 1.0), (LANES, F32, 1.0)],
        "w_gate": jnp.concatenate([w_ga, w_gs], axis=1),
        "w_attn_out": w_attn_out[l].astype(BF16), "w_ssd_out": w_ssd_out[l].astype(BF16), "w_out": w_out[l].astype(BF16),
        "norm_ffn2": row1(norm_ffn2[l]), "w_ffn2_in": w_ffn2_in[l].astype(BF16), "w_ffn2_out": w_ffn2_out[l].astype(BF16),
    }
    final_g = row1(norm_final)
    ssd_args = (conv_w[l], conv_b[l], a_log[l], dt_bias[l], d_skip[l], norm_ssd[l])

    ada = _ada(jnp.concatenate([c_prompt, c_sample], axis=0), w_ada[l], b_ada[l])
    ada_p = [a.reshape(batch, 1, d) for a in jnp.split(ada[:batch], 9, axis=1)]
    ada_s = [a.reshape(1, db, d) for a in jnp.split(ada[batch:], 9, axis=1)]

    tq = 128
    n_chunks = seq // SSD_CHUNK
    bias_tab = _bias_table(rel_bias, tq, seq)

    def attend_p(q, k, v, qi, ki, wi):
        return _dsa_prompt(q, qi, wi, k, v, ki, bias_tab, batch=batch, seq=seq, tq=tq)

    def ssd_p(z, xbc, dt):
        r3 = lambda a: a.reshape(batch * n_chunks, SSD_CHUNK, a.shape[-1])
        y, ssm, conv = _ssd(r3(xbc), r3(z), r3(dt), *ssd_args, n_seq=batch, n_chunks=n_chunks, rows=SSD_CHUNK)
        return y.reshape(batch * seq, d_inner), ssm, conv

    yp, (k_p, v_p, ki_p, ssm_p, conv_p) = _trunk(x_prompt.reshape(batch * seq, d), ada_p, seq, p, 512,
                                                 attend_p, ssd_p, final_g)

    pg = min(32, n_pages)
    n_top_s = min(TOPK_MAX, (n_pages * PAGE_SIZE + 1) // 4)
    bias_tab_s = _bias_table_s(rel_bias, n_pages)

    def attend_s(q, k, v, qi, ki, wi):
        wi_bc = jnp.broadcast_to(wi[:, :N_IDX_HEADS, None], (db, N_IDX_HEADS, LANES))
        scores = _dsa_s_scores(page_table, qi.reshape(db, N_IDX_HEADS, IDX_DIM), wi_bc,
                               cache_kidx[l].astype(BF16), pg=min(2 * pg, n_pages))
        mask = _dsa_s_select(scores.reshape(n_pages, db, PAGE_SIZE), qi, ki, wi, n_top=n_top_s)
        att = _dsa_s_attend(page_table, mask, bias_tab_s, q.reshape(db, N_HEADS, HEAD_DIM),
                            k.reshape(db, 1, att_kv), v.reshape(db, 1, att_kv),
                            cache_k[l].reshape(n_pool, PAGE_SIZE, att_kv),
                            cache_v[l].reshape(n_pool, PAGE_SIZE, att_kv), pg=pg)
        return att.reshape(db, att_q)

    def ssd_s(z, xbc, dt):
        r3 = lambda a: a.reshape(db, 1, a.shape[-1])
        y, ssm, conv = _ssd(r3(xbc), r3(z), r3(dt), *ssd_args, n_seq=db, n_chunks=1, rows=1,
                            conv0=state_conv[l], ssm0=state_ssm[l].reshape(db, d_inner, D_STATE))
        return y.reshape(db, d_inner), ssm, conv

    ys, (k_s, v_s, ki_s, ssm_s, conv_s) = _trunk(x_sample.reshape(db, d), ada_s, db, p, db,
                                                 attend_s, ssd_s, final_g)

    st = lambda a, *shape: a.reshape((1,) + shape)
    return (yp.reshape(batch, seq, d), ys.reshape(db, 1, d),
            st(k_p, batch, seq, N_KV_HEADS, HEAD_DIM), st(v_p, batch, seq, N_KV_HEADS, HEAD_DIM),
            st(ki_p, batch, seq, IDX_DIM),
            st(ssm_p, batch, n_ssd_heads, SSD_HEAD_DIM, D_STATE), st(conv_p, batch, conv_w.shape[1] - 1, conv_dim),
            st(k_s, db, 1, N_KV_HEADS, HEAD_DIM), st(v_s, db, 1, N_KV_HEADS, HEAD_DIM), st(ki_s, db, 1, IDX_DIM),
            st(ssm_s, db, n_ssd_heads, SSD_HEAD_DIM, D_STATE), st(conv_s, db, conv_w.shape[1] - 1, conv_dim))
```

```python
import functools
import math

import numpy as np
import jax
import jax.numpy as jnp
from jax import lax
from jax.experimental import pallas as pl
from jax.experimental.pallas import tpu as pltpu

N_HEADS = 16
HEAD_DIM = 64
N_KV_HEADS = 4
N_IDX_HEADS = 8
IDX_DIM = 64
TOPK_MAX = 256
N_BUCKETS = 32
MAX_DISTANCE = 128
SSD_HEAD_DIM = 64
SSD_GROUPS = 4
D_STATE = 128
SSD_CHUNK = 128
EPS = 1e-6
PAGE_SIZE = 128

LANES = 128
SUBLANES = 8
MXU_TILE = 256
VMEM_LIMIT = 56 * 1024 * 1024

F32 = jnp.float32
BF16 = jnp.bfloat16
I32 = jnp.int32
NEG_INF = float("-inf")
INT_MIN = -(2 ** 31)
LOG2E = math.log2(math.e)

_NT = (((1,), (1,)), ((), ()))


def _dot(a, b):
    return jnp.dot(a, b, preferred_element_type=F32)


def _dot_nt(a, b):
    return lax.dot_general(a, b, _NT, preferred_element_type=F32)


def _split3(a):
    hi = a.astype(BF16)
    r = a - hi.astype(F32)
    mid = r.astype(BF16)
    lo = (r - mid.astype(F32)).astype(BF16)
    return hi, mid, lo


def _dot_exact_rhs(a, b_bf16):
    hi, mid, lo = _split3(a)
    return _dot(hi, b_bf16) + _dot(mid, b_bf16) + _dot(lo, b_bf16)


def _dot_exact_lhs(a_bf16, b):
    hi, mid, lo = _split3(b)
    return _dot(a_bf16, hi) + _dot(a_bf16, mid) + _dot(a_bf16, lo)


def _rmsnorm(x, g):
    return (x * lax.rsqrt(jnp.mean(x * x, axis=-1, keepdims=True) + EPS)) * g


def _sigmoid(x):
    return 0.5 * jnp.tanh(0.5 * x) + 0.5


def _silu(x):
    h = 0.5 * x
    return h * jnp.tanh(h) + h


def _params(*sem):
    return pltpu.CompilerParams(dimension_semantics=sem, vmem_limit_bytes=VMEM_LIMIT)


def _t5_bucket_np(dist):
    n = np.maximum(dist, 0)
    max_exact = N_BUCKETS // 2
    nf = np.maximum(n, 1).astype(np.float32)
    val = (np.log(nf / np.float32(max_exact)) / np.float32(math.log(MAX_DISTANCE / max_exact))
           * np.float32(N_BUCKETS - max_exact)).astype(np.float32)
    frac = np.abs(val - np.round(val))
    knife = (frac < 1e-3) & (n > max_exact) & (val < N_BUCKETS - max_exact - 0.5)
    assert not knife.any()
    large = np.minimum(max_exact + val.astype(np.int32), N_BUCKETS - 1)
    return np.where(n < max_exact, n, large).astype(np.int32)


def _ada_kernel(c_ref, w_ref, b_ref, o_ref):
    h = _silu(c_ref[...]).astype(BF16)
    o_ref[...] = _dot(h, w_ref[...].astype(BF16)) + b_ref[...]


def _ada(c, w, b):
    rows, d = c.shape
    n = w.shape[1]
    tn = 1024
    return pl.pallas_call(
        _ada_kernel,
        grid=(n // tn,),
        in_specs=[pl.BlockSpec((rows, d), lambda j: (0, 0)),
                  pl.BlockSpec((d, tn), lambda j: (0, j)),
                  pl.BlockSpec((1, tn), lambda j: (0, j))],
        out_specs=pl.BlockSpec((rows, tn), lambda j: (0, j)),
        out_shape=jax.ShapeDtypeStruct((rows, n), F32),
        compiler_params=_params("arbitrary"),
    )(c, w, b.reshape(1, n))


class _Mod:
    def __init__(self, arr, rows_per_seq):
        self.arr = arr
        self.rows_per_seq = rows_per_seq

    def spec(self, tm):
        r = self.arr.shape[1]
        d = self.arr.shape[2]
        if r == 1:
            per = self.rows_per_seq // tm
            return pl.BlockSpec((1, 1, d), lambda i, *_: (i // per, 0, 0))
        assert r == tm
        return pl.BlockSpec((1, r, d), lambda i, *_: (i, 0, 0))


def _ffn_kernel(x_ref, sh_ref, sc_ref, gt_ref, g_ref, wi_ref, wo_ref, fg_ref, o_ref, *, final_norm, splits):
    x = x_ref[...]
    f = wo_ref.shape[0]
    h = (_rmsnorm(x, g_ref[...]) * (1.0 + sc_ref[0]) + sh_ref[0]).astype(BF16)
    acc = None
    for lo, hi in splits:
        gate = _dot(h, wi_ref[:, lo:hi])
        up = _dot(h, wi_ref[:, f + lo:f + hi])
        part = _dot((_silu(gate) * up).astype(BF16), wo_ref[lo:hi, :])
        acc = part if acc is None else acc + part
    out = x + 0.5 * gt_ref[0] * acc
    if final_norm:
        out = _rmsnorm(out, fg_ref[...])
    o_ref[...] = out


def _ffn(x, shift, scale, gate, norm_g, w_in, w_out, final_g, *, tm, final_norm):
    m, d = x.shape
    f = w_out.shape[0]
    assert f % MXU_TILE == 0 and m % tm == 0
    mid = (f // MXU_TILE // 2) * MXU_TILE
    splits = ((0, mid), (mid, f))
    row = lambda i: (i, 0)
    const = lambda i: (0, 0)
    resident = lambda shape: pl.BlockSpec(shape, const, pipeline_mode=pl.Buffered(1))
    return pl.pallas_call(
        functools.partial(_ffn_kernel, final_norm=final_norm, splits=splits),
        grid=(m // tm,),
        in_specs=[pl.BlockSpec((tm, d), row),
                  shift.spec(tm), scale.spec(tm), gate.spec(tm),
                  pl.BlockSpec((1, d), const),
                  resident(w_in.shape), resident(w_out.shape),
                  pl.BlockSpec((1, d), const)],
        out_specs=pl.BlockSpec((tm, d), row),
        out_shape=jax.ShapeDtypeStruct((m, d), F32),
        compiler_params=_params("parallel"),
    )(x, shift.arr, scale.arr, gate.arr, norm_g, w_in, w_out, final_g)


def _proj_kernel(x_ref, sh_ref, sc_ref, g_ref, w_ref, *o_refs, offsets, scales):
    h = (_rmsnorm(x_ref[...], g_ref[...]) * (1.0 + sc_ref[0]) + sh_ref[0]).astype(BF16)
    for o_ref, off, scale in zip(o_refs, offsets, scales):
        width = o_ref.shape[1]
        out = _dot(h, w_ref[:, off:off + width])
        if scale != 1.0:
            out = out * scale
        o_ref[...] = out.astype(o_ref.dtype)


def _proj(x, shift, scale, norm_g, w, segments, *, tm):
    m, d = x.shape
    offsets, off = [], 0
    for width, _, _ in segments:
        offsets.append(off)
        off += -(-width // LANES) * LANES
    assert off == w.shape[1]
    row = lambda i: (i, 0)
    const = lambda i: (0, 0)
    return pl.pallas_call(
        functools.partial(_proj_kernel, offsets=tuple(offsets), scales=tuple(s for _, _, s in segments)),
        grid=(m // tm,),
        in_specs=[pl.BlockSpec((tm, d), row), shift.spec(tm), scale.spec(tm),
                  pl.BlockSpec((1, d), const), pl.BlockSpec(w.shape, const, pipeline_mode=pl.Buffered(1))],
        out_specs=[pl.BlockSpec((tm, width), row) for width, _, _ in segments],
        out_shape=[jax.ShapeDtypeStruct((m, width), dt) for width, dt, _ in segments],
        compiler_params=_params("parallel"),
    )(x, shift.arr, scale.arr, norm_g, w)


def _sort_key(score):
    score = jnp.where(score == 0.0, 0.0, score)
    bits = pltpu.bitcast(score, I32)
    return jnp.where(bits >= 0, bits, bits ^ jnp.int32(0x7FFFFFFF))


def _radix_threshold(count_ge, n_top, shape):
    def bit_body(t, carry):
        thr_u, cnt_thr = carry
        cand_u = thr_u | jnp.left_shift(jnp.int32(1), 31 - t)
        cnt = count_ge(cand_u ^ jnp.int32(INT_MIN))
        take = cnt >= n_top
        return jnp.where(take, cand_u, thr_u), jnp.where(take, cnt, cnt_thr)

    thr_u, cnt_thr = lax.fori_loop(0, 32, bit_body, (jnp.zeros(shape, I32), jnp.zeros(shape, I32)))
    return thr_u ^ jnp.int32(INT_MIN), cnt_thr


def _bias_table_kernel(rb_ref, bkt_ref, o_ref):
    far = N_BUCKETS - 1
    for slot in range(bkt_ref.shape[0]):
        bkt = bkt_ref[slot]
        for h in range(N_HEADS):
            acc = jnp.zeros(bkt.shape, F32)
            for b in range(N_BUCKETS - 1):
                acc = jnp.where(bkt == b, (rb_ref[b, h] - rb_ref[far, h]) * LOG2E, acc)
            o_ref[slot, h] = acc


def _bias_table(rel_bias, tq, seq):
    s = np.arange(tq)[:, None]
    t = np.arange(tq)[None, :]
    bkt = np.stack([_t5_bucket_np(t + tq - s), _t5_bucket_np(t - s)])
    assert (_t5_bucket_np(np.arange(tq + 1, seq + 1)) == N_BUCKETS - 1).all()
    return pl.pallas_call(
        _bias_table_kernel,
        in_specs=[pl.BlockSpec(memory_space=pltpu.SMEM), pl.BlockSpec(memory_space=pltpu.VMEM)],
        out_specs=pl.BlockSpec(memory_space=pltpu.VMEM),
        out_shape=jax.ShapeDtypeStruct((2, N_HEADS, tq, tq), F32),
    )(rel_bias, jnp.asarray(bkt))


_V_ROWS = HEAD_DIM + 16


def _dsa_prompt_kernel(q_ref, qi_ref, wi_ref, k_ref, v_ref, ki_ref, bias_ref, o_ref,
                       kb_scr, vt_scr, kib_scr, key_scr, mb_scr, m_scr, al_scr, lg_scr, acc_scr, *, n_top, tq):
    i = pl.program_id(1)
    nq = N_HEADS // N_KV_HEADS
    nblk = key_scr.shape[0]
    s_io = lax.broadcasted_iota(I32, (tq, tq), 0)
    t_io = lax.broadcasted_iota(I32, (tq, tq), 1)
    causal = s_io <= t_io

    @pl.when(i == 0)
    def _():
        kb_scr[...] = k_ref[...].astype(BF16)
        kib_scr[...] = ki_ref[...].astype(BF16)
        for c in range(nblk):
            for pair in range(N_KV_HEADS // 2):
                vt = v_ref[c * tq:(c + 1) * tq, pair * 2 * HEAD_DIM:(pair + 1) * 2 * HEAD_DIM].T.astype(BF16)
                vt_scr[c, 2 * pair, 0:HEAD_DIM, :] = vt[0:HEAD_DIM]
                vt_scr[c, 2 * pair + 1, 0:HEAD_DIM, :] = vt[HEAD_DIM:2 * HEAD_DIM]
            for n in range(N_KV_HEADS):
                vt_scr[c, n, HEAD_DIM:_V_ROWS, :] = jnp.ones((_V_ROWS - HEAD_DIM, tq), BF16)

    def chunk(j):
        return pl.ds(pl.multiple_of(j * tq, tq), tq)

    def fold(a):
        return jnp.sum(a.reshape(tq // SUBLANES, SUBLANES, tq), axis=0)

    qi = qi_ref[...]
    qi_st = jnp.concatenate([qi[:, h * IDX_DIM:(h + 1) * IDX_DIM] for h in range(N_IDX_HEADS)], axis=0)
    wi_t = wi_ref[...].T
    wi_row = jnp.concatenate([wi_t[h:h + 1, :] for h in range(N_IDX_HEADS)], axis=1)
    idx_scale = (N_IDX_HEADS * IDX_DIM) ** -0.5

    def visible(j):
        return (j < i) | ((j == i) & causal)

    n_pairs = (i + 2) // 2

    def score_body(jp, carry):
        for j in (2 * jp, 2 * jp + 1):
            d = jnp.maximum(_dot_nt(kib_scr[chunk(j), :], qi_st), 0.0) * wi_row
            s = d[:, 0:tq]
            for h in range(1, N_IDX_HEADS):
                s = s + d[:, h * tq:(h + 1) * tq]
            key_scr[j] = _sort_key(jnp.where(visible(j), s * idx_scale, NEG_INF))
        return carry

    lax.fori_loop(0, n_pairs, score_body, 0)

    n_quads = (i + 4) // 4

    def park_body(j, carry):
        key_scr[j] = jnp.full((tq, tq), INT_MIN, I32)
        return carry

    lax.fori_loop(2 * n_pairs, 4 * n_quads, park_body, 0)

    def count(pred):
        def body(jq, c):
            for u in range(4):
                c = c + fold(pred(key_scr[4 * jq + u]).astype(I32))
            return c
        c = lax.fori_loop(0, n_quads, body, jnp.zeros((SUBLANES, tq), I32))
        return jnp.sum(c, axis=0, keepdims=True)

    thr, cnt_thr = _radix_threshold(lambda cand: count(lambda key: key >= cand), n_top, (1, tq))

    def sel_plain():
        def body(j, carry):
            mb_scr[j] = jnp.where((key_scr[j] >= thr) & visible(j), 0.0, NEG_INF)
            return carry
        lax.fori_loop(0, i + 1, body, 0)

    def sel_ties():
        need = (n_top - count(lambda key: key > thr)).astype(F32)
        tril = (s_io >= t_io).astype(BF16)

        def body(j, run_eq):
            key = key_scr[j]
            eq = key == thr
            pre = _dot(tril, eq.astype(BF16))
            sel = ((key > thr) | (eq & (run_eq + pre <= need))) & visible(j)
            mb_scr[j] = jnp.where(sel, 0.0, NEG_INF)
            return run_eq + pre[tq - 1:tq, :]
        lax.fori_loop(0, i + 1, body, jnp.zeros((1, tq), F32))

    lax.cond(jnp.max(cnt_thr) > n_top, sel_ties, sel_plain)

    q = q_ref[...]
    q_st = [jnp.concatenate([q[:, (n * nq + g) * HEAD_DIM:(n * nq + g + 1) * HEAD_DIM] for g in range(nq)], axis=0)
            for n in range(N_KV_HEADS)]
    m_scr[...] = jnp.full(m_scr.shape, -1e30, F32)
    acc_scr[...] = jnp.zeros_like(acc_scr)

    heads = range(N_KV_HEADS)

    def stage_a(j, slot):
        mb4 = jnp.concatenate([mb_scr[j]] * nq, axis=1)
        lgs = [_dot_nt(kb_scr[chunk(j), n * HEAD_DIM:(n + 1) * HEAD_DIM], q_st[n]) + mb4 for n in heads]
        if slot is not None:
            lgs = [lgs[n] + jnp.concatenate([bias_ref[slot, n * nq + g] for g in range(nq)], axis=1) for n in heads]
        for n in heads:
            m_old = m_scr[n]
            m_new = jnp.maximum(m_old, jnp.max(lgs[n], axis=0, keepdims=True))
            al_scr[n] = jnp.exp2(m_old - m_new)
            m_scr[n] = m_new
            lg_scr[j % 2, n] = lgs[n]

    def stage_b(j):
        ps = [jnp.exp2(lg_scr[j % 2, n] - m_scr[n]).astype(BF16) for n in heads]
        pvs = [_dot(vt_scr[j, n], ps[n]) for n in heads]
        for n in heads:
            acc_scr[n] = al_scr[n] * acc_scr[n] + pvs[n]

    @pl.when(i >= 2)
    def _():
        stage_a(0, None)

        def far_body(j, carry):
            stage_b(j - 1)
            stage_a(j, None)
            return carry

        lax.fori_loop(1, i - 1, far_body, 0)
        stage_b(i - 2)
        stage_a(i - 1, 0)

    @pl.when(i == 1)
    def _():
        stage_a(0, 0)

    @pl.when(i >= 1)
    def _():
        stage_b(i - 1)
        stage_a(i, 1)

    @pl.when(i == 0)
    def _():
        stage_a(0, 1)

    stage_b(i)

    for n in range(N_KV_HEADS):
        acc = acc_scr[n]
        out_t = acc[0:HEAD_DIM] / acc[HEAD_DIM:HEAD_DIM + 1]
        for pair in range(nq // 2):
            g0 = 2 * pair
            two = jnp.concatenate([out_t[:, g0 * tq:(g0 + 1) * tq], out_t[:, (g0 + 1) * tq:(g0 + 2) * tq]], axis=0)
            h0 = n * nq + g0
            o_ref[:, h0 * HEAD_DIM:(h0 + 2) * HEAD_DIM] = two.T.astype(o_ref.dtype)


def _dsa_prompt(q, qi, wi, k, v, ki, bias_tab, *, batch, seq, tq):
    nblk = seq // tq
    n_top = min(TOPK_MAX, seq // 4)
    nq = N_HEADS // N_KV_HEADS
    assert tq == LANES and nq % 2 == 0 and N_KV_HEADS % 2 == 0 and nblk % 4 == 0
    blk = lambda b, i: (b * nblk + i, 0)
    whole = lambda b, i: (b, 0)
    kvw = N_KV_HEADS * HEAD_DIM
    return pl.pallas_call(
        functools.partial(_dsa_prompt_kernel, n_top=n_top, tq=tq),
        grid=(batch, nblk),
        in_specs=[pl.BlockSpec((tq, N_HEADS * HEAD_DIM), blk),
                  pl.BlockSpec((tq, N_IDX_HEADS * IDX_DIM), blk),
                  pl.BlockSpec((tq, LANES), blk),
                  pl.BlockSpec((seq, kvw), whole),
                  pl.BlockSpec((seq, kvw), whole),
                  pl.BlockSpec((seq, IDX_DIM), whole),
                  pl.BlockSpec(bias_tab.shape, lambda b, i: (0, 0, 0, 0))],
        out_specs=pl.BlockSpec((tq, N_HEADS * HEAD_DIM), blk),
        out_shape=jax.ShapeDtypeStruct((batch * seq, N_HEADS * HEAD_DIM), BF16),
        scratch_shapes=[pltpu.VMEM((seq, kvw), BF16),
                        pltpu.VMEM((nblk, N_KV_HEADS, _V_ROWS, tq), BF16),
                        pltpu.VMEM((seq, IDX_DIM), BF16),
                        pltpu.VMEM((nblk, tq, tq), I32), pltpu.VMEM((nblk, tq, tq), F32),
                        pltpu.VMEM((N_KV_HEADS, 1, nq * tq), F32), pltpu.VMEM((N_KV_HEADS, 1, nq * tq), F32),
                        pltpu.VMEM((2, N_KV_HEADS, tq, nq * tq), F32),
                        pltpu.VMEM((N_KV_HEADS, _V_ROWS, nq * tq), F32)],
        compiler_params=_params("arbitrary", "arbitrary"),
    )(q, qi, wi, k, v, ki, bias_tab)


def _dsa_s_score_kernel(pt_ref, qi_ref, wi_ref, *refs, pg):
    page_refs, o_ref = refs[:pg], refs[pg]
    qi = qi_ref[0].astype(BF16)
    wi = wi_ref[0]
    idx_scale = (N_IDX_HEADS * IDX_DIM) ** -0.5
    for r in range(pg):
        d = _dot_nt(qi, page_refs[r][0])
        o_ref[r, 0] = jnp.sum(jnp.maximum(d, 0.0) * wi, axis=0, keepdims=True) * idx_scale


def _dsa_s_scores(page_table, qi, wi_bc, cache_kidx, *, pg):
    db, n_pages = page_table.shape
    page_spec = lambda r: pl.BlockSpec(
        (1, PAGE_SIZE, IDX_DIM), lambda b, s, pt: (pt[b * n_pages + s * pg + r], 0, 0))
    grid_spec = pltpu.PrefetchScalarGridSpec(
        num_scalar_prefetch=1,
        grid=(db, n_pages // pg),
        in_specs=[pl.BlockSpec((1, N_IDX_HEADS, IDX_DIM), lambda b, s, pt: (b, 0, 0)),
                  pl.BlockSpec((1, N_IDX_HEADS, LANES), lambda b, s, pt: (b, 0, 0))]
                 + [page_spec(r) for r in range(pg)],
        out_specs=pl.BlockSpec((pg, 1, 1, PAGE_SIZE), lambda b, s, pt: (s, b, 0, 0)),
    )
    return pl.pallas_call(
        functools.partial(_dsa_s_score_kernel, pg=pg),
        grid_spec=grid_spec,
        out_shape=jax.ShapeDtypeStruct((n_pages, db, 1, PAGE_SIZE), F32),
        compiler_params=_params("arbitrary", "arbitrary"),
    )(page_table.reshape(-1), qi, wi_bc, *([cache_kidx] * pg))


def _dsa_s_select_kernel(sc_ref, qi_ref, kis_ref, wi_ref, hsum_ref, o_ref, key_scr, *, n_top):
    n_pages, db, _ = sc_ref.shape
    idx_scale = (N_IDX_HEADS * IDX_DIM) ** -0.5

    prod = (qi_ref[...].astype(BF16).astype(F32)
            * jnp.concatenate([kis_ref[...].astype(BF16).astype(F32)] * N_IDX_HEADS, axis=1))
    d_self = _dot_exact_rhs(prod, hsum_ref[...])
    s_self = jnp.sum((jnp.maximum(d_self, 0.0) * wi_ref[...]).T, axis=0, keepdims=True) * idx_scale
    key_self = _sort_key(s_self)

    def to_keys(r, carry):
        key_scr[r] = _sort_key(sc_ref[r].T)
        return carry
    lax.fori_loop(0, n_pages, to_keys, 0)

    def fold(a):
        return jnp.sum(a.reshape(PAGE_SIZE // SUBLANES, SUBLANES, db), axis=0)

    def count(pred):
        def body(rp, c):
            return c + fold(pred(key_scr[2 * rp]).astype(I32)) + fold(pred(key_scr[2 * rp + 1]).astype(I32))
        c = lax.fori_loop(0, n_pages // 2, body, jnp.zeros((SUBLANES, db), I32))
        return jnp.sum(c, axis=0, keepdims=True) + pred(key_self).astype(I32)

    thr, cnt_thr = _radix_threshold(lambda cand: count(lambda key: key >= cand), n_top, (1, db))

    def emit(r, sel):
        o_ref[r] = jnp.where(sel, 0.0, NEG_INF).T

    def sel_plain():
        def body(r, carry):
            emit(r, key_scr[r] >= thr)
            return carry
        lax.fori_loop(0, n_pages, body, 0)
        emit(n_pages, jnp.broadcast_to(key_self >= thr, (PAGE_SIZE, db)))

    def sel_ties():
        need = (n_top - count(lambda key: key > thr)).astype(F32)
        r_io = lax.broadcasted_iota(I32, (PAGE_SIZE, PAGE_SIZE), 0)
        c_io = lax.broadcasted_iota(I32, (PAGE_SIZE, PAGE_SIZE), 1)
        tril = (r_io >= c_io).astype(BF16)

        def body(r, run_eq):
            key = key_scr[r]
            eq = key == thr
            pre = _dot(tril, eq.astype(BF16))
            emit(r, (key > thr) | (eq & (run_eq + pre <= need)))
            return run_eq + pre[PAGE_SIZE - 1:PAGE_SIZE, :]
        run_eq = lax.fori_loop(0, n_pages, body, jnp.zeros((1, db), F32))
        sel_self = (key_self > thr) | ((key_self == thr) & (run_eq + 1.0 <= need))
        emit(n_pages, jnp.broadcast_to(sel_self, (PAGE_SIZE, db)))

    lax.cond(jnp.max(cnt_thr) > n_top, sel_ties, sel_plain)


def _dsa_s_select(scores, qi, ki_s, wi, *, n_top):
    n_pages, db, _ = scores.shape
    assert db == LANES and n_pages % 2 == 0
    hsum = np.zeros((N_IDX_HEADS * IDX_DIM, LANES), np.float32)
    hsum[np.arange(N_IDX_HEADS * IDX_DIM), np.arange(N_IDX_HEADS * IDX_DIM) // IDX_DIM] = 1.0
    return pl.pallas_call(
        functools.partial(_dsa_s_select_kernel, n_top=n_top),
        out_shape=jax.ShapeDtypeStruct((n_pages + 1, db, PAGE_SIZE), F32),
        scratch_shapes=[pltpu.VMEM((n_pages, PAGE_SIZE, db), I32)],
        compiler_params=pltpu.CompilerParams(vmem_limit_bytes=VMEM_LIMIT),
    )(scores, qi, ki_s, wi, jnp.asarray(hsum, BF16))


def _bias_table_s_kernel(rbt_ref, bkt_ref, o_ref):
    rbt = rbt_ref[...] * LOG2E
    for r in range(bkt_ref.shape[0]):
        bkt = bkt_ref[r:r + 1, :]
        out = jnp.zeros((N_HEADS, bkt.shape[1]), F32)
        for b in range(N_BUCKETS):
            out = jnp.where(bkt == b, rbt[:, b:b + 1], out)
        o_ref[r] = out


def _bias_table_s(rel_bias, n_pages):
    past = n_pages * PAGE_SIZE
    pos = np.arange((n_pages + 1) * PAGE_SIZE).reshape(n_pages + 1, PAGE_SIZE)
    pos[n_pages] = past
    return pl.pallas_call(
        _bias_table_s_kernel,
        out_shape=jax.ShapeDtypeStruct((n_pages + 1, N_HEADS, PAGE_SIZE), F32),
    )(rel_bias.T, jnp.asarray(_t5_bucket_np(past - pos)))


def _dsa_s_attend_kernel(pt_ref, mb_ref, bias_ref, q_ref, ks_ref, vs_ref, *refs, pg, n_pages):
    k_refs, v_refs = refs[:pg], refs[pg:2 * pg]
    o_ref, m_scr, l_scr, acc_scr = refs[2 * pg:]
    s = pl.program_id(1)
    nq = N_HEADS // N_KV_HEADS
    kvw = N_KV_HEADS * HEAD_DIM

    @pl.when(s == 0)
    def _():
        m_scr[...] = jnp.full(m_scr.shape, -1e30, F32)
        l_scr[...] = jnp.zeros_like(l_scr)
        acc_scr[...] = jnp.zeros_like(acc_scr)

    q = q_ref[0]
    h_io = lax.broadcasted_iota(I32, (N_HEADS, kvw), 0)
    c_io = lax.broadcasted_iota(I32, (N_HEADS, kvw), 1)
    band = (c_io // HEAD_DIM) == (h_io // nq)
    q_bd = jnp.where(band, jnp.concatenate([q] * N_KV_HEADS, axis=1), 0.0)
    logits = []
    for r in range(pg):
        page = s * pg + r
        logits.append(_dot_nt(q_bd, k_refs[r][0].astype(BF16)) + bias_ref[page] + mb_ref[page, 0])
    m_old = m_scr[...]
    m_new = m_old
    for lg in logits:
        m_new = jnp.maximum(m_new, jnp.max(lg, axis=1, keepdims=True))
    alpha = jnp.exp2(m_old - m_new)
    l_new = alpha * l_scr[...]
    acc = alpha * acc_scr[...]
    for r, lg in enumerate(logits):
        p = jnp.exp2(lg - m_new)
        l_new = l_new + jnp.sum(p, axis=1, keepdims=True)
        acc = acc + _dot(p.astype(BF16), v_refs[r][0].astype(BF16))
    m_scr[...] = m_new
    l_scr[...] = l_new
    acc_scr[...] = acc

    @pl.when(s == pl.num_programs(1) - 1)
    def _():
        ks = ks_ref[0].astype(BF16).astype(F32)
        lg = jnp.sum(q_bd.astype(F32) * ks, axis=1, keepdims=True)
        lg = lg + bias_ref[n_pages][:, 0:1] + mb_ref[n_pages, 0][:, 0:1]
        m_fin = jnp.maximum(m_new, lg)
        a = jnp.exp2(m_new - m_fin)
        p = jnp.exp2(lg - m_fin)
        l_fin = a * l_new + p
        out = (a * acc + p.astype(BF16).astype(F32) * vs_ref[0].astype(BF16).astype(F32)) / l_fin
        out = jnp.where(band, out, 0.0)
        res = out[:, 0:HEAD_DIM]
        for n in range(1, N_KV_HEADS):
            res = res + out[:, n * HEAD_DIM:(n + 1) * HEAD_DIM]
        o_ref[0] = res.astype(o_ref.dtype)


def _dsa_s_attend(page_table, mask, bias_tab, q, k_s, v_s, cache_k, cache_v, *, pg):
    db, n_pages = page_table.shape
    kvw = N_KV_HEADS * HEAD_DIM
    width = PAGE_SIZE
    per_seq = lambda *shape: pl.BlockSpec((1,) + shape, lambda b, s, pt: (b,) + (0,) * len(shape))
    page_spec = lambda r: pl.BlockSpec(
        (1, PAGE_SIZE, kvw), lambda b, s, pt: (pt[b * n_pages + s * pg + r], 0, 0))
    grid_spec = pltpu.PrefetchScalarGridSpec(
        num_scalar_prefetch=1,
        grid=(db, n_pages // pg),
        in_specs=[pl.BlockSpec((n_pages + 1, 1, 1, width), lambda b, s, pt: (0, b, 0, 0)),
                  pl.BlockSpec(bias_tab.shape, lambda b, s, pt: (0, 0, 0)),
                  per_seq(N_HEADS, HEAD_DIM), per_seq(1, kvw), per_seq(1, kvw)]
                 + [page_spec(r) for r in range(pg)] * 2,
        out_specs=per_seq(N_HEADS, HEAD_DIM),
        scratch_shapes=[pltpu.VMEM((N_HEADS, 1), F32), pltpu.VMEM((N_HEADS, 1), F32),
                        pltpu.VMEM((N_HEADS, kvw), F32)],
    )
    return pl.pallas_call(
        functools.partial(_dsa_s_attend_kernel, pg=pg, n_pages=n_pages),
        grid_spec=grid_spec,
        out_shape=jax.ShapeDtypeStruct((db, N_HEADS, HEAD_DIM), BF16),
        compiler_params=_params("arbitrary", "arbitrary"),
    )(page_table.reshape(-1), mask.reshape(n_pages + 1, db, 1, width), bias_tab, q, k_s, v_s,
      *([cache_k] * pg), *([cache_v] * pg))


def _ssd_kernel(xbc_ref, z_ref, dt_ref, cw_ref, cb_ref, alog_ref, dtb_ref, dsk_ref, ng_ref, e_ref, *refs,
                rows, has_init, d_inner):
    if has_init:
        conv0_ref, ssm0_ref = refs[:2]
        refs = refs[2:]
    y_ref, ssm_ref, conv_ref, ext_scr, ht_scr, y_scr = refs
    c = pl.program_id(1)
    cl = SSD_CHUNK
    gw = d_inner // SSD_GROUPS
    hpg = gw // SSD_HEAD_DIM
    n_heads = d_inner // SSD_HEAD_DIM
    conv_dim = d_inner + 2 * SSD_GROUPS * D_STATE
    conv_w = cw_ref.shape[0]
    top = SUBLANES
    row_io = lax.broadcasted_iota(I32, (cl, 1), 0)

    def padded(ref):
        a = ref[0]
        if rows == cl:
            return a
        return jnp.where(row_io < rows, jnp.broadcast_to(a, (cl, a.shape[1])), 0.0)

    @pl.when(c == 0)
    def _():
        ext_scr[0:top, :] = jnp.zeros((top, conv_dim), F32)
        if has_init:
            ext_scr[top - conv_w + 1:top, :] = conv0_ref[0]
            for g in range(SSD_GROUPS):
                ht_scr[g] = ssm0_ref[0, g * gw:(g + 1) * gw, :].T
        else:
            ht_scr[...] = jnp.zeros_like(ht_scr)

    ext_scr[top:top + cl, :] = padded(xbc_ref)
    conv_ref[0] = ext_scr[top + rows - conv_w + 1:top + rows, :]

    cblk = 512
    for cb in range(conv_dim // cblk):
        sl = slice(cb * cblk, (cb + 1) * cblk)
        acc = jnp.broadcast_to(cb_ref[:, sl], (cl, cblk))
        for w in range(conv_w):
            acc = acc + ext_scr[top - conv_w + 1 + w:top - conv_w + 1 + w + cl, sl] * cw_ref[w:w + 1, sl]
        y_scr[:, sl] = _silu(acc)
    ext_scr[top - conv_w + 1:top, :] = ext_scr[top + cl - conv_w + 1:top + cl, :]

    dt = jax.nn.softplus(padded(dt_ref) + dtb_ref[...])
    if rows < cl:
        dt = jnp.where(row_io < rows, dt, 0.0)
    a_neg2 = -jnp.exp(alog_ref[...]) * LOG2E
    r_io = lax.broadcasted_iota(I32, (cl, cl), 0)
    c_io = lax.broadcasted_iota(I32, (cl, cl), 1)
    tril = r_io >= c_io
    acs = _dot_exact_lhs(tril.astype(BF16), dt * a_neg2)
    acs_t = acs.T
    acs_last = acs[cl - 1:cl, :]
    stacked = jnp.concatenate([dt, jnp.exp2(acs), jnp.exp2(acs_last - acs)], axis=0)
    hi = stacked.astype(BF16)
    lo = (stacked - hi.astype(F32)).astype(BF16)
    expanded = _dot(hi, e_ref[...]) + _dot(lo, e_ref[...])
    dt_x, ea_x, te_x = expanded[0:cl], expanded[cl:2 * cl], expanded[2 * cl:3 * cl]

    z = padded(z_ref)
    for g in range(SSD_GROUPS):
        gs = slice(g * gw, (g + 1) * gw)
        x_g = y_scr[:, gs]
        b_g = y_scr[:, d_inner + g * D_STATE:d_inner + (g + 1) * D_STATE]
        c_g = y_scr[:, d_inner + (SSD_GROUPS + g) * D_STATE:d_inner + (SSD_GROUPS + g + 1) * D_STATE]
        c16 = c_g.astype(BF16)
        cbm = _dot_nt(c16, b_g.astype(BF16))
        xdt = x_g * dt_x[:, gs]
        xdt16 = xdt.astype(BF16)
        ht = ht_scr[g]
        y_g = _dot(c16, ht.astype(BF16)) * ea_x[:, gs] + dsk_ref[:, gs] * x_g
        ht_scr[g] = ht * ea_x[cl - 1:cl, gs] + _dot(b_g.T.astype(BF16), (xdt * te_x[:, gs]).astype(BF16))
        if rows == 1:
            y_diag = cbm[0:1, 0:1] * xdt16.astype(F32)
        else:
            diag = []
            for r in range(hpg):
                h = g * hpg + r
                seg = acs[:, h:h + 1] - acs_t[h:h + 1, :]
                m = (cbm * jnp.exp2(jnp.where(tril, seg, NEG_INF))).astype(BF16)
                diag.append(_dot(m, xdt16[:, r * SSD_HEAD_DIM:(r + 1) * SSD_HEAD_DIM]))
            y_diag = jnp.concatenate(diag, axis=1)
        y_g = (y_g + y_diag) * _silu(z[:, gs])
        y_g = y_g * lax.rsqrt(jnp.mean(y_g * y_g, axis=1, keepdims=True) + EPS) * ng_ref[:, gs]
        y_ref[0, :, gs] = y_g[0:rows].astype(y_ref.dtype)

    @pl.when(c == pl.num_programs(1) - 1)
    def _():
        for g in range(SSD_GROUPS):
            ssm_ref[0, g * gw:(g + 1) * gw, :] = ht_scr[g].T


def _ssd(xbc, z, dt, conv_w, conv_b, a_log, dt_bias, d_skip, norm_g, *, n_seq, n_chunks, rows,
         conv0=None, ssm0=None):
    conv_dim = xbc.shape[-1]
    d_inner = z.shape[-1]
    n_heads = d_inner // SSD_HEAD_DIM
    cw = conv_w.shape[0]
    has_init = conv0 is not None
    pad = lambda a: jnp.pad(a, (0, LANES - a.shape[0])).reshape(1, LANES)
    expand = np.zeros((LANES, d_inner), np.float32)
    expand[np.arange(d_inner) // SSD_HEAD_DIM, np.arange(d_inner)] = 1.0
    step = lambda b, c: (b * n_chunks + c, 0, 0)
    seq = lambda b, c: (b, 0, 0)
    const = lambda b, c: (0, 0)
    in_specs = [pl.BlockSpec((1, rows, conv_dim), step), pl.BlockSpec((1, rows, d_inner), step),
                pl.BlockSpec((1, rows, LANES), step),
                pl.BlockSpec((cw, conv_dim), const), pl.BlockSpec((1, conv_dim), const),
                pl.BlockSpec((1, LANES), const), pl.BlockSpec((1, LANES), const),
                pl.BlockSpec((1, d_inner), const), pl.BlockSpec((1, d_inner), const),
                pl.BlockSpec((LANES, d_inner), const)]
    args = [xbc, z, dt, conv_w, conv_b.reshape(1, conv_dim), pad(a_log), pad(dt_bias),
            jnp.repeat(d_skip, SSD_HEAD_DIM).reshape(1, d_inner), norm_g.reshape(1, d_inner),
            jnp.asarray(expand, BF16)]
    if has_init:
        in_specs += [pl.BlockSpec((1, cw - 1, conv_dim), seq), pl.BlockSpec((1, d_inner, D_STATE), seq)]
        args += [conv0, ssm0]
    return pl.pallas_call(
        functools.partial(_ssd_kernel, rows=rows, has_init=has_init, d_inner=d_inner),
        grid=(n_seq, n_chunks),
        in_specs=in_specs,
        out_specs=[pl.BlockSpec((1, rows, d_inner), step), pl.BlockSpec((1, d_inner, D_STATE), seq),
                   pl.BlockSpec((1, cw - 1, conv_dim), seq)],
        out_shape=[jax.ShapeDtypeStruct((n_seq * n_chunks, rows, d_inner), BF16),
                   jax.ShapeDtypeStruct((n_seq, d_inner, D_STATE), F32),
                   jax.ShapeDtypeStruct((n_seq, cw - 1, conv_dim), F32)],
        scratch_shapes=[pltpu.VMEM((SUBLANES + SSD_CHUNK, conv_dim), F32),
                        pltpu.VMEM((SSD_GROUPS, D_STATE, d_inner // SSD_GROUPS), F32),
                        pltpu.VMEM((SSD_CHUNK, conv_dim), F32)],
        compiler_params=_params("arbitrary", "arbitrary"),
    )(*args)


def _merge_kernel(x_ref, sh_ref, sc_ref, gt_ref, g_ref, att_ref, ssd_ref, wg_ref, wa_ref, ws_ref, wo_ref, o_ref):
    x = x_ref[...]
    d = x.shape[1]
    h = (_rmsnorm(x, g_ref[...]) * (1.0 + sc_ref[0]) + sh_ref[0]).astype(BF16)
    gates = _dot(h, wg_ref[...])
    merged = (_sigmoid(gates[:, 0:d]) * _dot(att_ref[...], wa_ref[...])
              + _sigmoid(gates[:, d:2 * d]) * _dot(ssd_ref[...], ws_ref[...]))
    o_ref[...] = x + gt_ref[0] * _dot(merged.astype(BF16), wo_ref[...])


def _merge(x, shift, scale, gate, norm_g, att, ssd_y, w_g, w_a, w_s, w_o, *, tm):
    m, d = x.shape
    row = lambda i: (i, 0)
    const = lambda i: (0, 0)
    resident = lambda shape: pl.BlockSpec(shape, const, pipeline_mode=pl.Buffered(1))
    return pl.pallas_call(
        _merge_kernel,
        grid=(m // tm,),
        in_specs=[pl.BlockSpec((tm, d), row), shift.spec(tm), scale.spec(tm), gate.spec(tm),
                  pl.BlockSpec((1, d), const),
                  pl.BlockSpec((tm, att.shape[1]), row), pl.BlockSpec((tm, ssd_y.shape[1]), row),
                  resident(w_g.shape), resident(w_a.shape), resident(w_s.shape), resident(w_o.shape)],
        out_specs=pl.BlockSpec((tm, d), row),
        out_shape=jax.ShapeDtypeStruct((m, d), F32),
        compiler_params=_params("parallel"),
    )(x, shift.arr, scale.arr, gate.arr, norm_g, att, ssd_y, w_g, w_a, w_s, w_o)


def _pad_cols(w, width):
    return jnp.pad(w, ((0, 0), (0, width - w.shape[1])))


def _trunk(x, mods, rows_per_seq, p, tm, attend, ssd_fn, final_g):
    sh1, sc1, g1, sh2, sc2, g2, sh3, sc3, g3 = [_Mod(a, rows_per_seq) for a in mods]
    x = _ffn(x, sh1, sc1, g1, p["norm_ffn1"], p["w_ffn1_in"], p["w_ffn1_out"], final_g, tm=tm, final_norm=False)
    q, k, v, qi, ki, wi = _proj(x, sh2, sc2, p["norm_mix"], p["w_att"], p["seg_att"], tm=tm)
    z, xbc, dt = _proj(x, sh2, sc2, p["norm_mix"], p["w_ssd"], p["seg_ssd"], tm=tm)
    att = attend(q, k, v, qi, ki, wi)
    ssd_y, ssm_new, conv_new = ssd_fn(z, xbc, dt)
    x = _merge(x, sh2, sc2, g2, p["norm_mix"], att, ssd_y, p["w_gate"], p["w_attn_out"], p["w_ssd_out"], p["w_out"],
               tm=tm)
    y = _ffn(x, sh3, sc3, g3, p["norm_ffn2"], p["w_ffn2_in"], p["w_ffn2_out"], final_g, tm=tm, final_norm=True)
    return y, (k, v, ki, ssm_new, conv_new)


def kernel(x_prompt, x_sample, c_prompt, c_sample, cache_k, cache_v, cache_kidx, state_ssm, state_conv, page_table,
           w_ada, b_ada, norm_ffn1, w_ffn1_in, w_ffn1_out, norm_mix, w_in, rel_bias, conv_w, conv_b, a_log, dt_bias,
           d_skip, norm_ssd, w_attn_out, w_ssd_out, w_out, norm_ffn2, w_ffn2_in, w_ffn2_out, norm_final):
    depth = w_ada.shape[0]
    assert depth == 1
    batch, seq, d = x_prompt.shape
    db, dec_seq, _ = x_sample.shape
    assert dec_seq == 1
    n_pool = cache_k.shape[1]
    n_pages = page_table.shape[1]
    d_inner = norm_ssd.shape[1]
    conv_dim = conv_w.shape[2]
    n_ssd_heads = d_inner // SSD_HEAD_DIM
    att_q = N_HEADS * HEAD_DIM
    att_kv = N_KV_HEADS * HEAD_DIM
    idx_q = N_IDX_HEADS * IDX_DIM
    l = 0

    widths = (att_q, att_kv, att_kv, idx_q, IDX_DIM, N_IDX_HEADS, d_inner, conv_dim, n_ssd_heads, d, d)
    bounds = np.concatenate([[0], np.cumsum(widths)])
    assert bounds[-1] == w_in.shape[2]
    cols = [w_in[l][:, bounds[i]:bounds[i + 1]].astype(BF16) for i in range(len(widths))]
    w_q, w_k, w_v, w_qi, w_ki, w_wi, w_z, w_xbc, w_dt, w_ga, w_gs = cols
    row1 = lambda a: a.reshape(1, -1)
    p = {
        "norm_ffn1": row1(norm_ffn1[l]), "w_ffn1_in": w_ffn1_in[l].astype(BF16), "w_ffn1_out": w_ffn1_out[l].astype(BF16),
        "norm_mix": row1(norm_mix[l]),
        "w_att": jnp.concatenate([w_q, w_k, w_v, w_qi, _pad_cols(w_ki, LANES), _pad_cols(w_wi, LANES)], axis=1),
        "seg_att": [(att_q, BF16, HEAD_DIM ** -0.5 * LOG2E), (att_kv, F32, 1.0), (att_kv, F32, 1.0),
                    (idx_q, BF16, 1.0), (IDX_DIM, F32, 1.0), (LANES, F32, 1.0)],
        "w_ssd": jnp.concatenate([w_z, w_xbc, _pad_cols(w_dt, LANES)], axis=1),
        "seg_ssd": [(d_inner, F32, 1.0), (conv_dim, F32, 1.0), (LANES, F32, 1.0)],
        "w_gate": jnp.concatenate([w_ga, w_gs], axis=1),
        "w_attn_out": w_attn_out[l].astype(BF16), "w_ssd_out": w_ssd_out[l].astype(BF16), "w_out": w_out[l].astype(BF16),
        "norm_ffn2": row1(norm_ffn2[l]), "w_ffn2_in": w_ffn2_in[l].astype(BF16), "w_ffn2_out": w_ffn2_out[l].astype(BF16),
    }
    final_g = row1(norm_final)
    ssd_args = (conv_w[l], conv_b[l], a_log[l], dt_bias[l], d_skip[l], norm_ssd[l])

    ada = _ada(jnp.concatenate([c_prompt, c_sample], axis=0), w_ada[l], b_ada[l])
    ada_p = [a.reshape(batch, 1, d) for a in jnp.split(ada[:batch], 9, axis=1)]
    ada_s = [a.reshape(1, db, d) for a in jnp.split(ada[batch:], 9, axis=1)]

    tq = 128
    n_chunks = seq // SSD_CHUNK
    bias_tab = _bias_table(rel_bias, tq, seq)

    def attend_p(q, k, v, qi, ki, wi):
        return _dsa_prompt(q, qi, wi, k, v, ki, bias_tab, batch=batch, seq=seq, tq=tq)

    def ssd_p(z, xbc, dt):
        r3 = lambda a: a.reshape(batch * n_chunks, SSD_CHUNK, a.shape[-1])
        y, ssm, conv = _ssd(r3(xbc), r3(z), r3(dt), *ssd_args, n_seq=batch, n_chunks=n_chunks, rows=SSD_CHUNK)
        return y.reshape(batch * seq, d_inner), ssm, conv

    yp, (k_p, v_p, ki_p, ssm_p, conv_p) = _trunk(x_prompt.reshape(batch * seq, d), ada_p, seq, p, 512,
                                                 attend_p, ssd_p, final_g)

    pg = min(64, n_pages)
    n_top_s = min(TOPK_MAX, (n_pages * PAGE_SIZE + 1) // 4)
    bias_tab_s = _bias_table_s(rel_bias, n_pages)

    def attend_s(q, k, v, qi, ki, wi):
        wi_bc = jnp.broadcast_to(wi[:, :N_IDX_HEADS, None], (db, N_IDX_HEADS, LANES))
        scores = _dsa_s_scores(page_table, qi.reshape(db, N_IDX_HEADS, IDX_DIM), wi_bc,
                               cache_kidx[l].astype(BF16), pg=min(2 * pg, n_pages))
        mask = _dsa_s_select(scores.reshape(n_pages, db, PAGE_SIZE), qi, ki, wi, n_top=n_top_s)
        att = _dsa_s_attend(page_table, mask, bias_tab_s, q.reshape(db, N_HEADS, HEAD_DIM),
                            k.reshape(db, 1, att_kv), v.reshape(db, 1, att_kv),
                            cache_k[l].reshape(n_pool, PAGE_SIZE, att_kv),
                            cache_v[l].reshape(n_pool, PAGE_SIZE, att_kv), pg=pg)
        return att.reshape(db, att_q)

    def ssd_s(z, xbc, dt):
        r3 = lambda a: a.reshape(db, 1, a.shape[-1])
        y, ssm, conv = _ssd(r3(xbc), r3(z), r3(dt), *ssd_args, n_seq=db, n_chunks=1, rows=1,
                            conv0=state_conv[l], ssm0=state_ssm[l].reshape(db, d_inner, D_STATE))
        return y.reshape(db, d_inner), ssm, conv

    ys, (k_s, v_s, ki_s, ssm_s, conv_s) = _trunk(x_sample.reshape(db, d), ada_s, db, p, db,
                                                 attend_s, ssd_s, final_g)

    st = lambda a, *shape: a.reshape((1,) + shape)
    return (yp.reshape(batch, seq, d), ys.reshape(db, 1, d),
            st(k_p, batch, seq, N_KV_HEADS, HEAD_DIM), st(v_p, batch, seq, N_KV_HEADS, HEAD_DIM),
            st(ki_p, batch, seq, IDX_DIM),
            st(ssm_p, batch, n_ssd_heads, SSD_HEAD_DIM, D_STATE), st(conv_p, batch, conv_w.shape[1] - 1, conv_dim),
            st(k_s, db, 1, N_KV_HEADS, HEAD_DIM), st(v_s, db, 1, N_KV_HEADS, HEAD_DIM), st(ki_s, db, 1, IDX_DIM),
            st(ssm_s, db, n_ssd_heads, SSD_HEAD_DIM, D_STATE), st(conv_s, db, conv_w.shape[1] - 1, conv_dim))
```

```python
import functools
import math

import numpy as np
import jax
import jax.numpy as jnp
from jax import lax
from jax.experimental import pallas as pl
from jax.experimental.pallas import tpu as pltpu

N_HEADS = 16
HEAD_DIM = 64
N_KV_HEADS = 4
N_IDX_HEADS = 8
IDX_DIM = 64
TOPK_MAX = 256
N_BUCKETS = 32
MAX_DISTANCE = 128
SSD_HEAD_DIM = 64
SSD_GROUPS = 4
D_STATE = 128
SSD_CHUNK = 128
EPS = 1e-6
PAGE_SIZE = 128

LANES = 128
SUBLANES = 8
MXU_TILE = 256
VMEM_LIMIT = 56 * 1024 * 1024

F32 = jnp.float32
BF16 = jnp.bfloat16
I32 = jnp.int32
NEG_INF = float("-inf")
INT_MIN = -(2 ** 31)
LOG2E = math.log2(math.e)

_NT = (((1,), (1,)), ((), ()))


def _dot(a, b):
    return jnp.dot(a, b, preferred_element_type=F32)


def _dot_nt(a, b):
    return lax.dot_general(a, b, _NT, preferred_element_type=F32)


def _split3(a):
    hi = a.astype(BF16)
    r = a - hi.astype(F32)
    mid = r.astype(BF16)
    lo = (r - mid.astype(F32)).astype(BF16)
    return hi, mid, lo


def _dot_exact_rhs(a, b_bf16):
    hi, mid, lo = _split3(a)
    return _dot(hi, b_bf16) + _dot(mid, b_bf16) + _dot(lo, b_bf16)


def _dot_exact_lhs(a_bf16, b):
    hi, mid, lo = _split3(b)
    return _dot(a_bf16, hi) + _dot(a_bf16, mid) + _dot(a_bf16, lo)


def _rmsnorm(x, g):
    return (x * lax.rsqrt(jnp.mean(x * x, axis=-1, keepdims=True) + EPS)) * g


def _sigmoid(x):
    return 0.5 * jnp.tanh(0.5 * x) + 0.5


def _silu(x):
    h = 0.5 * x
    return h * jnp.tanh(h) + h


def _params(*sem):
    return pltpu.CompilerParams(dimension_semantics=sem, vmem_limit_bytes=VMEM_LIMIT)


def _t5_bucket_np(dist):
    n = np.maximum(dist, 0)
    max_exact = N_BUCKETS // 2
    nf = np.maximum(n, 1).astype(np.float32)
    val = (np.log(nf / np.float32(max_exact)) / np.float32(math.log(MAX_DISTANCE / max_exact))
           * np.float32(N_BUCKETS - max_exact)).astype(np.float32)
    frac = np.abs(val - np.round(val))
    knife = (frac < 1e-3) & (n > max_exact) & (val < N_BUCKETS - max_exact - 0.5)
    assert not knife.any()
    large = np.minimum(max_exact + val.astype(np.int32), N_BUCKETS - 1)
    return np.where(n < max_exact, n, large).astype(np.int32)


def _ada_kernel(c_ref, w_ref, b_ref, o_ref):
    h = _silu(c_ref[...]).astype(BF16)
    o_ref[...] = _dot(h, w_ref[...].astype(BF16)) + b_ref[...]


def _ada(c, w, b):
    rows, d = c.shape
    n = w.shape[1]
    tn = 1024
    return pl.pallas_call(
        _ada_kernel,
        grid=(n // tn,),
        in_specs=[pl.BlockSpec((rows, d), lambda j: (0, 0)),
                  pl.BlockSpec((d, tn), lambda j: (0, j)),
                  pl.BlockSpec((1, tn), lambda j: (0, j))],
        out_specs=pl.BlockSpec((rows, tn), lambda j: (0, j)),
        out_shape=jax.ShapeDtypeStruct((rows, n), F32),
        compiler_params=_params("arbitrary"),
    )(c, w, b.reshape(1, n))


class _Mod:
    def __init__(self, arr, rows_per_seq):
        self.arr = arr
        self.rows_per_seq = rows_per_seq

    def spec(self, tm):
        r = self.arr.shape[1]
        d = self.arr.shape[2]
        if r == 1:
            per = self.rows_per_seq // tm
            return pl.BlockSpec((1, 1, d), lambda i, *_: (i // per, 0, 0))
        assert r == tm
        return pl.BlockSpec((1, r, d), lambda i, *_: (i, 0, 0))


def _ffn_kernel(x_ref, sh_ref, sc_ref, gt_ref, g_ref, wi_ref, wo_ref, fg_ref, o_ref, *, final_norm, splits):
    x = x_ref[...]
    f = wo_ref.shape[0]
    h = (_rmsnorm(x, g_ref[...]) * (1.0 + sc_ref[0]) + sh_ref[0]).astype(BF16)
    acc = None
    for lo, hi in splits:
        gate = _dot(h, wi_ref[:, lo:hi])
        up = _dot(h, wi_ref[:, f + lo:f + hi])
        part = _dot((_silu(gate) * up).astype(BF16), wo_ref[lo:hi, :])
        acc = part if acc is None else acc + part
    out = x + 0.5 * gt_ref[0] * acc
    if final_norm:
        out = _rmsnorm(out, fg_ref[...])
    o_ref[...] = out


def _ffn(x, shift, scale, gate, norm_g, w_in, w_out, final_g, *, tm, final_norm):
    m, d = x.shape
    f = w_out.shape[0]
    assert f % MXU_TILE == 0 and m % tm == 0
    mid = (f // MXU_TILE // 2) * MXU_TILE
    splits = ((0, mid), (mid, f))
    row = lambda i: (i, 0)
    const = lambda i: (0, 0)
    resident = lambda shape: pl.BlockSpec(shape, const, pipeline_mode=pl.Buffered(1))
    return pl.pallas_call(
        functools.partial(_ffn_kernel, final_norm=final_norm, splits=splits),
        grid=(m // tm,),
        in_specs=[pl.BlockSpec((tm, d), row),
                  shift.spec(tm), scale.spec(tm), gate.spec(tm),
                  pl.BlockSpec((1, d), const),
                  resident(w_in.shape), resident(w_out.shape),
                  pl.BlockSpec((1, d), const)],
        out_specs=pl.BlockSpec((tm, d), row),
        out_shape=jax.ShapeDtypeStruct((m, d), F32),
        compiler_params=_params("parallel"),
    )(x, shift.arr, scale.arr, gate.arr, norm_g, w_in, w_out, final_g)


def _proj_kernel(x_ref, sh_ref, sc_ref, g_ref, w_ref, *o_refs, offsets, scales):
    h = (_rmsnorm(x_ref[...], g_ref[...]) * (1.0 + sc_ref[0]) + sh_ref[0]).astype(BF16)
    for o_ref, off, scale in zip(o_refs, offsets, scales):
        width = o_ref.shape[1]
        out = _dot(h, w_ref[:, off:off + width])
        if scale != 1.0:
            out = out * scale
        o_ref[...] = out.astype(o_ref.dtype)


def _proj(x, shift, scale, norm_g, w, segments, *, tm):
    m, d = x.shape
    offsets, off = [], 0
    for width, _, _ in segments:
        offsets.append(off)
        off += -(-width // LANES) * LANES
    assert off == w.shape[1]
    row = lambda i: (i, 0)
    const = lambda i: (0, 0)
    return pl.pallas_call(
        functools.partial(_proj_kernel, offsets=tuple(offsets), scales=tuple(s for _, _, s in segments)),
        grid=(m // tm,),
        in_specs=[pl.BlockSpec((tm, d), row), shift.spec(tm), scale.spec(tm),
                  pl.BlockSpec((1, d), const), pl.BlockSpec(w.shape, const, pipeline_mode=pl.Buffered(1))],
        out_specs=[pl.BlockSpec((tm, width), row) for width, _, _ in segments],
        out_shape=[jax.ShapeDtypeStruct((m, width), dt) for width, dt, _ in segments],
        compiler_params=_params("parallel"),
    )(x, shift.arr, scale.arr, norm_g, w)


def _sort_key(score):
    score = jnp.where(score == 0.0, 0.0, score)
    bits = pltpu.bitcast(score, I32)
    return jnp.where(bits >= 0, bits, bits ^ jnp.int32(0x7FFFFFFF))


def _radix_threshold(count_ge, n_top, shape):
    def bit_body(t, carry):
        thr_u, cnt_thr = carry
        cand_u = thr_u | jnp.left_shift(jnp.int32(1), 31 - t)
        cnt = count_ge(cand_u ^ jnp.int32(INT_MIN))
        take = cnt >= n_top
        return jnp.where(take, cand_u, thr_u), jnp.where(take, cnt, cnt_thr)

    thr_u, cnt_thr = lax.fori_loop(0, 32, bit_body, (jnp.zeros(shape, I32), jnp.zeros(shape, I32)))
    return thr_u ^ jnp.int32(INT_MIN), cnt_thr


def _bias_table_kernel(rb_ref, bkt_ref, o_ref):
    far = N_BUCKETS - 1
    for slot in range(bkt_ref.shape[0]):
        bkt = bkt_ref[slot]
        for h in range(N_HEADS):
            acc = jnp.zeros(bkt.shape, F32)
            for b in range(N_BUCKETS - 1):
                acc = jnp.where(bkt == b, (rb_ref[b, h] - rb_ref[far, h]) * LOG2E, acc)
            o_ref[slot, h] = acc


def _bias_table(rel_bias, tq, seq):
    s = np.arange(tq)[:, None]
    t = np.arange(tq)[None, :]
    bkt = np.stack([_t5_bucket_np(t + tq - s), _t5_bucket_np(t - s)])
    assert (_t5_bucket_np(np.arange(tq + 1, seq + 1)) == N_BUCKETS - 1).all()
    return pl.pallas_call(
        _bias_table_kernel,
        in_specs=[pl.BlockSpec(memory_space=pltpu.SMEM), pl.BlockSpec(memory_space=pltpu.VMEM)],
        out_specs=pl.BlockSpec(memory_space=pltpu.VMEM),
        out_shape=jax.ShapeDtypeStruct((2, N_HEADS, tq, tq), F32),
    )(rel_bias, jnp.asarray(bkt))


_V_ROWS = HEAD_DIM + 16


def _dsa_prompt_kernel(q_ref, qi_ref, wi_ref, k_ref, v_ref, ki_ref, bias_ref, o_ref,
                       kb_scr, vt_scr, kib_scr, key_scr, mb_scr, m_scr, al_scr, lg_scr, acc_scr, *, n_top, tq):
    i = pl.program_id(1)
    nq = N_HEADS // N_KV_HEADS
    nblk = key_scr.shape[0]
    s_io = lax.broadcasted_iota(I32, (tq, tq), 0)
    t_io = lax.broadcasted_iota(I32, (tq, tq), 1)
    causal = s_io <= t_io

    @pl.when(i == 0)
    def _():
        kb_scr[...] = k_ref[...].astype(BF16)
        kib_scr[...] = ki_ref[...].astype(BF16)
        for c in range(nblk):
            for pair in range(N_KV_HEADS // 2):
                vt = v_ref[c * tq:(c + 1) * tq, pair * 2 * HEAD_DIM:(pair + 1) * 2 * HEAD_DIM].T.astype(BF16)
                vt_scr[c, 2 * pair, 0:HEAD_DIM, :] = vt[0:HEAD_DIM]
                vt_scr[c, 2 * pair + 1, 0:HEAD_DIM, :] = vt[HEAD_DIM:2 * HEAD_DIM]
            for n in range(N_KV_HEADS):
                vt_scr[c, n, HEAD_DIM:_V_ROWS, :] = jnp.ones((_V_ROWS - HEAD_DIM, tq), BF16)

    def chunk(j):
        return pl.ds(pl.multiple_of(j * tq, tq), tq)

    def fold(a):
        return jnp.sum(a.reshape(tq // SUBLANES, SUBLANES, tq), axis=0)

    qi = qi_ref[...]
    qi_st = jnp.concatenate([qi[:, h * IDX_DIM:(h + 1) * IDX_DIM] for h in range(N_IDX_HEADS)], axis=0)
    wi_t = wi_ref[...].T
    wi_row = jnp.concatenate([wi_t[h:h + 1, :] for h in range(N_IDX_HEADS)], axis=1)
    idx_scale = (N_IDX_HEADS * IDX_DIM) ** -0.5

    def visible(j):
        return (j < i) | ((j == i) & causal)

    n_pairs = (i + 2) // 2

    def score_body(jp, carry):
        for j in (2 * jp, 2 * jp + 1):
            d = jnp.maximum(_dot_nt(kib_scr[chunk(j), :], qi_st), 0.0) * wi_row
            s = d[:, 0:tq]
            for h in range(1, N_IDX_HEADS):
                s = s + d[:, h * tq:(h + 1) * tq]
            key_scr[j] = _sort_key(jnp.where(visible(j), s * idx_scale, NEG_INF))
        return carry

    lax.fori_loop(0, n_pairs, score_body, 0)

    n_quads = (i + 4) // 4

    def park_body(j, carry):
        key_scr[j] = jnp.full((tq, tq), INT_MIN, I32)
        return carry

    lax.fori_loop(2 * n_pairs, 4 * n_quads, park_body, 0)

    def count(pred):
        def body(jq, c):
            for u in range(4):
                c = c + fold(pred(key_scr[4 * jq + u]).astype(I32))
            return c
        c = lax.fori_loop(0, n_quads, body, jnp.zeros((SUBLANES, tq), I32))
        return jnp.sum(c, axis=0, keepdims=True)

    thr, cnt_thr = _radix_threshold(lambda cand: count(lambda key: key >= cand), n_top, (1, tq))

    def sel_plain():
        def body(j, carry):
            mb_scr[j] = jnp.where((key_scr[j] >= thr) & visible(j), 0.0, NEG_INF)
            return carry
        lax.fori_loop(0, i + 1, body, 0)

    def sel_ties():
        need = (n_top - count(lambda key: key > thr)).astype(F32)
        tril = (s_io >= t_io).astype(BF16)

        def body(j, run_eq):
            key = key_scr[j]
            eq = key == thr
            pre = _dot(tril, eq.astype(BF16))
            sel = ((key > thr) | (eq & (run_eq + pre <= need))) & visible(j)
            mb_scr[j] = jnp.where(sel, 0.0, NEG_INF)
            return run_eq + pre[tq - 1:tq, :]
        lax.fori_loop(0, i + 1, body, jnp.zeros((1, tq), F32))

    lax.cond(jnp.max(cnt_thr) > n_top, sel_ties, sel_plain)

    q = q_ref[...]
    q_st = [jnp.concatenate([q[:, (n * nq + g) * HEAD_DIM:(n * nq + g + 1) * HEAD_DIM] for g in range(nq)], axis=0)
            for n in range(N_KV_HEADS)]
    m_scr[...] = jnp.full(m_scr.shape, -1e30, F32)
    acc_scr[...] = jnp.zeros_like(acc_scr)

    all_heads = tuple(range(N_KV_HEADS))
    head_halves = (all_heads[:N_KV_HEADS // 2], all_heads[N_KV_HEADS // 2:])

    def stage_a(j, slot, heads=all_heads):
        mb4 = jnp.concatenate([mb_scr[j]] * nq, axis=1)
        lgs = {n: _dot_nt(kb_scr[chunk(j), n * HEAD_DIM:(n + 1) * HEAD_DIM], q_st[n]) + mb4 for n in heads}
        if slot is not None:
            lgs = {n: lgs[n] + jnp.concatenate([bias_ref[slot, n * nq + g] for g in range(nq)], axis=1)
                   for n in heads}
        for n in heads:
            m_old = m_scr[n]
            m_new = jnp.maximum(m_old, jnp.max(lgs[n], axis=0, keepdims=True))
            al_scr[n] = jnp.exp2(m_old - m_new)
            m_scr[n] = m_new
            lg_scr[j % 2, n] = lgs[n]

    def stage_b(j, heads=all_heads):
        ps = {n: jnp.exp2(lg_scr[j % 2, n] - m_scr[n]).astype(BF16) for n in heads}
        pvs = {n: _dot(vt_scr[j, n], ps[n]) for n in heads}
        for n in heads:
            acc_scr[n] = al_scr[n] * acc_scr[n] + pvs[n]

    def stage_ba(jb, ja, slot):
        for half in head_halves:
            stage_b(jb, half)
            stage_a(ja, slot, half)

    @pl.when(i >= 2)
    def _():
        stage_a(0, None)

        def far_body(j, carry):
            stage_ba(j - 1, j, None)
            return carry

        lax.fori_loop(1, i - 1, far_body, 0)
        stage_ba(i - 2, i - 1, 0)

    @pl.when(i == 1)
    def _():
        stage_a(0, 0)

    @pl.when(i >= 1)
    def _():
        stage_ba(i - 1, i, 1)

    @pl.when(i == 0)
    def _():
        stage_a(0, 1)

    stage_b(i)

    for n in range(N_KV_HEADS):
        acc = acc_scr[n]
        out_t = acc[0:HEAD_DIM] / acc[HEAD_DIM:HEAD_DIM + 1]
        for pair in range(nq // 2):
            g0 = 2 * pair
            two = jnp.concatenate([out_t[:, g0 * tq:(g0 + 1) * tq], out_t[:, (g0 + 1) * tq:(g0 + 2) * tq]], axis=0)
            h0 = n * nq + g0
            o_ref[:, h0 * HEAD_DIM:(h0 + 2) * HEAD_DIM] = two.T.astype(o_ref.dtype)


def _dsa_prompt(q, qi, wi, k, v, ki, bias_tab, *, batch, seq, tq):
    nblk = seq // tq
    n_top = min(TOPK_MAX, seq // 4)
    nq = N_HEADS // N_KV_HEADS
    assert tq == LANES and nq % 2 == 0 and N_KV_HEADS % 2 == 0 and nblk % 4 == 0
    blk = lambda b, i: (b * nblk + i, 0)
    whole = lambda b, i: (b, 0)
    kvw = N_KV_HEADS * HEAD_DIM
    return pl.pallas_call(
        functools.partial(_dsa_prompt_kernel, n_top=n_top, tq=tq),
        grid=(batch, nblk),
        in_specs=[pl.BlockSpec((tq, N_HEADS * HEAD_DIM), blk),
                  pl.BlockSpec((tq, N_IDX_HEADS * IDX_DIM), blk),
                  pl.BlockSpec((tq, LANES), blk),
                  pl.BlockSpec((seq, kvw), whole),
                  pl.BlockSpec((seq, kvw), whole),
                  pl.BlockSpec((seq, IDX_DIM), whole),
                  pl.BlockSpec(bias_tab.shape, lambda b, i: (0, 0, 0, 0))],
        out_specs=pl.BlockSpec((tq, N_HEADS * HEAD_DIM), blk),
        out_shape=jax.ShapeDtypeStruct((batch * seq, N_HEADS * HEAD_DIM), BF16),
        scratch_shapes=[pltpu.VMEM((seq, kvw), BF16),
                        pltpu.VMEM((nblk, N_KV_HEADS, _V_ROWS, tq), BF16),
                        pltpu.VMEM((seq, IDX_DIM), BF16),
                        pltpu.VMEM((nblk, tq, tq), I32), pltpu.VMEM((nblk, tq, tq), F32),
                        pltpu.VMEM((N_KV_HEADS, 1, nq * tq), F32), pltpu.VMEM((N_KV_HEADS, 1, nq * tq), F32),
                        pltpu.VMEM((2, N_KV_HEADS, tq, nq * tq), F32),
                        pltpu.VMEM((N_KV_HEADS, _V_ROWS, nq * tq), F32)],
        compiler_params=_params("arbitrary", "arbitrary"),
    )(q, qi, wi, k, v, ki, bias_tab)


def _dsa_s_score_kernel(pt_ref, qi_ref, wi_ref, *refs, pg):
    page_refs, o_ref = refs[:pg], refs[pg]
    qi = qi_ref[0].astype(BF16)
    wi = wi_ref[0]
    idx_scale = (N_IDX_HEADS * IDX_DIM) ** -0.5
    for r in range(pg):
        d = _dot_nt(qi, page_refs[r][0])
        o_ref[r, 0] = jnp.sum(jnp.maximum(d, 0.0) * wi, axis=0, keepdims=True) * idx_scale


def _dsa_s_scores(page_table, qi, wi_bc, cache_kidx, *, pg):
    db, n_pages = page_table.shape
    page_spec = lambda r: pl.BlockSpec(
        (1, PAGE_SIZE, IDX_DIM), lambda b, s, pt: (pt[b * n_pages + s * pg + r], 0, 0))
    grid_spec = pltpu.PrefetchScalarGridSpec(
        num_scalar_prefetch=1,
        grid=(db, n_pages // pg),
        in_specs=[pl.BlockSpec((1, N_IDX_HEADS, IDX_DIM), lambda b, s, pt: (b, 0, 0)),
                  pl.BlockSpec((1, N_IDX_HEADS, LANES), lambda b, s, pt: (b, 0, 0))]
                 + [page_spec(r) for r in range(pg)],
        out_specs=pl.BlockSpec((pg, 1, 1, PAGE_SIZE), lambda b, s, pt: (s, b, 0, 0)),
    )
    return pl.pallas_call(
        functools.partial(_dsa_s_score_kernel, pg=pg),
        grid_spec=grid_spec,
        out_shape=jax.ShapeDtypeStruct((n_pages, db, 1, PAGE_SIZE), F32),
        compiler_params=_params("arbitrary", "arbitrary"),
    )(page_table.reshape(-1), qi, wi_bc, *([cache_kidx] * pg))


def _dsa_s_select_kernel(sc_ref, qi_ref, kis_ref, wi_ref, hsum_ref, o_ref, key_scr, *, n_top):
    n_pages, db, _ = sc_ref.shape
    idx_scale = (N_IDX_HEADS * IDX_DIM) ** -0.5

    prod = (qi_ref[...].astype(BF16).astype(F32)
            * jnp.concatenate([kis_ref[...].astype(BF16).astype(F32)] * N_IDX_HEADS, axis=1))
    d_self = _dot_exact_rhs(prod, hsum_ref[...])
    s_self = jnp.sum((jnp.maximum(d_self, 0.0) * wi_ref[...]).T, axis=0, keepdims=True) * idx_scale
    key_self = _sort_key(s_self)

    def to_keys(r, carry):
        key_scr[r] = _sort_key(sc_ref[r].T)
        return carry
    lax.fori_loop(0, n_pages, to_keys, 0)

    def fold(a):
        return jnp.sum(a.reshape(PAGE_SIZE // SUBLANES, SUBLANES, db), axis=0)

    def count(pred):
        def body(rp, c):
            return c + fold(pred(key_scr[2 * rp]).astype(I32)) + fold(pred(key_scr[2 * rp + 1]).astype(I32))
        c = lax.fori_loop(0, n_pages // 2, body, jnp.zeros((SUBLANES, db), I32))
        return jnp.sum(c, axis=0, keepdims=True) + pred(key_self).astype(I32)

    thr, cnt_thr = _radix_threshold(lambda cand: count(lambda key: key >= cand), n_top, (1, db))

    def emit(r, sel):
        o_ref[r] = jnp.where(sel, 0.0, NEG_INF).T

    def sel_plain():
        def body(r, carry):
            emit(r, key_scr[r] >= thr)
            return carry
        lax.fori_loop(0, n_pages, body, 0)
        emit(n_pages, jnp.broadcast_to(key_self >= thr, (PAGE_SIZE, db)))

    def sel_ties():
        need = (n_top - count(lambda key: key > thr)).astype(F32)
        r_io = lax.broadcasted_iota(I32, (PAGE_SIZE, PAGE_SIZE), 0)
        c_io = lax.broadcasted_iota(I32, (PAGE_SIZE, PAGE_SIZE), 1)
        tril = (r_io >= c_io).astype(BF16)

        def body(r, run_eq):
            key = key_scr[r]
            eq = key == thr
            pre = _dot(tril, eq.astype(BF16))
            emit(r, (key > thr) | (eq & (run_eq + pre <= need)))
            return run_eq + pre[PAGE_SIZE - 1:PAGE_SIZE, :]
        run_eq = lax.fori_loop(0, n_pages, body, jnp.zeros((1, db), F32))
        sel_self = (key_self > thr) | ((key_self == thr) & (run_eq + 1.0 <= need))
        emit(n_pages, jnp.broadcast_to(sel_self, (PAGE_SIZE, db)))

    lax.cond(jnp.max(cnt_thr) > n_top, sel_ties, sel_plain)


def _dsa_s_select(scores, qi, ki_s, wi, *, n_top):
    n_pages, db, _ = scores.shape
    assert db == LANES and n_pages % 2 == 0
    hsum = np.zeros((N_IDX_HEADS * IDX_DIM, LANES), np.float32)
    hsum[np.arange(N_IDX_HEADS * IDX_DIM), np.arange(N_IDX_HEADS * IDX_DIM) // IDX_DIM] = 1.0
    return pl.pallas_call(
        functools.partial(_dsa_s_select_kernel, n_top=n_top),
        out_shape=jax.ShapeDtypeStruct((n_pages + 1, db, PAGE_SIZE), F32),
        scratch_shapes=[pltpu.VMEM((n_pages, PAGE_SIZE, db), I32)],
        compiler_params=pltpu.CompilerParams(vmem_limit_bytes=VMEM_LIMIT),
    )(scores, qi, ki_s, wi, jnp.asarray(hsum, BF16))


def _bias_table_s_kernel(rbt_ref, bkt_ref, o_ref):
    rbt = rbt_ref[...] * LOG2E
    for r in range(bkt_ref.shape[0]):
        bkt = bkt_ref[r:r + 1, :]
        out = jnp.zeros((N_HEADS, bkt.shape[1]), F32)
        for b in range(N_BUCKETS):
            out = jnp.where(bkt == b, rbt[:, b:b + 1], out)
        o_ref[r] = out


def _bias_table_s(rel_bias, n_pages):
    past = n_pages * PAGE_SIZE
    pos = np.arange((n_pages + 1) * PAGE_SIZE).reshape(n_pages + 1, PAGE_SIZE)
    pos[n_pages] = past
    return pl.pallas_call(
        _bias_table_s_kernel,
        out_shape=jax.ShapeDtypeStruct((n_pages + 1, N_HEADS, PAGE_SIZE), F32),
    )(rel_bias.T, jnp.asarray(_t5_bucket_np(past - pos)))


def _dsa_s_attend_kernel(pt_ref, mb_ref, bias_ref, q_ref, ks_ref, vs_ref, *refs, pg, n_pages):
    k_refs, v_refs = refs[:pg], refs[pg:2 * pg]
    o_ref, m_scr, l_scr, acc_scr = refs[2 * pg:]
    s = pl.program_id(1)
    nq = N_HEADS // N_KV_HEADS
    kvw = N_KV_HEADS * HEAD_DIM

    @pl.when(s == 0)
    def _():
        m_scr[...] = jnp.full(m_scr.shape, -1e30, F32)
        l_scr[...] = jnp.zeros_like(l_scr)
        acc_scr[...] = jnp.zeros_like(acc_scr)

    q = q_ref[0]
    h_io = lax.broadcasted_iota(I32, (N_HEADS, kvw), 0)
    c_io = lax.broadcasted_iota(I32, (N_HEADS, kvw), 1)
    band = (c_io // HEAD_DIM) == (h_io // nq)
    q_bd = jnp.where(band, jnp.concatenate([q] * N_KV_HEADS, axis=1), 0.0)
    logits = []
    for r in range(pg):
        page = s * pg + r
        logits.append(_dot_nt(q_bd, k_refs[r][0].astype(BF16)) + bias_ref[page] + mb_ref[page, 0])
    m_old = m_scr[...]
    m_new = m_old
    for lg in logits:
        m_new = jnp.maximum(m_new, jnp.max(lg, axis=1, keepdims=True))
    alpha = jnp.exp2(m_old - m_new)
    l_new = alpha * l_scr[...]
    acc = alpha * acc_scr[...]
    for r, lg in enumerate(logits):
        p = jnp.exp2(lg - m_new)
        l_new = l_new + jnp.sum(p, axis=1, keepdims=True)
        acc = acc + _dot(p.astype(BF16), v_refs[r][0].astype(BF16))
    m_scr[...] = m_new
    l_scr[...] = l_new
    acc_scr[...] = acc

    @pl.when(s == pl.num_programs(1) - 1)
    def _():
        ks = ks_ref[0].astype(BF16).astype(F32)
        lg = jnp.sum(q_bd.astype(F32) * ks, axis=1, keepdims=True)
        lg = lg + bias_ref[n_pages][:, 0:1] + mb_ref[n_pages, 0][:, 0:1]
        m_fin = jnp.maximum(m_new, lg)
        a = jnp.exp2(m_new - m_fin)
        p = jnp.exp2(lg - m_fin)
        l_fin = a * l_new + p
        out = (a * acc + p.astype(BF16).astype(F32) * vs_ref[0].astype(BF16).astype(F32)) / l_fin
        out = jnp.where(band, out, 0.0)
        res = out[:, 0:HEAD_DIM]
        for n in range(1, N_KV_HEADS):
            res = res + out[:, n * HEAD_DIM:(n + 1) * HEAD_DIM]
        o_ref[0] = res.astype(o_ref.dtype)


def _dsa_s_attend(page_table, mask, bias_tab, q, k_s, v_s, cache_k, cache_v, *, pg):
    db, n_pages = page_table.shape
    kvw = N_KV_HEADS * HEAD_DIM
    width = PAGE_SIZE
    per_seq = lambda *shape: pl.BlockSpec((1,) + shape, lambda b, s, pt: (b,) + (0,) * len(shape))
    page_spec = lambda r: pl.BlockSpec(
        (1, PAGE_SIZE, kvw), lambda b, s, pt: (pt[b * n_pages + s * pg + r], 0, 0))
    grid_spec = pltpu.PrefetchScalarGridSpec(
        num_scalar_prefetch=1,
        grid=(db, n_pages // pg),
        in_specs=[pl.BlockSpec((n_pages + 1, 1, 1, width), lambda b, s, pt: (0, b, 0, 0)),
                  pl.BlockSpec(bias_tab.shape, lambda b, s, pt: (0, 0, 0)),
                  per_seq(N_HEADS, HEAD_DIM), per_seq(1, kvw), per_seq(1, kvw)]
                 + [page_spec(r) for r in range(pg)] * 2,
        out_specs=per_seq(N_HEADS, HEAD_DIM),
        scratch_shapes=[pltpu.VMEM((N_HEADS, 1), F32), pltpu.VMEM((N_HEADS, 1), F32),
                        pltpu.VMEM((N_HEADS, kvw), F32)],
    )
    return pl.pallas_call(
        functools.partial(_dsa_s_attend_kernel, pg=pg, n_pages=n_pages),
        grid_spec=grid_spec,
        out_shape=jax.ShapeDtypeStruct((db, N_HEADS, HEAD_DIM), BF16),
        compiler_params=_params("arbitrary", "arbitrary"),
    )(page_table.reshape(-1), mask.reshape(n_pages + 1, db, 1, width), bias_tab, q, k_s, v_s,
      *([cache_k] * pg), *([cache_v] * pg))


def _ssd_kernel(xbc_ref, z_ref, dt_ref, cw_ref, cb_ref, alog_ref, dtb_ref, dsk_ref, ng_ref, e_ref, *refs,
                rows, has_init, d_inner):
    if has_init:
        conv0_ref, ssm0_ref = refs[:2]
        refs = refs[2:]
    y_ref, ssm_ref, conv_ref, ext_scr, ht_scr, y_scr = refs
    c = pl.program_id(1)
    cl = SSD_CHUNK
    gw = d_inner // SSD_GROUPS
    hpg = gw // SSD_HEAD_DIM
    n_heads = d_inner // SSD_HEAD_DIM
    conv_dim = d_inner + 2 * SSD_GROUPS * D_STATE
    conv_w = cw_ref.shape[0]
    top = SUBLANES
    row_io = lax.broadcasted_iota(I32, (cl, 1), 0)

    def padded(ref):
        a = ref[0]
        if rows == cl:
            return a
        return jnp.where(row_io < rows, jnp.broadcast_to(a, (cl, a.shape[1])), 0.0)

    @pl.when(c == 0)
    def _():
        ext_scr[0:top, :] = jnp.zeros((top, conv_dim), F32)
        if has_init:
            ext_scr[top - conv_w + 1:top, :] = conv0_ref[0]
            for g in range(SSD_GROUPS):
                ht_scr[g] = ssm0_ref[0, g * gw:(g + 1) * gw, :].T
        else:
            ht_scr[...] = jnp.zeros_like(ht_scr)

    ext_scr[top:top + cl, :] = padded(xbc_ref)
    conv_ref[0] = ext_scr[top + rows - conv_w + 1:top + rows, :]

    cblk = 512
    for cb in range(conv_dim // cblk):
        sl = slice(cb * cblk, (cb + 1) * cblk)
        acc = jnp.broadcast_to(cb_ref[:, sl], (cl, cblk))
        for w in range(conv_w):
            acc = acc + ext_scr[top - conv_w + 1 + w:top - conv_w + 1 + w + cl, sl] * cw_ref[w:w + 1, sl]
        y_scr[:, sl] = _silu(acc)
    ext_scr[top - conv_w + 1:top, :] = ext_scr[top + cl - conv_w + 1:top + cl, :]

    dt = jax.nn.softplus(padded(dt_ref) + dtb_ref[...])
    if rows < cl:
        dt = jnp.where(row_io < rows, dt, 0.0)
    a_neg2 = -jnp.exp(alog_ref[...]) * LOG2E
    r_io = lax.broadcasted_iota(I32, (cl, cl), 0)
    c_io = lax.broadcasted_iota(I32, (cl, cl), 1)
    tril = r_io >= c_io
    acs = _dot_exact_lhs(tril.astype(BF16), dt * a_neg2)
    acs_t = acs.T
    acs_last = acs[cl - 1:cl, :]
    stacked = jnp.concatenate([dt, jnp.exp2(acs), jnp.exp2(acs_last - acs)], axis=0)
    hi = stacked.astype(BF16)
    lo = (stacked - hi.astype(F32)).astype(BF16)
    expanded = _dot(hi, e_ref[...]) + _dot(lo, e_ref[...])
    dt_x, ea_x, te_x = expanded[0:cl], expanded[cl:2 * cl], expanded[2 * cl:3 * cl]

    z = padded(z_ref)
    for g in range(SSD_GROUPS):
        gs = slice(g * gw, (g + 1) * gw)
        x_g = y_scr[:, gs]
        b_g = y_scr[:, d_inner + g * D_STATE:d_inner + (g + 1) * D_STATE]
        c_g = y_scr[:, d_inner + (SSD_GROUPS + g) * D_STATE:d_inner + (SSD_GROUPS + g + 1) * D_STATE]
        c16 = c_g.astype(BF16)
        cbm = _dot_nt(c16, b_g.astype(BF16))
        xdt = x_g * dt_x[:, gs]
        xdt16 = xdt.astype(BF16)
        ht = ht_scr[g]
        y_g = _dot(c16, ht.astype(BF16)) * ea_x[:, gs] + dsk_ref[:, gs] * x_g
        ht_scr[g] = ht * ea_x[cl - 1:cl, gs] + _dot(b_g.T.astype(BF16), (xdt * te_x[:, gs]).astype(BF16))
        if rows == 1:
            y_diag = cbm[0:1, 0:1] * xdt16.astype(F32)
        else:
            diag = []
            for r in range(hpg):
                h = g * hpg + r
                seg = acs[:, h:h + 1] - acs_t[h:h + 1, :]
                m = (cbm * jnp.exp2(jnp.where(tril, seg, NEG_INF))).astype(BF16)
                diag.append(_dot(m, xdt16[:, r * SSD_HEAD_DIM:(r + 1) * SSD_HEAD_DIM]))
            y_diag = jnp.concatenate(diag, axis=1)
        y_g = (y_g + y_diag) * _silu(z[:, gs])
        y_g = y_g * lax.rsqrt(jnp.mean(y_g * y_g, axis=1, keepdims=True) + EPS) * ng_ref[:, gs]
        y_ref[0, :, gs] = y_g[0:rows].astype(y_ref.dtype)

    @pl.when(c == pl.num_programs(1) - 1)
    def _():
        for g in range(SSD_GROUPS):
            ssm_ref[0, g * gw:(g + 1) * gw, :] = ht_scr[g].T


def _ssd(xbc, z, dt, conv_w, conv_b, a_log, dt_bias, d_skip, norm_g, *, n_seq, n_chunks, rows,
         conv0=None, ssm0=None):
    conv_dim = xbc.shape[-1]
    d_inner = z.shape[-1]
    n_heads = d_inner // SSD_HEAD_DIM
    cw = conv_w.shape[0]
    has_init = conv0 is not None
    pad = lambda a: jnp.pad(a, (0, LANES - a.shape[0])).reshape(1, LANES)
    expand = np.zeros((LANES, d_inner), np.float32)
    expand[np.arange(d_inner) // SSD_HEAD_DIM, np.arange(d_inner)] = 1.0
    step = lambda b, c: (b * n_chunks + c, 0, 0)
    seq = lambda b, c: (b, 0, 0)
    const = lambda b, c: (0, 0)
    in_specs = [pl.BlockSpec((1, rows, conv_dim), step), pl.BlockSpec((1, rows, d_inner), step),
                pl.BlockSpec((1, rows, LANES), step),
                pl.BlockSpec((cw, conv_dim), const), pl.BlockSpec((1, conv_dim), const),
                pl.BlockSpec((1, LANES), const), pl.BlockSpec((1, LANES), const),
                pl.BlockSpec((1, d_inner), const), pl.BlockSpec((1, d_inner), const),
                pl.BlockSpec((LANES, d_inner), const)]
    args = [xbc, z, dt, conv_w, conv_b.reshape(1, conv_dim), pad(a_log), pad(dt_bias),
            jnp.repeat(d_skip, SSD_HEAD_DIM).reshape(1, d_inner), norm_g.reshape(1, d_inner),
            jnp.asarray(expand, BF16)]
    if has_init:
        in_specs += [pl.BlockSpec((1, cw - 1, conv_dim), seq), pl.BlockSpec((1, d_inner, D_STATE), seq)]
        args += [conv0, ssm0]
    return pl.pallas_call(
        functools.partial(_ssd_kernel, rows=rows, has_init=has_init, d_inner=d_inner),
        grid=(n_seq, n_chunks),
        in_specs=in_specs,
        out_specs=[pl.BlockSpec((1, rows, d_inner), step), pl.BlockSpec((1, d_inner, D_STATE), seq),
                   pl.BlockSpec((1, cw - 1, conv_dim), seq)],
        out_shape=[jax.ShapeDtypeStruct((n_seq * n_chunks, rows, d_inner), BF16),
                   jax.ShapeDtypeStruct((n_seq, d_inner, D_STATE), F32),
                   jax.ShapeDtypeStruct((n_seq, cw - 1, conv_dim), F32)],
        scratch_shapes=[pltpu.VMEM((SUBLANES + SSD_CHUNK, conv_dim), F32),
                        pltpu.VMEM((SSD_GROUPS, D_STATE, d_inner // SSD_GROUPS), F32),
                        pltpu.VMEM((SSD_CHUNK, conv_dim), F32)],
        compiler_params=_params("arbitrary", "arbitrary"),
    )(*args)


def _merge_kernel(x_ref, sh_ref, sc_ref, gt_ref, g_ref, att_ref, ssd_ref, wg_ref, wa_ref, ws_ref, wo_ref, o_ref):
    x = x_ref[...]
    d = x.shape[1]
    h = (_rmsnorm(x, g_ref[...]) * (1.0 + sc_ref[0]) + sh_ref[0]).astype(BF16)
    gates = _dot(h, wg_ref[...])
    merged = (_sigmoid(gates[:, 0:d]) * _dot(att_ref[...], wa_ref[...])
              + _sigmoid(gates[:, d:2 * d]) * _dot(ssd_ref[...], ws_ref[...]))
    o_ref[...] = x + gt_ref[0] * _dot(merged.astype(BF16), wo_ref[...])


def _merge(x, shift, scale, gate, norm_g, att, ssd_y, w_g, w_a, w_s, w_o, *, tm):
    m, d = x.shape
    row = lambda i: (i, 0)
    const = lambda i: (0, 0)
    resident = lambda shape: pl.BlockSpec(shape, const, pipeline_mode=pl.Buffered(1))
    return pl.pallas_call(
        _merge_kernel,
        grid=(m // tm,),
        in_specs=[pl.BlockSpec((tm, d), row), shift.spec(tm), scale.spec(tm), gate.spec(tm),
                  pl.BlockSpec((1, d), const),
                  pl.BlockSpec((tm, att.shape[1]), row), pl.BlockSpec((tm, ssd_y.shape[1]), row),
                  resident(w_g.shape), resident(w_a.shape), resident(w_s.shape), resident(w_o.shape)],
        out_specs=pl.BlockSpec((tm, d), row),
        out_shape=jax.ShapeDtypeStruct((m, d), F32),
        compiler_params=_params("parallel"),
    )(x, shift.arr, scale.arr, gate.arr, norm_g, att, ssd_y, w_g, w_a, w_s, w_o)


def _pad_cols(w, width):
    return jnp.pad(w, ((0, 0), (0, width - w.shape[1])))


def _trunk(x, mods, rows_per_seq, p, tm, attend, ssd_fn, final_g):
    sh1, sc1, g1, sh2, sc2, g2, sh3, sc3, g3 = [_Mod(a, rows_per_seq) for a in mods]
    x = _ffn(x, sh1, sc1, g1, p["norm_ffn1"], p["w_ffn1_in"], p["w_ffn1_out"], final_g, tm=tm, final_norm=False)
    q, k, v, qi, ki, wi = _proj(x, sh2, sc2, p["norm_mix"], p["w_att"], p["seg_att"], tm=tm)
    z, xbc, dt = _proj(x, sh2, sc2, p["norm_mix"], p["w_ssd"], p["seg_ssd"], tm=tm)
    att = attend(q, k, v, qi, ki, wi)
    ssd_y, ssm_new, conv_new = ssd_fn(z, xbc, dt)
    x = _merge(x, sh2, sc2, g2, p["norm_mix"], att, ssd_y, p["w_gate"], p["w_attn_out"], p["w_ssd_out"], p["w_out"],
               tm=tm)
    y = _ffn(x, sh3, sc3, g3, p["norm_ffn2"], p["w_ffn2_in"], p["w_ffn2_out"], final_g, tm=tm, final_norm=True)
    return y, (k, v, ki, ssm_new, conv_new)


def kernel(x_prompt, x_sample, c_prompt, c_sample, cache_k, cache_v, cache_kidx, state_ssm, state_conv, page_table,
           w_ada, b_ada, norm_ffn1, w_ffn1_in, w_ffn1_out, norm_mix, w_in, rel_bias, conv_w, conv_b, a_log, dt_bias,
           d_skip, norm_ssd, w_attn_out, w_ssd_out, w_out, norm_ffn2, w_ffn2_in, w_ffn2_out, norm_final):
    depth = w_ada.shape[0]
    assert depth == 1
    batch, seq, d = x_prompt.shape
    db, dec_seq, _ = x_sample.shape
    assert dec_seq == 1
    n_pool = cache_k.shape[1]
    n_pages = page_table.shape[1]
    d_inner = norm_ssd.shape[1]
    conv_dim = conv_w.shape[2]
    n_ssd_heads = d_inner // SSD_HEAD_DIM
    att_q = N_HEADS * HEAD_DIM
    att_kv = N_KV_HEADS * HEAD_DIM
    idx_q = N_IDX_HEADS * IDX_DIM
    l = 0

    widths = (att_q, att_kv, att_kv, idx_q, IDX_DIM, N_IDX_HEADS, d_inner, conv_dim, n_ssd_heads, d, d)
    bounds = np.concatenate([[0], np.cumsum(widths)])
    assert bounds[-1] == w_in.shape[2]
    cols = [w_in[l][:, bounds[i]:bounds[i + 1]].astype(BF16) for i in range(len(widths))]
    w_q, w_k, w_v, w_qi, w_ki, w_wi, w_z, w_xbc, w_dt, w_ga, w_gs = cols
    row1 = lambda a: a.reshape(1, -1)
    p = {
        "norm_ffn1": row1(norm_ffn1[l]), "w_ffn1_in": w_ffn1_in[l].astype(BF16), "w_ffn1_out": w_ffn1_out[l].astype(BF16),
        "norm_mix": row1(norm_mix[l]),
        "w_att": jnp.concatenate([w_q, w_k, w_v, w_qi, _pad_cols(w_ki, LANES), _pad_cols(w_wi, LANES)], axis=1),
        "seg_att": [(att_q, BF16, HEAD_DIM ** -0.5 * LOG2E), (att_kv, F32, 1.0), (att_kv, F32, 1.0),
                    (idx_q, BF16, 1.0), (IDX_DIM, F32, 1.0), (LANES, F32, 1.0)],
        "w_ssd": jnp.concatenate([w_z, w_xbc, _pad_cols(w_dt, LANES)], axis=1),
        "seg_ssd": [(d_inner, F32, 1.0), (conv_dim, F32, 1.0), (LANES, F32, 1.0)],
        "w_gate": jnp.concatenate([w_ga, w_gs], axis=1),
        "w_attn_out": w_attn_out[l].astype(BF16), "w_ssd_out": w_ssd_out[l].astype(BF16), "w_out": w_out[l].astype(BF16),
        "norm_ffn2": row1(norm_ffn2[l]), "w_ffn2_in": w_ffn2_in[l].astype(BF16), "w_ffn2_out": w_ffn2_out[l].astype(BF16),
    }
    final_g = row1(norm_final)
    ssd_args = (conv_w[l], conv_b[l], a_log[l], dt_bias[l], d_skip[l], norm_ssd[l])

    ada = _ada(jnp.concatenate([c_prompt, c_sample], axis=0), w_ada[l], b_ada[l])
    ada_p = [a.reshape(batch, 1, d) for a in jnp.split(ada[:batch], 9, axis=1)]
    ada_s = [a.reshape(1, db, d) for a in jnp.split(ada[batch:], 9, axis=1)]

    tq = 128
    n_chunks = seq // SSD_CHUNK
    bias_tab = _bias_table(rel_bias, tq, seq)

    def attend_p(q, k, v, qi, ki, wi):
        return _dsa_prompt(q, qi, wi, k, v, ki, bias_tab, batch=batch, seq=seq, tq=tq)

    def ssd_p(z, xbc, dt):
        r3 = lambda a: a.reshape(batch * n_chunks, SSD_CHUNK, a.shape[-1])
        y, ssm, conv = _ssd(r3(xbc), r3(z), r3(dt), *ssd_args, n_seq=batch, n_chunks=n_chunks, rows=SSD_CHUNK)
        return y.reshape(batch * seq, d_inner), ssm, conv

    yp, (k_p, v_p, ki_p, ssm_p, conv_p) = _trunk(x_prompt.reshape(batch * seq, d), ada_p, seq, p, 512,
                                                 attend_p, ssd_p, final_g)

    pg = min(64, n_pages)
    n_top_s = min(TOPK_MAX, (n_pages * PAGE_SIZE + 1) // 4)
    bias_tab_s = _bias_table_s(rel_bias, n_pages)

    def attend_s(q, k, v, qi, ki, wi):
        wi_bc = jnp.broadcast_to(wi[:, :N_IDX_HEADS, None], (db, N_IDX_HEADS, LANES))
        scores = _dsa_s_scores(page_table, qi.reshape(db, N_IDX_HEADS, IDX_DIM), wi_bc,
                               cache_kidx[l].astype(BF16), pg=min(2 * pg, n_pages))
        mask = _dsa_s_select(scores.reshape(n_pages, db, PAGE_SIZE), qi, ki, wi, n_top=n_top_s)
        att = _dsa_s_attend(page_table, mask, bias_tab_s, q.reshape(db, N_HEADS, HEAD_DIM),
                            k.reshape(db, 1, att_kv), v.reshape(db, 1, att_kv),
                            cache_k[l].reshape(n_pool, PAGE_SIZE, att_kv),
                            cache_v[l].reshape(n_pool, PAGE_SIZE, att_kv), pg=pg)
        return att.reshape(db, att_q)

    def ssd_s(z, xbc, dt):
        r3 = lambda a: a.reshape(db, 1, a.shape[-1])
        y, ssm, conv = _ssd(r3(xbc), r3(z), r3(dt), *ssd_args, n_seq=db, n_chunks=1, rows=1,
                            conv0=state_conv[l], ssm0=state_ssm[l].reshape(db, d_inner, D_STATE))
        return y.reshape(db, d_inner), ssm, conv

    ys, (k_s, v_s, ki_s, ssm_s, conv_s) = _trunk(x_sample.reshape(db, d), ada_s, db, p, db,
                                                 attend_s, ssd_s, final_g)

    st = lambda a, *shape: a.reshape((1,) + shape)
    return (yp.reshape(batch, seq, d), ys.reshape(db, 1, d),
            st(k_p, batch, seq, N_KV_HEADS, HEAD_DIM), st(v_p, batch, seq, N_KV_HEADS, HEAD_DIM),
            st(ki_p, batch, seq, IDX_DIM),
            st(ssm_p, batch, n_ssd_heads, SSD_HEAD_DIM, D_STATE), st(conv_p, batch, conv_w.shape[1] - 1, conv_dim),
            st(k_s, db, 1, N_KV_HEADS, HEAD_DIM), st(v_s, db, 1, N_KV_HEADS, HEAD_DIM), st(ki_s, db, 1, IDX_DIM),
            st(ssm_s, db, n_ssd_heads, SSD_HEAD_DIM, D_STATE), st(conv_s, db, conv_w.shape[1] - 1, conv_dim))
```
